```python
import math
import jax, jax.numpy as jnp
from jax import lax
import numpy as np

D_MODEL = 2048
BATCH = 2
SEQ = 8192
DEPTH = 4

GRID_W = 64
CTX_LEN = 256
EPS = 1e-6

S5_WIDTH = D_MODEL // 4
S5_GROUP = 16
S5_GROUPS = S5_WIDTH // S5_GROUP
S5_STATE = 64

HY_WIDTH = D_MODEL // 4
HY_ORDER = 2
HY_POS_FREQS = 16
HY_POS_DIM = 1 + 2 * HY_POS_FREQS
HY_FILTER_HIDDEN = 64
HY_DECAY_TARGET = 1e-2
HY_SHORT_DECAY_PCT = 0.3
HY_LONG_DECAY_PCT = 1.5

GLA_HEADS = 4
GLA_DK = D_MODEL // 16
GLA_DV = D_MODEL // 8
GLA_KEY = GLA_HEADS * GLA_DK
GLA_VAL = GLA_HEADS * GLA_DV
GLA_GATE_RANK = 16
GLA_GATE_TEMP = 16.0
GLA_CHUNK = 64

MIX_WIDTH = S5_WIDTH + HY_WIDTH + GLA_VAL
N_BRANCH = 3
FF_HIDDEN = 11 * D_MODEL // 4

C_S5 = 0
C_GK = C_S5 + S5_WIDTH
C_GV = C_GK + GLA_KEY
C_GG = C_GV + GLA_VAL
C_GQ = C_GG + 2 * GLA_GATE_RANK
STATE_COLS = C_GQ
C_GR = C_GQ + GLA_KEY
C_HY = C_GR + GLA_VAL
C_MG = C_HY + (HY_ORDER + 1) * HY_WIDTH
IN_WIDTH = C_MG + N_BRANCH * D_MODEL

kernel_name = "hybrid_s5_hyena_gla_dit_block"

F32 = jnp.float32


def rmsnorm(x, g):
    xf = x.astype(F32)
    y = xf * lax.rsqrt(jnp.mean(xf * xf, axis=-1, keepdims=True) + EPS)
    return (y * g.astype(F32)).astype(x.dtype)


def dwconv3(x, w, b):
    xp = jnp.pad(x, ((0, 0), (1, 1), (0, 0)))
    return xp[:, :-2] * w[0] + xp[:, 1:-1] * w[1] + xp[:, 2:] * w[2] + b


def to_col_major(t):
    bsz, n = t.shape[:2]
    rows = n // GRID_W
    return t.reshape(bsz, rows, GRID_W, *t.shape[2:]).swapaxes(1, 2).reshape(bsz, n, *t.shape[2:])


def from_col_major(t):
    bsz, n = t.shape[:2]
    rows = n // GRID_W
    return t.reshape(bsz, GRID_W, rows, *t.shape[2:]).swapaxes(1, 2).reshape(bsz, n, *t.shape[2:])


def flip_seq(t):
    return None if t is None else jnp.flip(t, axis=1)


def _linear_op(e1, e2):
    a1, b1 = e1
    a2, b2 = e2
    return a1 * a2, a2 * b1 + b2


def s5_scan(bu, a_bar, reverse):
    a = jnp.broadcast_to(a_bar, bu.shape)
    _, xs = lax.associative_scan(_linear_op, (a, bu), reverse=reverse, axis=1)
    return xs


def s5_carry(lam_dt, s0, n, reverse):
    steps = (jnp.arange(n, 0, -1) if reverse else jnp.arange(1, n + 1)).astype(F32)
    decay = jnp.exp(lam_dt[None] * steps[:, None, None])
    return decay[None] * s0[:, None]


def s5_drive(u, b_bar):
    ug = u.astype(F32).reshape(u.shape[0], u.shape[1], S5_GROUPS, S5_GROUP).astype(jnp.complex64)
    return jnp.einsum('blgh,gnh->blgn', ug, b_bar)


def s5_readout(c_mat, xs):
    y = jnp.real(jnp.einsum('ghn,blgn->blgh', c_mat, xs))
    return y.reshape(y.shape[0], y.shape[1], S5_WIDTH)


def s5_mixer(ux, uc, p, ctx_out):
    yx, yc = [], []
    for d in range(2):
        rev = d == 1
        lam = lax.complex(p['s5_a_re'][d].astype(F32), p['s5_a_im'][d].astype(F32))
        lam_dt = lam * jnp.exp(p['s5_log_step'][d].astype(F32))[:, None]
        a_bar = jnp.exp(lam_dt)
        b_mat = lax.complex(p['s5_b_re'][d].astype(F32), p['s5_b_im'][d].astype(F32))
        b_bar = ((a_bar - 1.0) / lam)[..., None] * b_mat
        c_mat = lax.complex(p['s5_c_re'][d].astype(F32), p['s5_c_im'][d].astype(F32))
        xs_c = s5_scan(s5_drive(uc, b_bar), a_bar, rev)
        s0 = xs_c[:, 0] if rev else xs_c[:, -1]
        xs_x = s5_scan(s5_drive(ux, b_bar), a_bar, rev) + s5_carry(lam_dt, s0, ux.shape[1], rev)
        yx.append(s5_readout(c_mat, xs_x))
        if ctx_out:
            yc.append(s5_readout(c_mat, xs_c))

    def finish(ys, u):
        y = ys[0] + ys[1] + p['s5_d'].astype(F32) * u.astype(F32)
        y = jax.nn.gelu(y)
        y = y * jax.nn.sigmoid(y @ p['s5_glu_w'].astype(F32) + p['s5_glu_b'].astype(F32))
        return y.astype(u.dtype)

    return finish(yx, ux), (finish(yc, uc) if ctx_out else None)


def hyena_filters(n, p):
    pos = jnp.arange(n, dtype=F32)
    t = pos / max(n - 1, 1)
    freqs = jnp.linspace(1e-4, HY_POS_FREQS - 1, HY_POS_FREQS, dtype=F32)
    ang = (2.0 * math.pi / n) * pos[:, None] * freqs[None]
    feats = jnp.concatenate([t[:, None], jnp.cos(ang), -jnp.sin(ang)], axis=-1)
    h = jnp.sin(p['hy_f_freq1'].astype(F32) * (feats @ p['hy_f_w1'].astype(F32) + p['hy_f_b1'].astype(F32)))
    h = jnp.sin(p['hy_f_freq2'].astype(F32) * (h @ p['hy_f_w2'].astype(F32) + p['hy_f_b2'].astype(F32)))
    h = (h @ p['hy_f_w3'].astype(F32) + p['hy_f_b3'].astype(F32)).reshape(n, HY_ORDER, 2, HY_WIDTH)
    rates = jnp.abs(jnp.linspace(math.log(HY_DECAY_TARGET) / HY_LONG_DECAY_PCT,
                                 math.log(HY_DECAY_TARGET) / HY_SHORT_DECAY_PCT, HY_WIDTH, dtype=F32))
    h = h * jnp.exp(-t[:, None] * rates)[:, None, None, :]
    fwd, bwd = h[:, :, 0], h[:, :, 1]
    filt = jnp.concatenate([fwd, jnp.zeros_like(fwd[:1]), jnp.flip(bwd[1:], axis=0)], axis=0)
    filt = filt / (jnp.sum(jnp.abs(filt), axis=0, keepdims=True) + EPS)
    return jnp.fft.rfft(filt, axis=0)


def hyena_seq(z, p):
    n = z.shape[1]
    zc = dwconv3(z, p['hy_conv_w'], p['hy_conv_b']).astype(F32)
    v, x1, x2 = jnp.split(zc, 3, axis=-1)
    filt_f = hyena_filters(n, p)
    y = v
    for o, gate in enumerate((x1, x2)):
        yf = jnp.fft.rfft(y, n=2 * n, axis=1)
        conv = jnp.fft.irfft(yf * filt_f[None, :, o], n=2 * n, axis=1)[:, :n]
        y = gate * (conv + y * p['hy_bias'][o].astype(F32))
    return y.astype(z.dtype)


def gla_chunked(q, k, v, g, s0):
    bsz, n = k.shape[:2]
    nc = n // GLA_CHUNK
    chunk = lambda t: t.reshape(bsz, nc, GLA_CHUNK, *t.shape[2:])
    k, v, g = chunk(k), chunk(v), chunk(g)
    b = jnp.cumsum(g, axis=2)
    b_last = b[:, :, -1]
    kv = jnp.einsum('bnjhd,bnjhe->bnhde', k * jnp.exp(b_last[:, :, None] - b), v)
    step = lambda s, inp: (jnp.exp(inp[0])[..., None] * s + inp[1], s)
    s_fin, s_prev = lax.scan(step, s0, (jnp.moveaxis(b_last, 1, 0), jnp.moveaxis(kv, 1, 0)))
    if q is None:
        return None, s_fin
    q_in = chunk(q) * jnp.exp(b)
    scores = jnp.einsum('bnihd,bnjhd->bnhij', q_in, k * jnp.exp(-b))
    lower = jnp.tril(jnp.ones((GLA_CHUNK, GLA_CHUNK), dtype=bool))
    scores = jnp.where(lower, scores, 0.0)
    o = (jnp.einsum('bnhij,bnjhe->bnihe', scores, v)
         + jnp.einsum('bnihd,bnhde->bnihe', q_in, jnp.moveaxis(s_prev, 0, 1)))
    return o.reshape(bsz, n, *o.shape[3:]), s_fin


def gla_mixer(px, pc, p, ctx_out):
    heads = lambda t, dh: t.astype(F32).reshape(t.shape[0], t.shape[1], GLA_HEADS, dh)
    scale = GLA_DK ** -0.5

    def gate(lr, d):
        pre = lr[..., d * GLA_GATE_RANK:(d + 1) * GLA_GATE_RANK] @ p['gla_wg'][d] + p['gla_bg'][d]
        return heads(jax.nn.log_sigmoid(pre.astype(F32)) / GLA_GATE_TEMP, GLA_DK)

    qx = heads(to_col_major(px[..., C_GQ:C_GR]), GLA_DK) * scale
    kx = heads(to_col_major(px[..., C_GK:C_GV]), GLA_DK)
    vx = heads(to_col_major(px[..., C_GV:C_GG]), GLA_DV)
    lx = to_col_major(px[..., C_GG:C_GQ])
    qc = heads(pc[..., C_GQ:C_GR], GLA_DK) * scale if ctx_out else None
    kc = heads(pc[..., C_GK:C_GV], GLA_DK)
    vc = heads(pc[..., C_GV:C_GG], GLA_DV)
    lc = pc[..., C_GG:C_GQ]
    zero = jnp.zeros((kc.shape[0], GLA_HEADS, GLA_DK, GLA_DV), F32)
    ox, oc = [], []
    for d in range(2):
        f = flip_seq if d == 1 else (lambda t: t)
        o_c, s_c = gla_chunked(f(qc), f(kc), f(vc), f(gate(lc, d)), zero)
        o_x, _ = gla_chunked(f(qx), f(kx), f(vx), f(gate(lx, d)), s_c)
        ox.append(f(o_x))
        if ctx_out:
            oc.append(f(o_c))

    def finish(o, r):
        o = rmsnorm(o, p['gla_norm_g']).reshape(o.shape[0], o.shape[1], GLA_VAL)
        return (o * jax.nn.silu(r.astype(F32))).astype(r.dtype)

    yx = finish(from_col_major(ox[0] + ox[1]), px[..., C_GR:C_HY])
    yc = finish(oc[0] + oc[1], pc[..., C_GR:C_HY]) if ctx_out else None
    return yx, yc


def merge(ya, yb, yc, gate_logits, p):
    wb = p['w_branch']
    g = jax.nn.sigmoid(gate_logits.astype(F32)).astype(ya.dtype)
    m = (g[..., :D_MODEL] * (ya @ wb[:S5_WIDTH])
         + g[..., D_MODEL:2 * D_MODEL] * (yb @ wb[S5_WIDTH:S5_WIDTH + HY_WIDTH])
         + g[..., 2 * D_MODEL:] * (yc @ wb[S5_WIDTH + HY_WIDTH:]))
    return m @ p['w_out']


def token_mixer(hx, hc, p, ctx_out):
    px = hx @ p['w_in']
    pc = hc @ (p['w_in'] if ctx_out else p['w_in'][:, :STATE_COLS])
    ya_x, ya_c = s5_mixer(px[..., C_S5:C_GK], pc[..., C_S5:C_GK], p, ctx_out)
    yc_x, yc_c = gla_mixer(px, pc, p, ctx_out)
    yb_x = hyena_seq(px[..., C_HY:C_MG], p)
    out_x = merge(ya_x, yb_x, yc_x, px[..., C_MG:], p)
    if not ctx_out:
        return out_x, None
    yb_c = hyena_seq(pc[..., C_HY:C_MG], p)
    out_c = merge(ya_c, yb_c, yc_c, pc[..., C_MG:], p)
    return out_x, out_c


def conv_ffn(h, w_up, conv_w, conv_b, w_down):
    a, b = jnp.split(h @ w_up, 2, axis=-1)
    a = dwconv3(a, conv_w, conv_b)
    return (jax.nn.silu(a) * b) @ w_down


def setup_inputs(seed: int = 0) -> dict:
    key = jax.random.key(seed)
    ks = iter(jax.random.split(key, 64))

    def nrm(shape, scale=1.0):
        return scale * jax.random.normal(next(ks), shape, F32)

    def gain(shape):
        return 1.0 + nrm(shape, 0.02)

    G, N, H = S5_GROUPS, S5_STATE, S5_GROUP
    return {
        "x": nrm((BATCH, SEQ, D_MODEL)),
        "c": nrm((BATCH, D_MODEL)),
        "ctx": nrm((BATCH, CTX_LEN, D_MODEL)),
        "c_ctx": nrm((D_MODEL,)),
        "w_mod": nrm((DEPTH, D_MODEL, 6 * D_MODEL), 0.5 * D_MODEL ** -0.5),
        "b_mod": nrm((DEPTH, 6 * D_MODEL), 0.01),
        "norm1_g": gain((DEPTH, D_MODEL)),
        "norm2_g": gain((DEPTH, D_MODEL)),
        "w_in": nrm((DEPTH, D_MODEL, IN_WIDTH), D_MODEL ** -0.5),
        "s5_a_re": -0.5 + nrm((DEPTH, 2, G, N), 0.01),
        "s5_a_im": math.pi * jnp.arange(N, dtype=F32) + nrm((DEPTH, 2, G, N), 0.01),
        "s5_log_step": jax.random.uniform(next(ks), (DEPTH, 2, G), F32, math.log(1e-3), math.log(1e-1)),
        "s5_b_re": nrm((DEPTH, 2, G, N, H), (2 * H) ** -0.5),
        "s5_b_im": nrm((DEPTH, 2, G, N, H), (2 * H) ** -0.5),
        "s5_c_re": nrm((DEPTH, 2, G, H, N), N ** -0.5),
        "s5_c_im": nrm((DEPTH, 2, G, H, N), N ** -0.5),
        "s5_d": nrm((DEPTH, S5_WIDTH)),
        "s5_glu_w": nrm((DEPTH, S5_WIDTH, S5_WIDTH), S5_WIDTH ** -0.5),
        "s5_glu_b": nrm((DEPTH, S5_WIDTH), 0.01),
        "hy_conv_w": nrm((DEPTH, 3, (HY_ORDER + 1) * HY_WIDTH), 0.5),
        "hy_conv_b": nrm((DEPTH, (HY_ORDER + 1) * HY_WIDTH), 0.01),
        "hy_f_w1": nrm((DEPTH, HY_POS_DIM, HY_FILTER_HIDDEN), HY_POS_DIM ** -0.5),
        "hy_f_b1": nrm((DEPTH, HY_FILTER_HIDDEN), 0.1),
        "hy_f_freq1": 1.0 + nrm((DEPTH, HY_FILTER_HIDDEN), 0.1),
        "hy_f_w2": nrm((DEPTH, HY_FILTER_HIDDEN, HY_FILTER_HIDDEN), HY_FILTER_HIDDEN ** -0.5),
        "hy_f_b2": nrm((DEPTH, HY_FILTER_HIDDEN), 0.1),
        "hy_f_freq2": 1.0 + nrm((DEPTH, HY_FILTER_HIDDEN), 0.1),
        "hy_f_w3": nrm((DEPTH, HY_FILTER_HIDDEN, HY_ORDER * 2 * HY_WIDTH), HY_FILTER_HIDDEN ** -0.5),
        "hy_f_b3": nrm((DEPTH, HY_ORDER * 2 * HY_WIDTH), 0.01),
        "hy_bias": nrm((DEPTH, HY_ORDER, HY_WIDTH)),
        "gla_wg": nrm((DEPTH, 2, GLA_GATE_RANK, GLA_KEY), GLA_GATE_RANK ** -0.5),
        "gla_bg": nrm((DEPTH, 2, GLA_KEY), 0.01),
        "gla_norm_g": gain((DEPTH, GLA_DV)),
        "w_branch": jnp.concatenate([nrm((DEPTH, S5_WIDTH, D_MODEL), S5_WIDTH ** -0.5),
                                     nrm((DEPTH, HY_WIDTH, D_MODEL), HY_WIDTH ** -0.5),
                                     nrm((DEPTH, GLA_VAL, D_MODEL), GLA_VAL ** -0.5)], axis=1),
        "w_out": nrm((DEPTH, D_MODEL, D_MODEL), D_MODEL ** -0.5),
        "ff_w_up": nrm((DEPTH, D_MODEL, 2 * FF_HIDDEN), D_MODEL ** -0.5),
        "ff_conv_w": nrm((DEPTH, 3, FF_HIDDEN), 0.5),
        "ff_conv_b": nrm((DEPTH, FF_HIDDEN), 0.01),
        "ff_w_down": nrm((DEPTH, FF_HIDDEN, D_MODEL), FF_HIDDEN ** -0.5),
        "final_norm_g": gain((D_MODEL,)),
    }


def reference(x, c, ctx, c_ctx, w_mod, b_mod, norm1_g, norm2_g, w_in,
              s5_a_re, s5_a_im, s5_log_step, s5_b_re, s5_b_im, s5_c_re, s5_c_im,
              s5_d, s5_glu_w, s5_glu_b,
              hy_conv_w, hy_conv_b, hy_f_w1, hy_f_b1, hy_f_freq1, hy_f_w2, hy_f_b2,
              hy_f_freq2, hy_f_w3, hy_f_b3, hy_bias,
              gla_wg, gla_bg, gla_norm_g, w_branch, w_out,
              ff_w_up, ff_conv_w, ff_conv_b, ff_w_down, final_norm_g):
    dt = x.dtype
    sc = jax.nn.silu(c.astype(F32)).astype(dt)
    scc = jax.nn.silu(c_ctx.astype(F32)).astype(dt)
    for l in range(DEPTH):
        ctx_out = l < DEPTH - 1
        mod_x = (sc @ w_mod[l] + b_mod[l])[:, None, :]
        mod_c = scc @ w_mod[l] + b_mod[l]
        sh1, s1, g1, sh2, s2, g2 = jnp.split(mod_x, 6, axis=-1)
        csh1, cs1, cg1, csh2, cs2, cg2 = jnp.split(mod_c, 6, axis=-1)
        p = dict(w_in=w_in[l], s5_a_re=s5_a_re[l], s5_a_im=s5_a_im[l], s5_log_step=s5_log_step[l],
                 s5_b_re=s5_b_re[l], s5_b_im=s5_b_im[l], s5_c_re=s5_c_re[l], s5_c_im=s5_c_im[l],
                 s5_d=s5_d[l], s5_glu_w=s5_glu_w[l], s5_glu_b=s5_glu_b[l],
                 hy_conv_w=hy_conv_w[l], hy_conv_b=hy_conv_b[l], hy_f_w1=hy_f_w1[l], hy_f_b1=hy_f_b1[l],
                 hy_f_freq1=hy_f_freq1[l], hy_f_w2=hy_f_w2[l], hy_f_b2=hy_f_b2[l],
                 hy_f_freq2=hy_f_freq2[l], hy_f_w3=hy_f_w3[l], hy_f_b3=hy_f_b3[l], hy_bias=hy_bias[l],
                 gla_wg=gla_wg[l], gla_bg=gla_bg[l], gla_norm_g=gla_norm_g[l],
                 w_branch=w_branch[l], w_out=w_out[l])
        hx = rmsnorm(x, norm1_g[l]) * (1 + s1) + sh1
        hc = rmsnorm(ctx, norm1_g[l]) * (1 + cs1) + csh1
        ox, oc = token_mixer(hx, hc, p, ctx_out)
        x = x + g1 * ox
        hx = rmsnorm(x, norm2_g[l]) * (1 + s2) + sh2
        x = x + g2 * conv_ffn(hx, ff_w_up[l], ff_conv_w[l], ff_conv_b[l], ff_w_down[l])
        if ctx_out:
            ctx = ctx + cg1 * oc
            hc = rmsnorm(ctx, norm2_g[l]) * (1 + cs2) + csh2
            ctx = ctx + cg2 * conv_ffn(hc, ff_w_up[l], ff_conv_w[l], ff_conv_b[l], ff_w_down[l])
    return rmsnorm(x, final_norm_g)
```

```python
import functools
import math

import jax
import jax.numpy as jnp
from jax import lax
from jax.experimental import pallas as pl
from jax.experimental.pallas import tpu as pltpu

F32 = jnp.float32
BF16 = jnp.bfloat16
EPS = 1e-6
GRID_W = 64
S5_GROUP = 16
GLA_CHUNK = 64
GLA_GATE_TEMP = 16.0
HY_POS_FREQS = 16
HY_DECAY_TARGET = 1e-2
HY_SHORT_DECAY_PCT = 0.3
HY_LONG_DECAY_PCT = 1.5
LANE = 128
VMEM_LIMIT = 56 * 1024 * 1024


def _pick_tile(n, cap, mult=LANE):
    best = None
    for t in range(mult, min(n, cap) + 1, mult):
        if n % t == 0:
            best = t
    assert best is not None, (n, cap, mult)
    return best


def _cparams(*sem):
    return pltpu.CompilerParams(dimension_semantics=sem, vmem_limit_bytes=VMEM_LIMIT)


def _mod_kernel(r_ref, w_ref, b_ref, o_ref):
    r = r_ref[...]
    s = r * jax.nn.sigmoid(r)
    o_ref[0] = jnp.dot(s, w_ref[0], preferred_element_type=F32,
                       precision=lax.Precision.HIGHEST) + b_ref[0]


def modulation(rows, w_mod, b_mod):
    depth, d, n = w_mod.shape
    tn = _pick_tile(n, 1024)
    return pl.pallas_call(
        _mod_kernel,
        out_shape=jax.ShapeDtypeStruct((depth, 8, n), F32),
        grid=(depth, n // tn),
        in_specs=[pl.BlockSpec((8, d), lambda l, j: (0, 0)),
                  pl.BlockSpec((1, d, tn), lambda l, j: (l, 0, j)),
                  pl.BlockSpec((1, 1, tn), lambda l, j: (l, 0, j))],
        out_specs=pl.BlockSpec((1, 8, tn), lambda l, j: (l, 0, j)),
        compiler_params=_cparams("parallel", "parallel"),
        name="modulation",
    )(rows, w_mod, b_mod.reshape(depth, 1, n))


def _norm_mm_kernel(x_ref, gs_ref, sh_ref, w_ref, o_ref, h_ref):
    @pl.when(pl.program_id(1) == 0)
    def _():
        x = x_ref[...]
        ms = jnp.mean(x * x, axis=-1, keepdims=True)
        h_ref[...] = (x * lax.rsqrt(ms + EPS) * gs_ref[0] + sh_ref[0]).astype(BF16)

    o_ref[...] = jnp.dot(h_ref[...], w_ref[...], preferred_element_type=F32).astype(o_ref.dtype)


def norm_matmul(x, gs, sh, w, out_dtype=F32):
    m, d = x.shape
    n = w.shape[1]
    nb = gs.shape[0]
    tm = _pick_tile(m // nb, 1024, 8)
    tn = _pick_tile(n, 1024)
    tpb = (m // nb) // tm
    return pl.pallas_call(
        _norm_mm_kernel,
        out_shape=jax.ShapeDtypeStruct((m, n), out_dtype),
        grid=(m // tm, n // tn),
        in_specs=[pl.BlockSpec((tm, d), lambda i, j: (i, 0)),
                  pl.BlockSpec((1, 1, d), lambda i, j: (i // tpb, 0, 0)),
                  pl.BlockSpec((1, 1, d), lambda i, j: (i // tpb, 0, 0)),
                  pl.BlockSpec((d, tn), lambda i, j: (0, j))],
        out_specs=pl.BlockSpec((tm, tn), lambda i, j: (i, j)),
        scratch_shapes=[pltpu.VMEM((tm, d), BF16)],
        compiler_params=_cparams("parallel", "arbitrary"),
        name="norm_matmul",
    )(x, gs, sh, w)


def _mm_res_kernel(a_ref, w_ref, r_ref, g_ref, o_ref):
    acc = jnp.dot(a_ref[...], w_ref[...], preferred_element_type=F32)
    o_ref[...] = r_ref[...] + g_ref[0] * acc


def matmul_residual(a, w, res, g):
    m, k = a.shape
    n = w.shape[1]
    nb = g.shape[0]
    tm = _pick_tile(m // nb, 1024, 8)
    tn = _pick_tile(n, 512)
    tpb = (m // nb) // tm
    return pl.pallas_call(
        _mm_res_kernel,
        out_shape=jax.ShapeDtypeStruct((m, n), F32),
        grid=(m // tm, n // tn),
        in_specs=[pl.BlockSpec((tm, k), lambda i, j: (i, 0)),
                  pl.BlockSpec((k, tn), lambda i, j: (0, j)),
                  pl.BlockSpec((tm, tn), lambda i, j: (i, j)),
                  pl.BlockSpec((1, 1, tn), lambda i, j: (i // tpb, 0, j))],
        out_specs=pl.BlockSpec((tm, tn), lambda i, j: (i, j)),
        compiler_params=_cparams("parallel", "arbitrary"),
        name="matmul_residual",
    )(a, w, res, g)


def _merge_kernel(x_ref, gs_ref, sh_ref, ya_ref, yb_ref, yc_ref, wg_ref, wba_ref, wbb_ref, wbc_ref,
                  o_ref, h_ref):
    @pl.when(pl.program_id(1) == 0)
    def _():
        x = x_ref[...]
        ms = jnp.mean(x * x, axis=-1, keepdims=True)
        h_ref[...] = (x * lax.rsqrt(ms + EPS) * gs_ref[0] + sh_ref[0]).astype(BF16)

    h = h_ref[...]
    m = None
    for i, (y_ref, wb_ref) in enumerate(((ya_ref, wba_ref), (yb_ref, wbb_ref), (yc_ref, wbc_ref))):
        gate = jax.nn.sigmoid(jnp.dot(h, wg_ref[i], preferred_element_type=F32))
        br = jnp.dot(y_ref[...].astype(BF16), wb_ref[...], preferred_element_type=F32)
        m = gate * br if m is None else m + gate * br
    o_ref[...] = m.astype(o_ref.dtype)


def merge(x, gs, sh, ya, yb, yc, wg, wba, wbb, wbc):
    m, d = x.shape
    nb = gs.shape[0]
    tm = _pick_tile(m // nb, 512, 8)
    tn = _pick_tile(d, 512)
    tpb = (m // nb) // tm
    row = lambda i, j: (i, 0)
    return pl.pallas_call(
        _merge_kernel,
        out_shape=jax.ShapeDtypeStruct((m, d), BF16),
        grid=(m // tm, d // tn),
        in_specs=[pl.BlockSpec((tm, d), row),
                  pl.BlockSpec((1, 1, d), lambda i, j: (i // tpb, 0, 0)),
                  pl.BlockSpec((1, 1, d), lambda i, j: (i // tpb, 0, 0)),
                  pl.BlockSpec((tm, ya.shape[1]), row),
                  pl.BlockSpec((tm, yb.shape[1]), row),
                  pl.BlockSpec((tm, yc.shape[1]), row),
                  pl.BlockSpec((3, d, tn), lambda i, j: (0, 0, j)),
                  pl.BlockSpec((wba.shape[0], tn), lambda i, j: (0, j)),
                  pl.BlockSpec((wbb.shape[0], tn), lambda i, j: (0, j)),
                  pl.BlockSpec((wbc.shape[0], tn), lambda i, j: (0, j))],
        out_specs=pl.BlockSpec((tm, tn), lambda i, j: (i, j)),
        scratch_shapes=[pltpu.VMEM((tm, d), BF16)],
        compiler_params=_cparams("parallel", "arbitrary"),
        name="merge",
    )(x, gs, sh, ya, yb, yc, wg, wba, wbb, wbc)


def _dwconv3_tile(a, prev8, next8, first, last, w_ref, cb_ref):
    tr = a.shape[0]
    row = lax.broadcasted_iota(jnp.int32, a.shape, 0)
    before = jnp.where(first, 0.0, prev8[7:8, :])
    after = jnp.where(last, 0.0, next8[0:1, :])
    prev = jnp.where(row == 0, before, pltpu.roll(a, 1, 0))
    nxt = jnp.where(row == tr - 1, after, pltpu.roll(a, tr - 1, 0))
    return prev * w_ref[0:1, :] + a * w_ref[1:2, :] + nxt * w_ref[2:3, :] + cb_ref[...]


def _ffn_act_kernel(a_ref, ap_ref, an_ref, b_ref, w_ref, cb_ref, o_ref):
    i = pl.program_id(1)
    conv = _dwconv3_tile(a_ref[0], ap_ref[0], an_ref[0], i == 0, i == pl.num_programs(1) - 1, w_ref, cb_ref)
    o_ref[0] = (conv * jax.nn.sigmoid(conv) * b_ref[0]).astype(o_ref.dtype)


def ffn_act(ab, conv_w, conv_b, bsz):
    m, f2 = ab.shape
    f = f2 // 2
    n = m // bsz
    tc = _pick_tile(f, 512)
    tr = _pick_tile(n, 1024, 8)
    nf = f // tc
    r8 = tr // 8
    last8 = n // 8 - 1
    ab3 = ab.reshape(bsz, n, f2)
    out = pl.pallas_call(
        _ffn_act_kernel,
        out_shape=jax.ShapeDtypeStruct((bsz, n, f), BF16),
        grid=(bsz, n // tr, nf),
        in_specs=[pl.BlockSpec((1, tr, tc), lambda b, i, j: (b, i, j)),
                  pl.BlockSpec((1, 8, tc), lambda b, i, j: (b, jnp.maximum(i * r8 - 1, 0), j)),
                  pl.BlockSpec((1, 8, tc), lambda b, i, j: (b, jnp.minimum((i + 1) * r8, last8), j)),
                  pl.BlockSpec((1, tr, tc), lambda b, i, j: (b, i, j + nf)),
                  pl.BlockSpec((3, tc), lambda b, i, j: (0, j)),
                  pl.BlockSpec((1, tc), lambda b, i, j: (0, j))],
        out_specs=pl.BlockSpec((1, tr, tc), lambda b, i, j: (b, i, j)),
        compiler_params=_cparams("parallel", "parallel", "parallel"),
        name="ffn_act",
    )(ab3, ab3, ab3, ab3, conv_w, conv_b.reshape(1, f))
    return out.reshape(m, f)


def _rmsnorm_kernel(x_ref, g_ref, o_ref):
    x = x_ref[...]
    ms = jnp.mean(x * x, axis=-1, keepdims=True)
    o_ref[...] = x * lax.rsqrt(ms + EPS) * g_ref[...]


def final_rmsnorm(x, g):
    m, d = x.shape
    tm = _pick_tile(m, 1024, 8)
    return pl.pallas_call(
        _rmsnorm_kernel,
        out_shape=jax.ShapeDtypeStruct((m, d), F32),
        grid=(m // tm,),
        in_specs=[pl.BlockSpec((tm, d), lambda i: (i, 0)), pl.BlockSpec((1, d), lambda i: (0, 0))],
        out_specs=pl.BlockSpec((tm, d), lambda i: (i, 0)),
        compiler_params=_cparams("parallel"),
        name="final_rmsnorm",
    )(x, g.reshape(1, d))


def _rmsnorm(x, g):
    y = x * lax.rsqrt(jnp.mean(x * x, axis=-1, keepdims=True) + EPS)
    return y * g


def _dwconv3(x, w, b):
    xp = jnp.pad(x, ((0, 0), (1, 1), (0, 0)))
    return xp[:, :-2] * w[0] + xp[:, 1:-1] * w[1] + xp[:, 2:] * w[2] + b


def _to_col_major(t):
    bsz, n = t.shape[:2]
    rows = n // GRID_W
    return t.reshape(bsz, rows, GRID_W, *t.shape[2:]).swapaxes(1, 2).reshape(bsz, n, *t.shape[2:])


def _from_col_major(t):
    bsz, n = t.shape[:2]
    rows = n // GRID_W
    return t.reshape(bsz, GRID_W, rows, *t.shape[2:]).swapaxes(1, 2).reshape(bsz, n, *t.shape[2:])


def _linear_op(e1, e2):
    a1, b1 = e1
    a2, b2 = e2
    return a1 * a2, a2 * b1 + b2


def _s5_mixer(ux, uc, p, ctx_out):
    groups, nstate = p['s5_a_re'].shape[1:]
    width = ux.shape[-1]

    def drive(u, b_bar):
        ug = u.reshape(u.shape[0], u.shape[1], groups, S5_GROUP).astype(jnp.complex64)
        return jnp.einsum('blgh,gnh->blgn', ug, b_bar)

    def scan(bu, a_bar, reverse):
        a = jnp.broadcast_to(a_bar, bu.shape)
        _, xs = lax.associative_scan(_linear_op, (a, bu), reverse=reverse, axis=1)
        return xs

    def carry(lam_dt, s0, n, reverse):
        steps = (jnp.arange(n, 0, -1) if reverse else jnp.arange(1, n + 1)).astype(F32)
        decay = jnp.exp(lam_dt[None] * steps[:, None, None])
        return decay[None] * s0[:, None]

    def readout(c_mat, xs):
        y = jnp.real(jnp.einsum('ghn,blgn->blgh', c_mat, xs))
        return y.reshape(y.shape[0], y.shape[1], width)

    yx, yc = [], []
    for d in range(2):
        rev = d == 1
        lam = lax.complex(p['s5_a_re'][d], p['s5_a_im'][d])
        lam_dt = lam * jnp.exp(p['s5_log_step'][d])[:, None]
        a_bar = jnp.exp(lam_dt)
        b_mat = lax.complex(p['s5_b_re'][d], p['s5_b_im'][d])
        b_bar = ((a_bar - 1.0) / lam)[..., None] * b_mat
        c_mat = lax.complex(p['s5_c_re'][d], p['s5_c_im'][d])
        xs_c = scan(drive(uc, b_bar), a_bar, rev)
        s0 = xs_c[:, 0] if rev else xs_c[:, -1]
        xs_x = scan(drive(ux, b_bar), a_bar, rev) + carry(lam_dt, s0, ux.shape[1], rev)
        yx.append(readout(c_mat, xs_x))
        if ctx_out:
            yc.append(readout(c_mat, xs_c))

    def finish(ys, u):
        y = ys[0] + ys[1] + p['s5_d'] * u
        y = jax.nn.gelu(y)
        return y * jax.nn.sigmoid(y @ p['s5_glu_w'] + p['s5_glu_b'])

    return finish(yx, ux), (finish(yc, uc) if ctx_out else None)


def _hyena_filters(n, p):
    width = p['hy_bias'].shape[-1]
    order = p['hy_bias'].shape[0]
    pos = jnp.arange(n, dtype=F32)
    t = pos / max(n - 1, 1)
    freqs = jnp.linspace(1e-4, HY_POS_FREQS - 1, HY_POS_FREQS, dtype=F32)
    ang = (2.0 * math.pi / n) * pos[:, None] * freqs[None]
    feats = jnp.concatenate([t[:, None], jnp.cos(ang), -jnp.sin(ang)], axis=-1)
    h = jnp.sin(p['hy_f_freq1'] * (feats @ p['hy_f_w1'] + p['hy_f_b1']))
    h = jnp.sin(p['hy_f_freq2'] * (h @ p['hy_f_w2'] + p['hy_f_b2']))
    h = (h @ p['hy_f_w3'] + p['hy_f_b3']).reshape(n, order, 2, width)
    rates = jnp.abs(jnp.linspace(math.log(HY_DECAY_TARGET) / HY_LONG_DECAY_PCT,
                                 math.log(HY_DECAY_TARGET) / HY_SHORT_DECAY_PCT, width, dtype=F32))
    h = h * jnp.exp(-t[:, None] * rates)[:, None, None, :]
    fwd, bwd = h[:, :, 0], h[:, :, 1]
    filt = jnp.concatenate([fwd, jnp.zeros_like(fwd[:1]), jnp.flip(bwd[1:], axis=0)], axis=0)
    filt = filt / (jnp.sum(jnp.abs(filt), axis=0, keepdims=True) + EPS)
    return jnp.fft.rfft(filt, axis=0)


def _hyena_seq(z, p):
    n = z.shape[1]
    zc = _dwconv3(z, p['hy_conv_w'], p['hy_conv_b'])
    v, x1, x2 = jnp.split(zc, 3, axis=-1)
    filt_f = _hyena_filters(n, p)
    y = v
    for o, gate in enumerate((x1, x2)):
        yf = jnp.fft.rfft(y, n=2 * n, axis=1)
        conv = jnp.fft.irfft(yf * filt_f[None, :, o], n=2 * n, axis=1)[:, :n]
        y = gate * (conv + y * p['hy_bias'][o])
    return y


def _gla_chunked(q, k, v, g, s0):
    bsz, n = k.shape[:2]
    nc = n // GLA_CHUNK
    chunk = lambda t: t.reshape(bsz, nc, GLA_CHUNK, *t.shape[2:])
    k, v, g = chunk(k), chunk(v), chunk(g)
    b = jnp.cumsum(g, axis=2)
    b_last = b[:, :, -1]
    kv = jnp.einsum('bnjhd,bnjhe->bnhde', k * jnp.exp(b_last[:, :, None] - b), v)
    step = lambda s, inp: (jnp.exp(inp[0])[..., None] * s + inp[1], s)
    s_fin, s_prev = lax.scan(step, s0, (jnp.moveaxis(b_last, 1, 0), jnp.moveaxis(kv, 1, 0)))
    if q is None:
        return None, s_fin
    q_in = chunk(q) * jnp.exp(b)
    scores = jnp.einsum('bnihd,bnjhd->bnhij', q_in, k * jnp.exp(-b))
    lower = jnp.tril(jnp.ones((GLA_CHUNK, GLA_CHUNK), dtype=bool))
    scores = jnp.where(lower, scores, 0.0)
    o = (jnp.einsum('bnhij,bnjhe->bnihe', scores, v)
         + jnp.einsum('bnihd,bnhde->bnihe', q_in, jnp.moveaxis(s_prev, 0, 1)))
    return o.reshape(bsz, n, *o.shape[3:]), s_fin


def _gla_mixer(x_parts, c_parts, p, ctx_out):
    qx, kx, vx, lx, rx = x_parts
    qc, kc, vc, lc, rc = c_parts
    dv = p['gla_norm_g'].shape[0]
    nh = vx.shape[-1] // dv
    dk = kx.shape[-1] // nh
    rank = p['gla_wg'].shape[1]
    heads = lambda t, dh: t.reshape(t.shape[0], t.shape[1], nh, dh)
    scale = dk ** -0.5
    flip = lambda t: None if t is None else jnp.flip(t, axis=1)

    def gate(lr, d):
        pre = lr[..., d * rank:(d + 1) * rank] @ p['gla_wg'][d] + p['gla_bg'][d]
        return heads(jax.nn.log_sigmoid(pre) / GLA_GATE_TEMP, dk)

    qx = heads(_to_col_major(qx), dk) * scale
    kx = heads(_to_col_major(kx), dk)
    vx = heads(_to_col_major(vx), dv)
    lx = _to_col_major(lx)
    qc = heads(qc, dk) * scale if ctx_out else None
    kc = heads(kc, dk)
    vc = heads(vc, dv)
    zero = jnp.zeros((kc.shape[0], nh, dk, dv), F32)
    ox, oc = [], []
    for d in range(2):
        f = flip if d == 1 else (lambda t: t)
        o_c, s_c = _gla_chunked(f(qc), f(kc), f(vc), f(gate(lc, d)), zero)
        o_x, _ = _gla_chunked(f(qx), f(kx), f(vx), f(gate(lx, d)), s_c)
        ox.append(f(o_x))
        if ctx_out:
            oc.append(f(o_c))

    def finish(o, r):
        o = _rmsnorm(o, p['gla_norm_g']).reshape(o.shape[0], o.shape[1], nh * dv)
        return o * jax.nn.silu(r)

    yx = finish(_from_col_major(ox[0] + ox[1]), rx)
    yc = finish(oc[0] + oc[1], rc) if ctx_out else None
    return yx, yc


def kernel(x, c, ctx, c_ctx, w_mod, b_mod, norm1_g, norm2_g, w_in, s5_a_re, s5_a_im, s5_log_step, s5_b_re, s5_b_im, s5_c_re, s5_c_im, s5_d, s5_glu_w, s5_glu_b, hy_conv_w, hy_conv_b, hy_f_w1, hy_f_b1, hy_f_freq1, hy_f_w2, hy_f_b2, hy_f_freq2, hy_f_w3, hy_f_b3, hy_bias, gla_wg, gla_bg, gla_norm_g, w_branch, w_out, ff_w_up, ff_conv_w, ff_conv_b, ff_w_down, final_norm_g):
    bsz, seq, d = x.shape
    nctx = ctx.shape[1]
    depth = w_mod.shape[0]
    q4 = d // 4
    rank2 = 2 * gla_wg.shape[2]
    gpad = max(LANE, q4 // 2)
    assert bsz + 1 <= 8

    c_gk = q4
    c_gv = 2 * q4
    c_gg = 4 * q4
    c_gq = c_gg + rank2
    c_gr = c_gq + q4
    c_hy = c_gr + 2 * q4
    c_mg = c_hy + 3 * q4

    rows = jnp.zeros((8, d), F32).at[:bsz].set(c).at[bsz].set(c_ctx)
    mod = modulation(rows, w_mod, b_mod)

    xs = x.reshape(bsz * seq, d)
    cs = ctx.reshape(bsz * nctx, d)
    for l in range(depth):
        ctx_out = l < depth - 1
        sh1, s1, g1, sh2, s2, g2 = [mod[l, :, i * d:(i + 1) * d][:, None, :] for i in range(6)]
        gs1 = norm1_g[l] * (1.0 + s1)
        gs2 = norm2_g[l] * (1.0 + s2)
        bx = slice(0, bsz)
        bc = slice(bsz, bsz + 1)

        wl = w_in[l]
        w_pack = jnp.concatenate([
            wl[:, 0:c_gk], wl[:, c_gk:c_gv], wl[:, c_gq:c_gr], wl[:, c_hy:c_mg],
            wl[:, c_gv:c_gg], wl[:, c_gr:c_hy], wl[:, c_gg:c_gq],
            jnp.zeros((d, gpad - rank2), F32)], axis=1).astype(BF16)
        w_gate = wl[:, c_mg:].reshape(d, 3, d).swapaxes(0, 1).astype(BF16)
        wb = w_branch[l].astype(BF16)
        w_o = w_out[l].astype(BF16)
        w_up = ff_w_up[l].astype(BF16)
        w_dn = ff_w_down[l].astype(BF16)

        p = dict(s5_a_re=s5_a_re[l], s5_a_im=s5_a_im[l], s5_log_step=s5_log_step[l],
                 s5_b_re=s5_b_re[l], s5_b_im=s5_b_im[l], s5_c_re=s5_c_re[l], s5_c_im=s5_c_im[l],
                 s5_d=s5_d[l], s5_glu_w=s5_glu_w[l], s5_glu_b=s5_glu_b[l],
                 hy_conv_w=hy_conv_w[l], hy_conv_b=hy_conv_b[l], hy_f_w1=hy_f_w1[l], hy_f_b1=hy_f_b1[l],
                 hy_f_freq1=hy_f_freq1[l], hy_f_w2=hy_f_w2[l], hy_f_b2=hy_f_b2[l],
                 hy_f_freq2=hy_f_freq2[l], hy_f_w3=hy_f_w3[l], hy_f_b3=hy_f_b3[l], hy_bias=hy_bias[l],
                 gla_wg=gla_wg[l], gla_bg=gla_bg[l], gla_norm_g=gla_norm_g[l])

        px = norm_matmul(xs, gs1[bx], sh1[bx], w_pack).reshape(bsz, seq, -1)
        pc = norm_matmul(cs, gs1[bc], sh1[bc], w_pack).reshape(bsz, nctx, -1)

        def parts(t):
            return dict(s5=t[..., 0:q4], gk=t[..., q4:2 * q4], gq=t[..., 2 * q4:3 * q4],
                        hy=t[..., 3 * q4:6 * q4], gv=t[..., 6 * q4:8 * q4], gr=t[..., 8 * q4:10 * q4],
                        gg=t[..., 10 * q4:10 * q4 + rank2])

        tx, tc = parts(px), parts(pc)
        ya_x, ya_c = _s5_mixer(tx['s5'], tc['s5'], p, ctx_out)
        yc_x, yc_c = _gla_mixer((tx['gq'], tx['gk'], tx['gv'], tx['gg'], tx['gr']),
                                (tc['gq'], tc['gk'], tc['gv'], tc['gg'], tc['gr']), p, ctx_out)
        yb_x = _hyena_seq(tx['hy'], p)

        flat = lambda t: t.reshape(-1, t.shape[-1])
        mx = merge(xs, gs1[bx], sh1[bx], flat(ya_x), flat(yb_x), flat(yc_x), w_gate,
                   wb[:q4], wb[q4:2 * q4], wb[2 * q4:])
        xs = matmul_residual(mx, w_o, xs, g1[bx])
        ab = norm_matmul(xs, gs2[bx], sh2[bx], w_up)
        act = ffn_act(ab, ff_conv_w[l], ff_conv_b[l], bsz)
        xs = matmul_residual(act, w_dn, xs, g2[bx])

        if ctx_out:
            yb_c = _hyena_seq(tc['hy'], p)
            mc = merge(cs, gs1[bc], sh1[bc], flat(ya_c), flat(yb_c), flat(yc_c), w_gate,
                       wb[:q4], wb[q4:2 * q4], wb[2 * q4:])
            cs = matmul_residual(mc, w_o, cs, g1[bc])
            abc = norm_matmul(cs, gs2[bc], sh2[bc], w_up)
            actc = ffn_act(abc, ff_conv_w[l], ff_conv_b[l], bsz)
            cs = matmul_residual(actc, w_dn, cs, g2[bc])

    return final_rmsnorm(xs, final_norm_g).reshape(bsz, seq, d)
```

```python
import functools
import math

import jax
import jax.numpy as jnp
from jax import lax
from jax.experimental import pallas as pl
from jax.experimental.pallas import tpu as pltpu

F32 = jnp.float32
BF16 = jnp.bfloat16
EPS = 1e-6
GRID_W = 64
S5_GROUP = 16
GLA_CHUNK = 64
GLA_GATE_TEMP = 16.0
HY_POS_FREQS = 16
HY_DECAY_TARGET = 1e-2
HY_SHORT_DECAY_PCT = 0.3
HY_LONG_DECAY_PCT = 1.5
LANE = 128
VMEM_LIMIT = 56 * 1024 * 1024


def _pick_tile(n, cap, mult=LANE):
    best = None
    for t in range(mult, min(n, cap) + 1, mult):
        if n % t == 0:
            best = t
    assert best is not None, (n, cap, mult)
    return best


def _cparams(*sem):
    return pltpu.CompilerParams(dimension_semantics=sem, vmem_limit_bytes=VMEM_LIMIT)


def _mod_kernel(r_ref, w_ref, b_ref, o_ref):
    r = r_ref[...]
    s = r * jax.nn.sigmoid(r)
    o_ref[0] = jnp.dot(s, w_ref[0], preferred_element_type=F32,
                       precision=lax.Precision.HIGHEST) + b_ref[0]


def modulation(rows, w_mod, b_mod):
    depth, d, n = w_mod.shape
    tn = _pick_tile(n, 1024)
    return pl.pallas_call(
        _mod_kernel,
        out_shape=jax.ShapeDtypeStruct((depth, 8, n), F32),
        grid=(depth, n // tn),
        in_specs=[pl.BlockSpec((8, d), lambda l, j: (0, 0)),
                  pl.BlockSpec((1, d, tn), lambda l, j: (l, 0, j)),
                  pl.BlockSpec((1, 1, tn), lambda l, j: (l, 0, j))],
        out_specs=pl.BlockSpec((1, 8, tn), lambda l, j: (l, 0, j)),
        compiler_params=_cparams("parallel", "parallel"),
        name="modulation",
    )(rows, w_mod, b_mod.reshape(depth, 1, n))


def _norm_mm_kernel(x_ref, gs_ref, sh_ref, w_ref, o_ref, h_ref):
    @pl.when(pl.program_id(1) == 0)
    def _():
        x = x_ref[...]
        ms = jnp.mean(x * x, axis=-1, keepdims=True)
        h_ref[...] = (x * lax.rsqrt(ms + EPS) * gs_ref[0] + sh_ref[0]).astype(BF16)

    o_ref[...] = jnp.dot(h_ref[...], w_ref[...], preferred_element_type=F32).astype(o_ref.dtype)


def norm_matmul(x, gs, sh, w, out_dtype=F32):
    m, d = x.shape
    n = w.shape[1]
    nb = gs.shape[0]
    tm = _pick_tile(m // nb, 1024, 8)
    tn = _pick_tile(n, 1024)
    tpb = (m // nb) // tm
    return pl.pallas_call(
        _norm_mm_kernel,
        out_shape=jax.ShapeDtypeStruct((m, n), out_dtype),
        grid=(m // tm, n // tn),
        in_specs=[pl.BlockSpec((tm, d), lambda i, j: (i, 0)),
                  pl.BlockSpec((1, 1, d), lambda i, j: (i // tpb, 0, 0)),
                  pl.BlockSpec((1, 1, d), lambda i, j: (i // tpb, 0, 0)),
                  pl.BlockSpec((d, tn), lambda i, j: (0, j))],
        out_specs=pl.BlockSpec((tm, tn), lambda i, j: (i, j)),
        scratch_shapes=[pltpu.VMEM((tm, d), BF16)],
        compiler_params=_cparams("parallel", "arbitrary"),
        name="norm_matmul",
    )(x, gs, sh, w)


def _mm_res_kernel(a_ref, w_ref, r_ref, g_ref, o_ref):
    acc = jnp.dot(a_ref[...], w_ref[...], preferred_element_type=F32)
    o_ref[...] = r_ref[...] + g_ref[0] * acc


def matmul_residual(a, w, res, g):
    m, k = a.shape
    n = w.shape[1]
    nb = g.shape[0]
    tm = _pick_tile(m // nb, 1024, 8)
    tn = _pick_tile(n, 512)
    tpb = (m // nb) // tm
    return pl.pallas_call(
        _mm_res_kernel,
        out_shape=jax.ShapeDtypeStruct((m, n), F32),
        grid=(m // tm, n // tn),
        in_specs=[pl.BlockSpec((tm, k), lambda i, j: (i, 0)),
                  pl.BlockSpec((k, tn), lambda i, j: (0, j)),
                  pl.BlockSpec((tm, tn), lambda i, j: (i, j)),
                  pl.BlockSpec((1, 1, tn), lambda i, j: (i // tpb, 0, j))],
        out_specs=pl.BlockSpec((tm, tn), lambda i, j: (i, j)),
        compiler_params=_cparams("parallel", "arbitrary"),
        name="matmul_residual",
    )(a, w, res, g)


def _merge_kernel(x_ref, gs_ref, sh_ref, ya_ref, yb_ref, yc_ref, wg_ref, wba_ref, wbb_ref, wbc_ref,
                  o_ref, h_ref):
    @pl.when(pl.program_id(1) == 0)
    def _():
        x = x_ref[...]
        ms = jnp.mean(x * x, axis=-1, keepdims=True)
        h_ref[...] = (x * lax.rsqrt(ms + EPS) * gs_ref[0] + sh_ref[0]).astype(BF16)

    h = h_ref[...]
    m = None
    for i, (y_ref, wb_ref) in enumerate(((ya_ref, wba_ref), (yb_ref, wbb_ref), (yc_ref, wbc_ref))):
        gate = jax.nn.sigmoid(jnp.dot(h, wg_ref[i], preferred_element_type=F32))
        br = jnp.dot(y_ref[...].astype(BF16), wb_ref[...], preferred_element_type=F32)
        m = gate * br if m is None else m + gate * br
    o_ref[...] = m.astype(o_ref.dtype)


def merge(x, gs, sh, ya, yb, yc, wg, wba, wbb, wbc):
    m, d = x.shape
    nb = gs.shape[0]
    tm = _pick_tile(m // nb, 512, 8)
    tn = _pick_tile(d, 512)
    tpb = (m // nb) // tm
    row = lambda i, j: (i, 0)
    return pl.pallas_call(
        _merge_kernel,
        out_shape=jax.ShapeDtypeStruct((m, d), BF16),
        grid=(m // tm, d // tn),
        in_specs=[pl.BlockSpec((tm, d), row),
                  pl.BlockSpec((1, 1, d), lambda i, j: (i // tpb, 0, 0)),
                  pl.BlockSpec((1, 1, d), lambda i, j: (i // tpb, 0, 0)),
                  pl.BlockSpec((tm, ya.shape[1]), row),
                  pl.BlockSpec((tm, yb.shape[1]), row),
                  pl.BlockSpec((tm, yc.shape[1]), row),
                  pl.BlockSpec((3, d, tn), lambda i, j: (0, 0, j)),
                  pl.BlockSpec((wba.shape[0], tn), lambda i, j: (0, j)),
                  pl.BlockSpec((wbb.shape[0], tn), lambda i, j: (0, j)),
                  pl.BlockSpec((wbc.shape[0], tn), lambda i, j: (0, j))],
        out_specs=pl.BlockSpec((tm, tn), lambda i, j: (i, j)),
        scratch_shapes=[pltpu.VMEM((tm, d), BF16)],
        compiler_params=_cparams("parallel", "arbitrary"),
        name="merge",
    )(x, gs, sh, ya, yb, yc, wg, wba, wbb, wbc)


def _dwconv3_tile(a, prev8, next8, first, last, w_ref, cb_ref):
    tr = a.shape[0]
    row = lax.broadcasted_iota(jnp.int32, a.shape, 0)
    before = jnp.where(first, 0.0, prev8[7:8, :])
    after = jnp.where(last, 0.0, next8[0:1, :])
    prev = jnp.where(row == 0, before, pltpu.roll(a, 1, 0))
    nxt = jnp.where(row == tr - 1, after, pltpu.roll(a, tr - 1, 0))
    return prev * w_ref[0:1, :] + a * w_ref[1:2, :] + nxt * w_ref[2:3, :] + cb_ref[...]


def _ffn_act_kernel(a_ref, ap_ref, an_ref, b_ref, w_ref, cb_ref, o_ref):
    i = pl.program_id(1)
    conv = _dwconv3_tile(a_ref[0], ap_ref[0], an_ref[0], i == 0, i == pl.num_programs(1) - 1, w_ref, cb_ref)
    o_ref[0] = (conv * jax.nn.sigmoid(conv) * b_ref[0]).astype(o_ref.dtype)


def ffn_act(ab, conv_w, conv_b, bsz):
    m, f2 = ab.shape
    f = f2 // 2
    n = m // bsz
    tc = _pick_tile(f, 512)
    tr = _pick_tile(n, 1024, 8)
    nf = f // tc
    r8 = tr // 8
    last8 = n // 8 - 1
    ab3 = ab.reshape(bsz, n, f2)
    out = pl.pallas_call(
        _ffn_act_kernel,
        out_shape=jax.ShapeDtypeStruct((bsz, n, f), BF16),
        grid=(bsz, n // tr, nf),
        in_specs=[pl.BlockSpec((1, tr, tc), lambda b, i, j: (b, i, j)),
                  pl.BlockSpec((1, 8, tc), lambda b, i, j: (b, jnp.maximum(i * r8 - 1, 0), j)),
                  pl.BlockSpec((1, 8, tc), lambda b, i, j: (b, jnp.minimum((i + 1) * r8, last8), j)),
                  pl.BlockSpec((1, tr, tc), lambda b, i, j: (b, i, j + nf)),
                  pl.BlockSpec((3, tc), lambda b, i, j: (0, j)),
                  pl.BlockSpec((1, tc), lambda b, i, j: (0, j))],
        out_specs=pl.BlockSpec((1, tr, tc), lambda b, i, j: (b, i, j)),
        compiler_params=_cparams("parallel", "parallel", "parallel"),
        name="ffn_act",
    )(ab3, ab3, ab3, ab3, conv_w, conv_b.reshape(1, f))
    return out.reshape(m, f)


def _rmsnorm_kernel(x_ref, g_ref, o_ref):
    x = x_ref[...]
    ms = jnp.mean(x * x, axis=-1, keepdims=True)
    o_ref[...] = x * lax.rsqrt(ms + EPS) * g_ref[...]


def final_rmsnorm(x, g):
    m, d = x.shape
    tm = _pick_tile(m, 1024, 8)
    return pl.pallas_call(
        _rmsnorm_kernel,
        out_shape=jax.ShapeDtypeStruct((m, d), F32),
        grid=(m // tm,),
        in_specs=[pl.BlockSpec((tm, d), lambda i: (i, 0)), pl.BlockSpec((1, d), lambda i: (0, 0))],
        out_specs=pl.BlockSpec((tm, d), lambda i: (i, 0)),
        compiler_params=_cparams("parallel"),
        name="final_rmsnorm",
    )(x, g.reshape(1, d))


S5_CHUNK = 256


def _const_spec(shape):
    zeros = (0,) * len(shape)
    return pl.BlockSpec(shape, lambda b, k: zeros, pipeline_mode=pl.Buffered(1))


def _s5_kernel(*refs, ncc, reverse, finish):
    if finish:
        (uc_ref, ux_ref, bblk_ref, cblk_ref, enr_ref, eni_ref, epr_ref, epi_ref, ac_ref,
         pc_ref, px_ref, d_ref, gw_ref, gb_ref, yc_ref, yx_ref, h_ref) = refs
    else:
        (uc_ref, ux_ref, bblk_ref, cblk_ref, enr_ref, eni_ref, epr_ref, epi_ref, ac_ref,
         yc_ref, yx_ref, h_ref) = refs
    k = pl.program_id(1)
    t = uc_ref.shape[1]
    gn = enr_ref.shape[1]

    @pl.when(k == 0)
    def _():
        h_ref[...] = jnp.zeros_like(h_ref)

    is_ctx = k < ncc
    u = jnp.where(is_ctx, uc_ref[0], ux_ref[0])
    bu = jnp.dot(u.astype(BF16), bblk_ref[...], preferred_element_type=F32)
    br, bi = bu[:, :gn], bu[:, gn:]
    enr, eni = enr_ref[...], eni_ref[...]
    z = jnp.concatenate([br * enr - bi * eni, br * eni + bi * enr], axis=1).astype(BF16)
    row = lax.broadcasted_iota(jnp.int32, (t, t), 0)
    col = lax.broadcasted_iota(jnp.int32, (t, t), 1)
    tri = jnp.where((col >= row) if reverse else (col <= row), 1.0, 0.0).astype(BF16)
    cs = jnp.dot(tri, z, preferred_element_type=F32)
    hr, hi = h_ref[0:1, :], h_ref[1:2, :]
    acr, aci = ac_ref[0:1, :], ac_ref[1:2, :]
    sr = cs[:, :gn] + (hr * acr - hi * aci)
    si = cs[:, gn:] + (hr * aci + hi * acr)
    epr, epi = epr_ref[...], epi_ref[...]
    xr = sr * epr - si * epi
    xi = sr * epi + si * epr
    last = 0 if reverse else t - 1
    h_ref[0:1, :] = xr[last:last + 1, :]
    h_ref[1:2, :] = xi[last:last + 1, :]
    xs = jnp.concatenate([xr, xi], axis=1).astype(BF16)
    y = jnp.dot(xs, cblk_ref[...], preferred_element_type=F32)
    if finish:
        y = y + jnp.where(is_ctx, pc_ref[0], px_ref[0]) + d_ref[...] * u
        y = jax.nn.gelu(y)
        gate = jnp.dot(y.astype(BF16), gw_ref[...], preferred_element_type=F32) + gb_ref[...]
        y = y * jax.nn.sigmoid(gate)

    @pl.when(is_ctx)
    def _():
        yc_ref[0] = y.astype(yc_ref.dtype)

    @pl.when(jnp.logical_not(is_ctx))
    def _():
        yx_ref[0] = y.astype(yx_ref.dtype)


def _s5_tables(a_re, a_im, log_step, b_re, b_im, c_re, c_im, t):
    g, n, h = b_re.shape
    eye = jnp.eye(g, dtype=F32)
    dt = jnp.exp(log_step)[:, None]
    ldr, ldi = a_re * dt, a_im * dt
    mag = jnp.exp(ldr)
    abr, abi = mag * jnp.cos(ldi), mag * jnp.sin(ldi)
    den = a_re * a_re + a_im * a_im
    nr, ni = abr - 1.0, abi
    fr = (nr * a_re + ni * a_im) / den
    fi = (ni * a_re - nr * a_im) / den
    bbr = fr[..., None] * b_re - fi[..., None] * b_im
    bbi = fr[..., None] * b_im + fi[..., None] * b_re
    blk_b = lambda m: jnp.einsum('gnh,gk->ghkn', m, eye).reshape(g * h, g * n)
    bblk = jnp.concatenate([blk_b(bbr), blk_b(bbi)], axis=1).astype(BF16)
    blk_c = lambda m: jnp.einsum('ghn,gk->gnkh', m, eye).reshape(g * n, g * h)
    cblk = jnp.concatenate([blk_c(c_re), -blk_c(c_im)], axis=0).astype(BF16)
    centre = float(t // 2)
    steps = jnp.arange(1, t + 1, dtype=F32)[:, None] - centre
    lr, li = ldr.reshape(1, g * n), ldi.reshape(1, g * n)
    er, ei = steps * lr, steps * li
    epr, epi = jnp.exp(er) * jnp.cos(ei), jnp.exp(er) * jnp.sin(ei)
    enr, eni = jnp.exp(-er) * jnp.cos(ei), -jnp.exp(-er) * jnp.sin(ei)
    ac = jnp.concatenate([jnp.exp(centre * lr) * jnp.cos(centre * li),
                          jnp.exp(centre * lr) * jnp.sin(centre * li)], axis=0)
    return bblk, cblk, (enr, eni, epr, epi, ac)


def _s5_pass(pc, px, col, width, tabs, reverse, fin=None, out_dtype=F32):
    bsz, nctx = pc.shape[:2]
    seq = px.shape[1]
    t = S5_CHUNK
    ncc, ncx = nctx // t, seq // t
    bblk, cblk, (enr, eni, epr, epi, ac) = tabs
    gn = enr.shape[1]
    if reverse:
        enr, eni, epr, epi = [jnp.flip(e, axis=0) for e in (enr, eni, epr, epi)]
        cidx = lambda k: jnp.maximum(ncc - 1 - k, 0)
        xidx = lambda k: jnp.minimum(ncx - 1 - (k - ncc), ncx - 1)
    else:
        cidx = lambda k: jnp.minimum(k, ncc - 1)
        xidx = lambda k: jnp.maximum(k - ncc, 0)
    in_specs = [pl.BlockSpec((1, t, width), lambda b, k: (b, cidx(k), col)),
                pl.BlockSpec((1, t, width), lambda b, k: (b, xidx(k), col)),
                _const_spec(bblk.shape), _const_spec(cblk.shape),
                _const_spec(enr.shape), _const_spec(eni.shape), _const_spec(epr.shape), _const_spec(epi.shape),
                _const_spec(ac.shape)]
    args = [pc, px, bblk, cblk, enr, eni, epr, epi, ac]
    if fin is not None:
        prev_c, prev_x, dvec, gw, gb = fin
        in_specs += [pl.BlockSpec((1, t, width), lambda b, k: (b, cidx(k), 0)),
                     pl.BlockSpec((1, t, width), lambda b, k: (b, xidx(k), 0)),
                     _const_spec((1, width)), _const_spec(gw.shape), _const_spec((1, width))]
        args += [prev_c, prev_x, dvec.reshape(1, width), gw, gb.reshape(1, width)]
    return pl.pallas_call(
        functools.partial(_s5_kernel, ncc=ncc, reverse=reverse, finish=fin is not None),
        out_shape=(jax.ShapeDtypeStruct((bsz, nctx, width), out_dtype),
                   jax.ShapeDtypeStruct((bsz, seq, width), out_dtype)),
        grid=(bsz, ncc + ncx),
        in_specs=in_specs,
        out_specs=(pl.BlockSpec((1, t, width), lambda b, k: (b, cidx(k), 0)),
                   pl.BlockSpec((1, t, width), lambda b, k: (b, xidx(k), 0))),
        scratch_shapes=[pltpu.VMEM((2, gn), F32)],
        compiler_params=_cparams("parallel", "arbitrary"),
        name="s5_rev" if reverse else "s5_fwd",
    )(*args)


def s5_mixer(pc, px, width, p):
    tabs = [_s5_tables(p['s5_a_re'][d], p['s5_a_im'][d], p['s5_log_step'][d], p['s5_b_re'][d], p['s5_b_im'][d],
                       p['s5_c_re'][d], p['s5_c_im'][d], S5_CHUNK) for d in range(2)]
    bc, bx = _s5_pass(pc, px, 0, width, tabs[1], reverse=True)
    return _s5_pass(pc, px, 0, width, tabs[0], reverse=False,
                    fin=(bc, bx, p['s5_d'], p['s5_glu_w'].astype(BF16), p['s5_glu_b']))


def _rmsnorm(x, g):
    y = x * lax.rsqrt(jnp.mean(x * x, axis=-1, keepdims=True) + EPS)
    return y * g


def _dwconv3(x, w, b):
    xp = jnp.pad(x, ((0, 0), (1, 1), (0, 0)))
    return xp[:, :-2] * w[0] + xp[:, 1:-1] * w[1] + xp[:, 2:] * w[2] + b


def _to_col_major(t):
    bsz, n = t.shape[:2]
    rows = n // GRID_W
    return t.reshape(bsz, rows, GRID_W, *t.shape[2:]).swapaxes(1, 2).reshape(bsz, n, *t.shape[2:])


def _from_col_major(t):
    bsz, n = t.shape[:2]
    rows = n // GRID_W
    return t.reshape(bsz, GRID_W, rows, *t.shape[2:]).swapaxes(1, 2).reshape(bsz, n, *t.shape[2:])


def _linear_op(e1, e2):
    a1, b1 = e1
    a2, b2 = e2
    return a1 * a2, a2 * b1 + b2


def _s5_mixer(ux, uc, p, ctx_out):
    groups, nstate = p['s5_a_re'].shape[1:]
    width = ux.shape[-1]

    def drive(u, b_bar):
        ug = u.reshape(u.shape[0], u.shape[1], groups, S5_GROUP).astype(jnp.complex64)
        return jnp.einsum('blgh,gnh->blgn', ug, b_bar)

    def scan(bu, a_bar, reverse):
        a = jnp.broadcast_to(a_bar, bu.shape)
        _, xs = lax.associative_scan(_linear_op, (a, bu), reverse=reverse, axis=1)
        return xs

    def carry(lam_dt, s0, n, reverse):
        steps = (jnp.arange(n, 0, -1) if reverse else jnp.arange(1, n + 1)).astype(F32)
        decay = jnp.exp(lam_dt[None] * steps[:, None, None])
        return decay[None] * s0[:, None]

    def readout(c_mat, xs):
        y = jnp.real(jnp.einsum('ghn,blgn->blgh', c_mat, xs))
        return y.reshape(y.shape[0], y.shape[1], width)

    yx, yc = [], []
    for d in range(2):
        rev = d == 1
        lam = lax.complex(p['s5_a_re'][d], p['s5_a_im'][d])
        lam_dt = lam * jnp.exp(p['s5_log_step'][d])[:, None]
        a_bar = jnp.exp(lam_dt)
        b_mat = lax.complex(p['s5_b_re'][d], p['s5_b_im'][d])
        b_bar = ((a_bar - 1.0) / lam)[..., None] * b_mat
        c_mat = lax.complex(p['s5_c_re'][d], p['s5_c_im'][d])
        xs_c = scan(drive(uc, b_bar), a_bar, rev)
        s0 = xs_c[:, 0] if rev else xs_c[:, -1]
        xs_x = scan(drive(ux, b_bar), a_bar, rev) + carry(lam_dt, s0, ux.shape[1], rev)
        yx.append(readout(c_mat, xs_x))
        if ctx_out:
            yc.append(readout(c_mat, xs_c))

    def finish(ys, u):
        y = ys[0] + ys[1] + p['s5_d'] * u
        y = jax.nn.gelu(y)
        return y * jax.nn.sigmoid(y @ p['s5_glu_w'] + p['s5_glu_b'])

    return finish(yx, ux), (finish(yc, uc) if ctx_out else None)


def _hyena_filters(n, p):
    width = p['hy_bias'].shape[-1]
    order = p['hy_bias'].shape[0]
    pos = jnp.arange(n, dtype=F32)
    t = pos / max(n - 1, 1)
    freqs = jnp.linspace(1e-4, HY_POS_FREQS - 1, HY_POS_FREQS, dtype=F32)
    ang = (2.0 * math.pi / n) * pos[:, None] * freqs[None]
    feats = jnp.concatenate([t[:, None], jnp.cos(ang), -jnp.sin(ang)], axis=-1)
    h = jnp.sin(p['hy_f_freq1'] * (feats @ p['hy_f_w1'] + p['hy_f_b1']))
    h = jnp.sin(p['hy_f_freq2'] * (h @ p['hy_f_w2'] + p['hy_f_b2']))
    h = (h @ p['hy_f_w3'] + p['hy_f_b3']).reshape(n, order, 2, width)
    rates = jnp.abs(jnp.linspace(math.log(HY_DECAY_TARGET) / HY_LONG_DECAY_PCT,
                                 math.log(HY_DECAY_TARGET) / HY_SHORT_DECAY_PCT, width, dtype=F32))
    h = h * jnp.exp(-t[:, None] * rates)[:, None, None, :]
    fwd, bwd = h[:, :, 0], h[:, :, 1]
    filt = jnp.concatenate([fwd, jnp.zeros_like(fwd[:1]), jnp.flip(bwd[1:], axis=0)], axis=0)
    filt = filt / (jnp.sum(jnp.abs(filt), axis=0, keepdims=True) + EPS)
    return jnp.fft.rfft(filt, axis=0)


def _hyena_seq(z, p):
    n = z.shape[1]
    zc = _dwconv3(z, p['hy_conv_w'], p['hy_conv_b'])
    v, x1, x2 = jnp.split(zc, 3, axis=-1)
    filt_f = _hyena_filters(n, p)
    y = v
    for o, gate in enumerate((x1, x2)):
        yf = jnp.fft.rfft(y, n=2 * n, axis=1)
        conv = jnp.fft.irfft(yf * filt_f[None, :, o], n=2 * n, axis=1)[:, :n]
        y = gate * (conv + y * p['hy_bias'][o])
    return y


def _gla_chunked(q, k, v, g, s0):
    bsz, n = k.shape[:2]
    nc = n // GLA_CHUNK
    chunk = lambda t: t.reshape(bsz, nc, GLA_CHUNK, *t.shape[2:])
    k, v, g = chunk(k), chunk(v), chunk(g)
    b = jnp.cumsum(g, axis=2)
    b_last = b[:, :, -1]
    kv = jnp.einsum('bnjhd,bnjhe->bnhde', k * jnp.exp(b_last[:, :, None] - b), v)
    step = lambda s, inp: (jnp.exp(inp[0])[..., None] * s + inp[1], s)
    s_fin, s_prev = lax.scan(step, s0, (jnp.moveaxis(b_last, 1, 0), jnp.moveaxis(kv, 1, 0)))
    if q is None:
        return None, s_fin
    q_in = chunk(q) * jnp.exp(b)
    scores = jnp.einsum('bnihd,bnjhd->bnhij', q_in, k * jnp.exp(-b))
    lower = jnp.tril(jnp.ones((GLA_CHUNK, GLA_CHUNK), dtype=bool))
    scores = jnp.where(lower, scores, 0.0)
    o = (jnp.einsum('bnhij,bnjhe->bnihe', scores, v)
         + jnp.einsum('bnihd,bnhde->bnihe', q_in, jnp.moveaxis(s_prev, 0, 1)))
    return o.reshape(bsz, n, *o.shape[3:]), s_fin


def _gla_mixer(x_parts, c_parts, p, ctx_out):
    qx, kx, vx, lx, rx = x_parts
    qc, kc, vc, lc, rc = c_parts
    dv = p['gla_norm_g'].shape[0]
    nh = vx.shape[-1] // dv
    dk = kx.shape[-1] // nh
    rank = p['gla_wg'].shape[1]
    heads = lambda t, dh: t.reshape(t.shape[0], t.shape[1], nh, dh)
    scale = dk ** -0.5
    flip = lambda t: None if t is None else jnp.flip(t, axis=1)

    def gate(lr, d):
        pre = lr[..., d * rank:(d + 1) * rank] @ p['gla_wg'][d] + p['gla_bg'][d]
        return heads(jax.nn.log_sigmoid(pre) / GLA_GATE_TEMP, dk)

    qx = heads(_to_col_major(qx), dk) * scale
    kx = heads(_to_col_major(kx), dk)
    vx = heads(_to_col_major(vx), dv)
    lx = _to_col_major(lx)
    qc = heads(qc, dk) * scale if ctx_out else None
    kc = heads(kc, dk)
    vc = heads(vc, dv)
    zero = jnp.zeros((kc.shape[0], nh, dk, dv), F32)
    ox, oc = [], []
    for d in range(2):
        f = flip if d == 1 else (lambda t: t)
        o_c, s_c = _gla_chunked(f(qc), f(kc), f(vc), f(gate(lc, d)), zero)
        o_x, _ = _gla_chunked(f(qx), f(kx), f(vx), f(gate(lx, d)), s_c)
        ox.append(f(o_x))
        if ctx_out:
            oc.append(f(o_c))

    def finish(o, r):
        o = _rmsnorm(o, p['gla_norm_g']).reshape(o.shape[0], o.shape[1], nh * dv)
        return o * jax.nn.silu(r)

    yx = finish(_from_col_major(ox[0] + ox[1]), rx)
    yc = finish(oc[0] + oc[1], rc) if ctx_out else None
    return yx, yc


def kernel(x, c, ctx, c_ctx, w_mod, b_mod, norm1_g, norm2_g, w_in, s5_a_re, s5_a_im, s5_log_step, s5_b_re, s5_b_im, s5_c_re, s5_c_im, s5_d, s5_glu_w, s5_glu_b, hy_conv_w, hy_conv_b, hy_f_w1, hy_f_b1, hy_f_freq1, hy_f_w2, hy_f_b2, hy_f_freq2, hy_f_w3, hy_f_b3, hy_bias, gla_wg, gla_bg, gla_norm_g, w_branch, w_out, ff_w_up, ff_conv_w, ff_conv_b, ff_w_down, final_norm_g):
    bsz, seq, d = x.shape
    nctx = ctx.shape[1]
    depth = w_mod.shape[0]
    q4 = d // 4
    rank2 = 2 * gla_wg.shape[2]
    gpad = max(LANE, q4 // 2)
    assert bsz + 1 <= 8

    c_gk = q4
    c_gv = 2 * q4
    c_gg = 4 * q4
    c_gq = c_gg + rank2
    c_gr = c_gq + q4
    c_hy = c_gr + 2 * q4
    c_mg = c_hy + 3 * q4

    rows = jnp.zeros((8, d), F32).at[:bsz].set(c).at[bsz].set(c_ctx)
    mod = modulation(rows, w_mod, b_mod)

    xs = x.reshape(bsz * seq, d)
    cs = ctx.reshape(bsz * nctx, d)
    for l in range(depth):
        ctx_out = l < depth - 1
        sh1, s1, g1, sh2, s2, g2 = [mod[l, :, i * d:(i + 1) * d][:, None, :] for i in range(6)]
        gs1 = norm1_g[l] * (1.0 + s1)
        gs2 = norm2_g[l] * (1.0 + s2)
        bx = slice(0, bsz)
        bc = slice(bsz, bsz + 1)

        wl = w_in[l]
        w_pack = jnp.concatenate([
            wl[:, 0:c_gk], wl[:, c_gk:c_gv], wl[:, c_gq:c_gr], wl[:, c_hy:c_mg],
            wl[:, c_gv:c_gg], wl[:, c_gr:c_hy], wl[:, c_gg:c_gq],
            jnp.zeros((d, gpad - rank2), F32)], axis=1).astype(BF16)
        w_gate = wl[:, c_mg:].reshape(d, 3, d).swapaxes(0, 1).astype(BF16)
        wb = w_branch[l].astype(BF16)
        w_o = w_out[l].astype(BF16)
        w_up = ff_w_up[l].astype(BF16)
        w_dn = ff_w_down[l].astype(BF16)

        p = dict(s5_a_re=s5_a_re[l], s5_a_im=s5_a_im[l], s5_log_step=s5_log_step[l],
                 s5_b_re=s5_b_re[l], s5_b_im=s5_b_im[l], s5_c_re=s5_c_re[l], s5_c_im=s5_c_im[l],
                 s5_d=s5_d[l], s5_glu_w=s5_glu_w[l], s5_glu_b=s5_glu_b[l],
                 hy_conv_w=hy_conv_w[l], hy_conv_b=hy_conv_b[l], hy_f_w1=hy_f_w1[l], hy_f_b1=hy_f_b1[l],
                 hy_f_freq1=hy_f_freq1[l], hy_f_w2=hy_f_w2[l], hy_f_b2=hy_f_b2[l],
                 hy_f_freq2=hy_f_freq2[l], hy_f_w3=hy_f_w3[l], hy_f_b3=hy_f_b3[l], hy_bias=hy_bias[l],
                 gla_wg=gla_wg[l], gla_bg=gla_bg[l], gla_norm_g=gla_norm_g[l])

        px = norm_matmul(xs, gs1[bx], sh1[bx], w_pack).reshape(bsz, seq, -1)
        pc = norm_matmul(cs, gs1[bc], sh1[bc], w_pack).reshape(bsz, nctx, -1)

        def parts(t):
            return dict(s5=t[..., 0:q4], gk=t[..., q4:2 * q4], gq=t[..., 2 * q4:3 * q4],
                        hy=t[..., 3 * q4:6 * q4], gv=t[..., 6 * q4:8 * q4], gr=t[..., 8 * q4:10 * q4],
                        gg=t[..., 10 * q4:10 * q4 + rank2])

        tx, tc = parts(px), parts(pc)
        ya_c, ya_x = s5_mixer(pc, px, q4, p)
        yc_x, yc_c = _gla_mixer((tx['gq'], tx['gk'], tx['gv'], tx['gg'], tx['gr']),
                                (tc['gq'], tc['gk'], tc['gv'], tc['gg'], tc['gr']), p, ctx_out)
        yb_x = _hyena_seq(tx['hy'], p)

        flat = lambda t: t.reshape(-1, t.shape[-1])
        mx = merge(xs, gs1[bx], sh1[bx], flat(ya_x), flat(yb_x), flat(yc_x), w_gate,
                   wb[:q4], wb[q4:2 * q4], wb[2 * q4:])
        xs = matmul_residual(mx, w_o, xs, g1[bx])
        ab = norm_matmul(xs, gs2[bx], sh2[bx], w_up)
        act = ffn_act(ab, ff_conv_w[l], ff_conv_b[l], bsz)
        xs = matmul_residual(act, w_dn, xs, g2[bx])

        if ctx_out:
            yb_c = _hyena_seq(tc['hy'], p)
            mc = merge(cs, gs1[bc], sh1[bc], flat(ya_c), flat(yb_c), flat(yc_c), w_gate,
                       wb[:q4], wb[q4:2 * q4], wb[2 * q4:])
            cs = matmul_residual(mc, w_o, cs, g1[bc])
            abc = norm_matmul(cs, gs2[bc], sh2[bc], w_up)
            actc = ffn_act(abc, ff_conv_w[l], ff_conv_b[l], bsz)
            cs = matmul_residual(actc, w_dn, cs, g2[bc])

    return final_rmsnorm(xs, final_norm_g).reshape(bsz, seq, d)
```

```python
import functools
import math

import jax
import jax.numpy as jnp
from jax import lax
from jax.experimental import pallas as pl
from jax.experimental.pallas import tpu as pltpu

F32 = jnp.float32
BF16 = jnp.bfloat16
EPS = 1e-6
GRID_W = 64
S5_GROUP = 16
GLA_CHUNK = 64
GLA_GATE_TEMP = 16.0
HY_POS_FREQS = 16
HY_DECAY_TARGET = 1e-2
HY_SHORT_DECAY_PCT = 0.3
HY_LONG_DECAY_PCT = 1.5
LANE = 128
VMEM_LIMIT = 56 * 1024 * 1024


def _pick_tile(n, cap, mult=LANE):
    best = None
    for t in range(mult, min(n, cap) + 1, mult):
        if n % t == 0:
            best = t
    assert best is not None, (n, cap, mult)
    return best


def _cparams(*sem):
    return pltpu.CompilerParams(dimension_semantics=sem, vmem_limit_bytes=VMEM_LIMIT)


def _mod_kernel(r_ref, w_ref, b_ref, o_ref):
    r = r_ref[...]
    s = r * jax.nn.sigmoid(r)
    o_ref[0] = jnp.dot(s, w_ref[0], preferred_element_type=F32,
                       precision=lax.Precision.HIGHEST) + b_ref[0]


def modulation(rows, w_mod, b_mod):
    depth, d, n = w_mod.shape
    tn = _pick_tile(n, 1024)
    return pl.pallas_call(
        _mod_kernel,
        out_shape=jax.ShapeDtypeStruct((depth, 8, n), F32),
        grid=(depth, n // tn),
        in_specs=[pl.BlockSpec((8, d), lambda l, j: (0, 0)),
                  pl.BlockSpec((1, d, tn), lambda l, j: (l, 0, j)),
                  pl.BlockSpec((1, 1, tn), lambda l, j: (l, 0, j))],
        out_specs=pl.BlockSpec((1, 8, tn), lambda l, j: (l, 0, j)),
        compiler_params=_cparams("parallel", "parallel"),
        name="modulation",
    )(rows, w_mod, b_mod.reshape(depth, 1, n))


def _norm_mm_kernel(x_ref, gs_ref, sh_ref, w_ref, o_ref, h_ref):
    @pl.when(pl.program_id(1) == 0)
    def _():
        x = x_ref[...]
        ms = jnp.mean(x * x, axis=-1, keepdims=True)
        h_ref[...] = (x * lax.rsqrt(ms + EPS) * gs_ref[0] + sh_ref[0]).astype(BF16)

    o_ref[...] = jnp.dot(h_ref[...], w_ref[...], preferred_element_type=F32).astype(o_ref.dtype)


def norm_matmul(x, gs, sh, w, out_dtype=F32):
    m, d = x.shape
    n = w.shape[1]
    nb = gs.shape[0]
    tm = _pick_tile(m // nb, 1024, 8)
    tn = _pick_tile(n, 1024)
    tpb = (m // nb) // tm
    return pl.pallas_call(
        _norm_mm_kernel,
        out_shape=jax.ShapeDtypeStruct((m, n), out_dtype),
        grid=(m // tm, n // tn),
        in_specs=[pl.BlockSpec((tm, d), lambda i, j: (i, 0)),
                  pl.BlockSpec((1, 1, d), lambda i, j: (i // tpb, 0, 0)),
                  pl.BlockSpec((1, 1, d), lambda i, j: (i // tpb, 0, 0)),
                  pl.BlockSpec((d, tn), lambda i, j: (0, j))],
        out_specs=pl.BlockSpec((tm, tn), lambda i, j: (i, j)),
        scratch_shapes=[pltpu.VMEM((tm, d), BF16)],
        compiler_params=_cparams("parallel", "arbitrary"),
        name="norm_matmul",
    )(x, gs, sh, w)


def _mm_res_kernel(a_ref, w_ref, r_ref, g_ref, o_ref):
    acc = jnp.dot(a_ref[...], w_ref[...], preferred_element_type=F32)
    o_ref[...] = r_ref[...] + g_ref[0] * acc


def matmul_residual(a, w, res, g):
    m, k = a.shape
    n = w.shape[1]
    nb = g.shape[0]
    tm = _pick_tile(m // nb, 1024, 8)
    tn = _pick_tile(n, 512)
    tpb = (m // nb) // tm
    return pl.pallas_call(
        _mm_res_kernel,
        out_shape=jax.ShapeDtypeStruct((m, n), F32),
        grid=(m // tm, n // tn),
        in_specs=[pl.BlockSpec((tm, k), lambda i, j: (i, 0)),
                  pl.BlockSpec((k, tn), lambda i, j: (0, j)),
                  pl.BlockSpec((tm, tn), lambda i, j: (i, j)),
                  pl.BlockSpec((1, 1, tn), lambda i, j: (i // tpb, 0, j))],
        out_specs=pl.BlockSpec((tm, tn), lambda i, j: (i, j)),
        compiler_params=_cparams("parallel", "arbitrary"),
        name="matmul_residual",
    )(a, w, res, g)


def _merge_kernel(x_ref, gs_ref, sh_ref, ya_ref, yb_ref, oc_ref, r_ref, wg_ref, wb_ref, o_ref, h_ref, y_ref):
    wa, wb = ya_ref.shape[1], yb_ref.shape[1]

    @pl.when(pl.program_id(1) == 0)
    def _():
        x = x_ref[...]
        ms = jnp.mean(x * x, axis=-1, keepdims=True)
        h_ref[...] = (x * lax.rsqrt(ms + EPS) * gs_ref[0] + sh_ref[0]).astype(BF16)
        r = r_ref[...]
        y_ref[:, 0:wa] = ya_ref[...].astype(BF16)
        y_ref[:, wa:wa + wb] = yb_ref[...].astype(BF16)
        y_ref[:, wa + wb:] = (oc_ref[...] * (r * jax.nn.sigmoid(r))).astype(BF16)

    h = h_ref[...]
    m = None
    for i, (lo, hi) in enumerate(((0, wa), (wa, wa + wb), (wa + wb, y_ref.shape[1]))):
        gate = jax.nn.sigmoid(jnp.dot(h, wg_ref[i], preferred_element_type=F32))
        br = jnp.dot(y_ref[:, lo:hi], wb_ref[lo:hi, :], preferred_element_type=F32)
        m = gate * br if m is None else m + gate * br
    o_ref[...] = m.astype(o_ref.dtype)


def merge(x, gs, sh, ya, yb, oc, pr, rcol, wg, wb):
    m, d = x.shape
    nb = gs.shape[0]
    tm = _pick_tile(m // nb, 512, 8)
    tn = _pick_tile(d, 512)
    tpb = (m // nb) // tm
    row = lambda i, j: (i, 0)
    wy = wb.shape[0]
    return pl.pallas_call(
        _merge_kernel,
        out_shape=jax.ShapeDtypeStruct((m, d), BF16),
        grid=(m // tm, d // tn),
        in_specs=[pl.BlockSpec((tm, d), row),
                  pl.BlockSpec((1, 1, d), lambda i, j: (i // tpb, 0, 0)),
                  pl.BlockSpec((1, 1, d), lambda i, j: (i // tpb, 0, 0)),
                  pl.BlockSpec((tm, ya.shape[1]), row),
                  pl.BlockSpec((tm, yb.shape[1]), row),
                  pl.BlockSpec((tm, oc.shape[1]), row),
                  pl.BlockSpec((tm, oc.shape[1]), lambda i, j: (i, rcol)),
                  pl.BlockSpec((3, d, tn), lambda i, j: (0, 0, j)),
                  pl.BlockSpec((wy, tn), lambda i, j: (0, j))],
        out_specs=pl.BlockSpec((tm, tn), lambda i, j: (i, j)),
        scratch_shapes=[pltpu.VMEM((tm, d), BF16), pltpu.VMEM((tm, wy), BF16)],
        compiler_params=_cparams("parallel", "arbitrary"),
        name="merge",
    )(x, gs, sh, ya, yb, oc, pr, wg, wb)


def _dwconv3_tile(a, prev8, next8, first, last, w_ref, cb_ref):
    tr = a.shape[0]
    row = lax.broadcasted_iota(jnp.int32, a.shape, 0)
    before = jnp.where(first, 0.0, prev8[7:8, :])
    after = jnp.where(last, 0.0, next8[0:1, :])
    prev = jnp.where(row == 0, before, pltpu.roll(a, 1, 0))
    nxt = jnp.where(row == tr - 1, after, pltpu.roll(a, tr - 1, 0))
    return prev * w_ref[0:1, :] + a * w_ref[1:2, :] + nxt * w_ref[2:3, :] + cb_ref[...]


def _ffn_act_kernel(a_ref, ap_ref, an_ref, b_ref, w_ref, cb_ref, o_ref):
    i = pl.program_id(1)
    conv = _dwconv3_tile(a_ref[0], ap_ref[0], an_ref[0], i == 0, i == pl.num_programs(1) - 1, w_ref, cb_ref)
    o_ref[0] = (conv * jax.nn.sigmoid(conv) * b_ref[0]).astype(o_ref.dtype)


def ffn_act(ab, conv_w, conv_b, bsz):
    m, f2 = ab.shape
    f = f2 // 2
    n = m // bsz
    tc = _pick_tile(f, 512)
    tr = _pick_tile(n, 1024, 8)
    nf = f // tc
    r8 = tr // 8
    last8 = n // 8 - 1
    ab3 = ab.reshape(bsz, n, f2)
    out = pl.pallas_call(
        _ffn_act_kernel,
        out_shape=jax.ShapeDtypeStruct((bsz, n, f), BF16),
        grid=(bsz, n // tr, nf),
        in_specs=[pl.BlockSpec((1, tr, tc), lambda b, i, j: (b, i, j)),
                  pl.BlockSpec((1, 8, tc), lambda b, i, j: (b, jnp.maximum(i * r8 - 1, 0), j)),
                  pl.BlockSpec((1, 8, tc), lambda b, i, j: (b, jnp.minimum((i + 1) * r8, last8), j)),
                  pl.BlockSpec((1, tr, tc), lambda b, i, j: (b, i, j + nf)),
                  pl.BlockSpec((3, tc), lambda b, i, j: (0, j)),
                  pl.BlockSpec((1, tc), lambda b, i, j: (0, j))],
        out_specs=pl.BlockSpec((1, tr, tc), lambda b, i, j: (b, i, j)),
        compiler_params=_cparams("parallel", "parallel", "parallel"),
        name="ffn_act",
    )(ab3, ab3, ab3, ab3, conv_w, conv_b.reshape(1, f))
    return out.reshape(m, f)


def _rmsnorm_kernel(x_ref, g_ref, o_ref):
    x = x_ref[...]
    ms = jnp.mean(x * x, axis=-1, keepdims=True)
    o_ref[...] = x * lax.rsqrt(ms + EPS) * g_ref[...]


def final_rmsnorm(x, g):
    m, d = x.shape
    tm = _pick_tile(m, 1024, 8)
    return pl.pallas_call(
        _rmsnorm_kernel,
        out_shape=jax.ShapeDtypeStruct((m, d), F32),
        grid=(m // tm,),
        in_specs=[pl.BlockSpec((tm, d), lambda i: (i, 0)), pl.BlockSpec((1, d), lambda i: (0, 0))],
        out_specs=pl.BlockSpec((tm, d), lambda i: (i, 0)),
        compiler_params=_cparams("parallel"),
        name="final_rmsnorm",
    )(x, g.reshape(1, d))


S5_CHUNK = 256


def _const_spec(shape):
    zeros = (0,) * len(shape)
    return pl.BlockSpec(shape, lambda b, k: zeros, pipeline_mode=pl.Buffered(1))


def _s5_kernel(*refs, ncc, reverse, finish):
    if finish:
        (uc_ref, ux_ref, bblk_ref, cblk_ref, enr_ref, eni_ref, epr_ref, epi_ref, ac_ref,
         pc_ref, px_ref, d_ref, gw_ref, gb_ref, yc_ref, yx_ref, h_ref) = refs
    else:
        (uc_ref, ux_ref, bblk_ref, cblk_ref, enr_ref, eni_ref, epr_ref, epi_ref, ac_ref,
         yc_ref, yx_ref, h_ref) = refs
    k = pl.program_id(1)
    t = uc_ref.shape[1]
    gn = enr_ref.shape[1]

    @pl.when(k == 0)
    def _():
        h_ref[...] = jnp.zeros_like(h_ref)

    is_ctx = k < ncc
    u = jnp.where(is_ctx, uc_ref[0], ux_ref[0])
    bu = jnp.dot(u.astype(BF16), bblk_ref[...], preferred_element_type=F32)
    br, bi = bu[:, :gn], bu[:, gn:]
    enr, eni = enr_ref[...], eni_ref[...]
    z = jnp.concatenate([br * enr - bi * eni, br * eni + bi * enr], axis=1).astype(BF16)
    row = lax.broadcasted_iota(jnp.int32, (t, t), 0)
    col = lax.broadcasted_iota(jnp.int32, (t, t), 1)
    tri = jnp.where((col >= row) if reverse else (col <= row), 1.0, 0.0).astype(BF16)
    cs = jnp.dot(tri, z, preferred_element_type=F32)
    hr, hi = h_ref[0:1, :], h_ref[1:2, :]
    acr, aci = ac_ref[0:1, :], ac_ref[1:2, :]
    sr = cs[:, :gn] + (hr * acr - hi * aci)
    si = cs[:, gn:] + (hr * aci + hi * acr)
    epr, epi = epr_ref[...], epi_ref[...]
    xr = sr * epr - si * epi
    xi = sr * epi + si * epr
    last = 0 if reverse else t - 1
    h_ref[0:1, :] = xr[last:last + 1, :]
    h_ref[1:2, :] = xi[last:last + 1, :]
    xs = jnp.concatenate([xr, xi], axis=1).astype(BF16)
    y = jnp.dot(xs, cblk_ref[...], preferred_element_type=F32)
    if finish:
        y = y + jnp.where(is_ctx, pc_ref[0], px_ref[0]) + d_ref[...] * u
        y = jax.nn.gelu(y)
        gate = jnp.dot(y.astype(BF16), gw_ref[...], preferred_element_type=F32) + gb_ref[...]
        y = y * jax.nn.sigmoid(gate)

    @pl.when(is_ctx)
    def _():
        yc_ref[0] = y.astype(yc_ref.dtype)

    @pl.when(jnp.logical_not(is_ctx))
    def _():
        yx_ref[0] = y.astype(yx_ref.dtype)


def _s5_tables(a_re, a_im, log_step, b_re, b_im, c_re, c_im, t):
    g, n, h = b_re.shape
    eye = jnp.eye(g, dtype=F32)
    dt = jnp.exp(log_step)[:, None]
    ldr, ldi = a_re * dt, a_im * dt
    mag = jnp.exp(ldr)
    abr, abi = mag * jnp.cos(ldi), mag * jnp.sin(ldi)
    den = a_re * a_re + a_im * a_im
    nr, ni = abr - 1.0, abi
    fr = (nr * a_re + ni * a_im) / den
    fi = (ni * a_re - nr * a_im) / den
    bbr = fr[..., None] * b_re - fi[..., None] * b_im
    bbi = fr[..., None] * b_im + fi[..., None] * b_re
    blk_b = lambda m: jnp.einsum('gnh,gk->ghkn', m, eye).reshape(g * h, g * n)
    bblk = jnp.concatenate([blk_b(bbr), blk_b(bbi)], axis=1).astype(BF16)
    blk_c = lambda m: jnp.einsum('ghn,gk->gnkh', m, eye).reshape(g * n, g * h)
    cblk = jnp.concatenate([blk_c(c_re), -blk_c(c_im)], axis=0).astype(BF16)
    centre = float(t // 2)
    steps = jnp.arange(1, t + 1, dtype=F32)[:, None] - centre
    lr, li = ldr.reshape(1, g * n), ldi.reshape(1, g * n)
    er, ei = steps * lr, steps * li
    epr, epi = jnp.exp(er) * jnp.cos(ei), jnp.exp(er) * jnp.sin(ei)
    enr, eni = jnp.exp(-er) * jnp.cos(ei), -jnp.exp(-er) * jnp.sin(ei)
    ac = jnp.concatenate([jnp.exp(centre * lr) * jnp.cos(centre * li),
                          jnp.exp(centre * lr) * jnp.sin(centre * li)], axis=0)
    return bblk, cblk, (enr, eni, epr, epi, ac)


def _s5_pass(pc, px, col, width, tabs, reverse, fin=None, out_dtype=F32):
    bsz, nctx = pc.shape[:2]
    seq = px.shape[1]
    t = S5_CHUNK
    ncc, ncx = nctx // t, seq // t
    bblk, cblk, (enr, eni, epr, epi, ac) = tabs
    gn = enr.shape[1]
    if reverse:
        enr, eni, epr, epi = [jnp.flip(e, axis=0) for e in (enr, eni, epr, epi)]
        cidx = lambda k: jnp.maximum(ncc - 1 - k, 0)
        xidx = lambda k: jnp.minimum(ncx - 1 - (k - ncc), ncx - 1)
    else:
        cidx = lambda k: jnp.minimum(k, ncc - 1)
        xidx = lambda k: jnp.maximum(k - ncc, 0)
    in_specs = [pl.BlockSpec((1, t, width), lambda b, k: (b, cidx(k), col)),
                pl.BlockSpec((1, t, width), lambda b, k: (b, xidx(k), col)),
                _const_spec(bblk.shape), _const_spec(cblk.shape),
                _const_spec(enr.shape), _const_spec(eni.shape), _const_spec(epr.shape), _const_spec(epi.shape),
                _const_spec(ac.shape)]
    args = [pc, px, bblk, cblk, enr, eni, epr, epi, ac]
    if fin is not None:
        prev_c, prev_x, dvec, gw, gb = fin
        in_specs += [pl.BlockSpec((1, t, width), lambda b, k: (b, cidx(k), 0)),
                     pl.BlockSpec((1, t, width), lambda b, k: (b, xidx(k), 0)),
                     _const_spec((1, width)), _const_spec(gw.shape), _const_spec((1, width))]
        args += [prev_c, prev_x, dvec.reshape(1, width), gw, gb.reshape(1, width)]
    return pl.pallas_call(
        functools.partial(_s5_kernel, ncc=ncc, reverse=reverse, finish=fin is not None),
        out_shape=(jax.ShapeDtypeStruct((bsz, nctx, width), out_dtype),
                   jax.ShapeDtypeStruct((bsz, seq, width), out_dtype)),
        grid=(bsz, ncc + ncx),
        in_specs=in_specs,
        out_specs=(pl.BlockSpec((1, t, width), lambda b, k: (b, cidx(k), 0)),
                   pl.BlockSpec((1, t, width), lambda b, k: (b, xidx(k), 0))),
        scratch_shapes=[pltpu.VMEM((2, gn), F32)],
        compiler_params=_cparams("parallel", "arbitrary"),
        name="s5_rev" if reverse else "s5_fwd",
    )(*args)


def s5_mixer(pc, px, width, p):
    tabs = [_s5_tables(p['s5_a_re'][d], p['s5_a_im'][d], p['s5_log_step'][d], p['s5_b_re'][d], p['s5_b_im'][d],
                       p['s5_c_re'][d], p['s5_c_im'][d], S5_CHUNK) for d in range(2)]
    bc, bx = _s5_pass(pc, px, 0, width, tabs[1], reverse=True)
    return _s5_pass(pc, px, 0, width, tabs[0], reverse=False,
                    fin=(bc, bx, p['s5_d'], p['s5_glu_w'].astype(BF16), p['s5_glu_b']))


GLA_STEP = 256


def _gla_kernel(*refs, nh, dk, dv, ncs, reverse, finish, scale):
    if finish:
        k_ref, q_ref, v_ref, l_ref, wg_ref, bg_ref, prev_ref, ng_ref, o_ref, s_ref = refs
    else:
        k_ref, q_ref, v_ref, l_ref, wg_ref, bg_ref, o_ref, s_ref = refs
    ts = k_ref.shape[1]
    nchunk = ts // GLA_CHUNK

    @pl.when(pl.program_id(1) == 0)
    def _():
        s_ref[...] = jnp.zeros_like(s_ref)

    pre = jnp.dot(l_ref[0], wg_ref[...], preferred_element_type=F32,
                  precision=lax.Precision.HIGHEST) + bg_ref[...]
    g = jax.nn.log_sigmoid(pre) * (1.0 / GLA_GATE_TEMP)
    row = lax.broadcasted_iota(jnp.int32, (ts, ts), 0)
    col = lax.broadcasted_iota(jnp.int32, (ts, ts), 1)
    same = (row // GLA_CHUNK) == (col // GLA_CHUNK)
    order = (col >= row) if reverse else (col <= row)
    cum = jnp.where(same & order, 1.0, 0.0)
    b = jnp.dot(cum, g, preferred_element_type=F32, precision=lax.Precision.HIGHEST)
    eb = jnp.exp(b)
    enb = jnp.exp(-b)
    q_in = q_ref[0] * scale * eb
    k_out = k_ref[0] * enb
    v = v_ref[0]
    ci = lax.broadcasted_iota(jnp.int32, (GLA_CHUNK, GLA_CHUNK), 0)
    cj = lax.broadcasted_iota(jnp.int32, (GLA_CHUNK, GLA_CHUNK), 1)
    keep = (cj >= ci) if reverse else (cj <= ci)
    nt = (((1,), (1,)), ((), ()))
    tn = (((0,), (0,)), ((), ()))
    for c in (range(nchunk - 1, -1, -1) if reverse else range(nchunk)):
        r0 = c * GLA_CHUNK
        rows = slice(r0, r0 + GLA_CHUNK)
        end = r0 if reverse else r0 + GLA_CHUNK - 1
        etot = jnp.exp(b[end:end + 1, :])
        k_kv = k_out[rows] * etot
        for h in range(nh):
            kc = slice(h * dk, (h + 1) * dk)
            vc = slice(h * dv, (h + 1) * dv)
            qh = q_in[rows, kc].astype(BF16)
            kh = k_out[rows, kc].astype(BF16)
            kkv = k_kv[:, kc].astype(BF16)
            vh = v[rows, vc].astype(BF16)
            st = s_ref[h]
            sc = lax.dot_general(qh, kh, nt, preferred_element_type=F32)
            sc = jnp.where(keep, sc, 0.0).astype(BF16)
            o = (jnp.dot(sc, vh, preferred_element_type=F32)
                 + lax.dot_general(qh, st.astype(BF16), nt, preferred_element_type=F32))
            kvt = lax.dot_general(vh, kkv, tn, preferred_element_type=F32)
            s_ref[h] = st * etot[:, kc] + kvt
            if finish:
                o = o + prev_ref[0, rows, vc]
                ms = jnp.mean(o * o, axis=-1, keepdims=True)
                o = o * lax.rsqrt(ms + EPS) * ng_ref[...]
            o_ref[0, rows, vc] = o


def _gla_pass(gall, wgp, bg, ncs, nh, dk, dv, gpad, reverse, fin=None):
    bsz, ntot, _ = gall.shape
    ts = GLA_STEP
    nsteps = ntot // ts
    key, val = nh * dk, nh * dv
    if reverse:
        idx = lambda k: jnp.where(k < ncs, ncs - 1 - k, nsteps - 1 - (k - ncs))
    else:
        idx = lambda k: k
    in_specs = [pl.BlockSpec((1, ts, key), lambda b, k: (b, idx(k), 0)),
                pl.BlockSpec((1, ts, key), lambda b, k: (b, idx(k), 1)),
                pl.BlockSpec((1, ts, val), lambda b, k: (b, idx(k), 1)),
                pl.BlockSpec((1, ts, gpad), lambda b, k: (b, idx(k), (2 * key + val) // gpad)),
                _const_spec(wgp.shape), _const_spec((1, key))]
    args = [gall, gall, gall, gall, wgp, bg.reshape(1, key)]
    if fin is not None:
        prev, ng = fin
        in_specs += [pl.BlockSpec((1, ts, val), lambda b, k: (b, idx(k), 0)), _const_spec((1, dv))]
        args += [prev, ng.reshape(1, dv)]
    return pl.pallas_call(
        functools.partial(_gla_kernel, nh=nh, dk=dk, dv=dv, ncs=ncs, reverse=reverse, finish=fin is not None,
                          scale=dk ** -0.5),
        out_shape=jax.ShapeDtypeStruct((bsz, ntot, val), F32),
        grid=(bsz, nsteps),
        in_specs=in_specs,
        out_specs=pl.BlockSpec((1, ts, val), lambda b, k: (b, idx(k), 0)),
        scratch_shapes=[pltpu.VMEM((nh, dv, dk), F32)],
        compiler_params=_cparams("parallel", "arbitrary"),
        name="gla_rev" if reverse else "gla_fwd",
    )(*args)


def gla_mixer(pc, px, c0, q4, gpad, p):
    nctx = pc.shape[1]
    dv = p['gla_norm_g'].shape[0]
    nh = (2 * q4) // dv
    dk = q4 // nh
    rank = p['gla_wg'].shape[1]
    gall = jnp.concatenate([pc[..., c0:], _to_col_major(px[..., c0:])], axis=1)
    wgp = [jnp.zeros((gpad, q4), F32).at[d * rank:(d + 1) * rank].set(p['gla_wg'][d]) for d in range(2)]
    kw = dict(ncs=nctx // GLA_STEP, nh=nh, dk=dk, dv=dv, gpad=gpad)
    o_rev = _gla_pass(gall, wgp[1], p['gla_bg'][1], reverse=True, **kw)
    o = _gla_pass(gall, wgp[0], p['gla_bg'][0], reverse=False, fin=(o_rev, p['gla_norm_g']), **kw)
    return o[:, :nctx], _from_col_major(o[:, nctx:])


HY_N2 = 128
HY_PAD = 8


def _hy_dims(n):
    n1 = 2 * n // HY_N2
    k1 = n1 // 2 + 1
    k1p = -(-k1 // 8) * 8
    return n1, k1, k1p


def _hy_tables(n):
    n1, k1, k1p = _hy_dims(n)
    big = 2 * n
    kk = jnp.arange(k1p, dtype=jnp.int32)[:, None]
    mm = jnp.arange(n1, dtype=jnp.int32)[None, :]
    ang = (2.0 * math.pi / n1) * ((kk * mm) % n1).astype(F32)
    valid = (kk < k1).astype(F32)
    f1 = jnp.concatenate([jnp.cos(ang) * valid, -jnp.sin(ang) * valid], axis=0)
    wk = jnp.where((kk == 0) | (kk == n1 // 2), 1.0, 2.0) * valid / big
    f1inv = jnp.concatenate([(jnp.cos(ang) * wk).T, (-jnp.sin(ang) * wk).T], axis=1)
    k1i = jnp.arange(k1, dtype=jnp.int32)[:, None, None]
    k2i = jnp.arange(HY_N2, dtype=jnp.int32)[None, :, None]
    n2i = jnp.arange(HY_N2, dtype=jnp.int32)[None, None, :]
    ph = (2.0 * math.pi / big) * ((n2i * (k1i + n1 * k2i)) % big).astype(F32)
    gr, gi = jnp.cos(ph), -jnp.sin(ph)
    gs = jnp.concatenate([gr, gi], axis=1)
    gts = jnp.concatenate([gr.swapaxes(1, 2), gi.swapaxes(1, 2)], axis=1)
    return f1.astype(BF16), f1inv.astype(BF16), gs.astype(BF16), gts.astype(BF16)


def _hy_stage1(src_ref, f1, a_r, a_i, nslab, k1p):
    pitch = HY_N2 + HY_PAD

    def body(n2, carry):
        rows = src_ref[pl.ds(n2, nslab, stride=pitch), :]
        out = jnp.dot(f1, rows.astype(BF16), preferred_element_type=F32)
        base = pl.multiple_of(n2 * k1p, 8)
        a_r[pl.ds(base, k1p), :] = out[:k1p]
        a_i[pl.ds(base, k1p), :] = out[k1p:]
        return carry

    lax.fori_loop(0, HY_N2, body, 0)


def _hy_stage2(a_r, a_i, gs_ref, k, k1p):
    ar = a_r[pl.ds(k, HY_N2, stride=k1p), :]
    ai = a_i[pl.ds(k, HY_N2, stride=k1p), :]
    rhs = jnp.concatenate([ar, ai], axis=1).astype(BF16)
    out = jnp.dot(gs_ref[k], rhs, preferred_element_type=F32)
    h = HY_N2
    return out[:h, :LANE] - out[h:, LANE:], out[:h, LANE:] + out[h:, :LANE]


def _hyena_conv_kernel(y_ref, g_ref, wy_ref, by_ref, wg_ref, bg_ref, bias_ref, hr_ref, hi_ref,
                       f1_ref, f1inv_ref, gs_ref, gts_ref, o_ref, ypad, zpad, a_r, a_i, *, conv_y):
    n = y_ref.shape[1]
    nslab = n // HY_N2
    pitch = HY_N2 + HY_PAD
    k1 = gs_ref.shape[0]
    k1p = f1_ref.shape[0] // 2

    def slab_conv(ref, i, w_ref, b_ref):
        r0 = pl.multiple_of(i * HY_N2, HY_N2)
        prev8 = ref[0, pl.ds(pl.multiple_of(jnp.maximum(r0 - 8, 0), 8), 8), :]
        next8 = ref[0, pl.ds(pl.multiple_of(jnp.minimum(r0 + HY_N2, n - 8), 8), 8), :]
        return _dwconv3_tile(ref[0, pl.ds(r0, HY_N2), :], prev8, next8, i == 0, i == nslab - 1, w_ref, b_ref)

    def fill(i, carry):
        if conv_y:
            y = slab_conv(y_ref, i, wy_ref, by_ref)
        else:
            y = y_ref[0, pl.ds(pl.multiple_of(i * HY_N2, HY_N2), HY_N2), :]
        ypad[pl.ds(pl.multiple_of(i * pitch, 8), HY_N2), :] = y
        return carry

    lax.fori_loop(0, nslab, fill, 0)
    _hy_stage1(ypad, f1_ref[...], a_r, a_i, nslab, k1p)

    def freq(k, carry):
        xr, xi = _hy_stage2(a_r, a_i, gs_ref, k, k1p)
        hr, hi = hr_ref[0, k], hi_ref[0, k]
        rhs = jnp.concatenate([xr * hr - xi * hi, xr * hi + xi * hr], axis=1).astype(BF16)
        out = jnp.dot(gts_ref[k], rhs, preferred_element_type=F32)
        h = HY_N2
        a_r[pl.ds(k, HY_N2, stride=k1p), :] = out[:h, :LANE] + out[h:, LANE:]
        a_i[pl.ds(k, HY_N2, stride=k1p), :] = out[:h, LANE:] - out[h:, :LANE]
        return carry

    lax.fori_loop(0, k1, freq, 0)
    f1inv = f1inv_ref[...][:nslab]

    def inv1(n2, carry):
        base = pl.multiple_of(n2 * k1p, 8)
        rhs = jnp.concatenate([a_r[pl.ds(base, k1p), :], a_i[pl.ds(base, k1p), :]], axis=0).astype(BF16)
        zpad[pl.ds(n2, nslab, stride=pitch), :] = jnp.dot(f1inv, rhs, preferred_element_type=F32)
        return carry

    lax.fori_loop(0, HY_N2, inv1, 0)

    def finish(i, carry):
        p0 = pl.multiple_of(i * pitch, 8)
        gate = slab_conv(g_ref, i, wg_ref, bg_ref)
        y = ypad[pl.ds(p0, HY_N2), :]
        o_ref[0, pl.ds(pl.multiple_of(i * HY_N2, HY_N2), HY_N2), :] = gate * (zpad[pl.ds(p0, HY_N2), :]
                                                                                + y * bias_ref[0])
        return carry

    lax.fori_loop(0, nslab, finish, 0)


def _hyena_spectrum_kernel(f_ref, f1_ref, gs_ref, hr_ref, hi_ref, a_r, a_i, fpad):
    big = f_ref.shape[1]
    nslab = big // HY_N2
    pitch = HY_N2 + HY_PAD
    k1 = gs_ref.shape[0]
    k1p = f1_ref.shape[0] // 2

    def fill(i, carry):
        fpad[pl.ds(pl.multiple_of(i * pitch, 8), HY_N2), :] = f_ref[0, pl.ds(pl.multiple_of(i * HY_N2, HY_N2), HY_N2), :]
        return carry

    lax.fori_loop(0, nslab, fill, 0)
    _hy_stage1(fpad, f1_ref[...], a_r, a_i, nslab, k1p)

    def freq(k, carry):
        xr, xi = _hy_stage2(a_r, a_i, gs_ref, k, k1p)
        hr_ref[0, k] = xr
        hi_ref[0, k] = xi
        return carry

    lax.fori_loop(0, k1, freq, 0)


def _one(shape, index_map):
    return pl.BlockSpec(shape, index_map, pipeline_mode=pl.Buffered(1))


def hyena_spectrum(filt, tabs):
    f1, _, gs, _ = tabs
    r, big, w = filt.shape
    ns = w // LANE
    n1, k1, k1p = _hy_dims(big // 2)
    shp = jax.ShapeDtypeStruct((r * ns, k1, HY_N2, LANE), F32)
    spec_o = pl.BlockSpec((1, k1, HY_N2, LANE), lambda i, j: (i * ns + j, 0, 0, 0))
    return pl.pallas_call(
        _hyena_spectrum_kernel,
        out_shape=(shp, shp),
        grid=(r, ns),
        in_specs=[_one((1, big, LANE), lambda i, j: (i, 0, j)),
                  _one(f1.shape, lambda i, j: (0, 0)),
                  _one(gs.shape, lambda i, j: (0, 0, 0))],
        out_specs=(spec_o, spec_o),
        scratch_shapes=[pltpu.VMEM((HY_N2 * k1p, LANE), F32), pltpu.VMEM((HY_N2 * k1p, LANE), F32),
                        pltpu.VMEM((n1 * (HY_N2 + HY_PAD), LANE), F32)],
        compiler_params=_cparams("parallel", "parallel"),
        name="hyena_spectrum",
    )(filt, f1, gs)


def hyena_order(y, ycol, g, gcol, conv_w, conv_b, cy, cg, bias, hr, hi, order, tabs, conv_y):
    f1, f1inv, gs, gts = tabs
    bsz, n = y.shape[:2]
    w = bias.shape[-1]
    ns = w // LANE
    n1, k1, k1p = _hy_dims(n)
    f1d = f1[:, :n1 // 2]
    pitch = HY_N2 + HY_PAD
    nslab = n // HY_N2
    cw = lambda c: _one((3, LANE), lambda j, b: (0, c + j))
    cb = lambda c: _one((1, LANE), lambda j, b: (0, c + j))
    hspec = _one((1, k1, HY_N2, LANE), lambda j, b: (order * ns + j, 0, 0, 0))
    return pl.pallas_call(
        functools.partial(_hyena_conv_kernel, conv_y=conv_y),
        out_shape=jax.ShapeDtypeStruct((bsz, n, w), F32),
        grid=(ns, bsz),
        in_specs=[_one((1, n, LANE), lambda j, b: (b, 0, ycol + j)),
                  _one((1, n, LANE), lambda j, b: (b, 0, gcol + j)),
                  cw(cy), cb(cy), cw(cg), cb(cg),
                  _one((1, 1, LANE), lambda j, b: (order, 0, j)),
                  hspec, hspec,
                  _one(f1d.shape, lambda j, b: (0, 0)), _one(f1inv.shape, lambda j, b: (0, 0)),
                  _one(gs.shape, lambda j, b: (0, 0, 0)), _one(gts.shape, lambda j, b: (0, 0, 0))],
        out_specs=pl.BlockSpec((1, n, LANE), lambda j, b: (b, 0, j)),
        scratch_shapes=[pltpu.VMEM((nslab * pitch, LANE), F32), pltpu.VMEM((nslab * pitch, LANE), F32),
                        pltpu.VMEM((HY_N2 * k1p, LANE), F32), pltpu.VMEM((HY_N2 * k1p, LANE), F32)],
        compiler_params=pltpu.CompilerParams(dimension_semantics=("parallel", "parallel"),
                                             vmem_limit_bytes=HY_VMEM_LIMIT),
        name="hyena_order",
    )(y, g, conv_w, conv_b, conv_w, conv_b, bias.reshape(bias.shape[0], 1, w), hr, hi, f1d, f1inv, gs, gts)


HY_VMEM_LIMIT = 60 * 1024 * 1024


def hyena_latent(px, c0, filt, p):
    w = p['hy_bias'].shape[-1]
    ns = w // LANE
    n = px.shape[1]
    tabs = _hy_tables(n)
    hr, hi = hyena_spectrum(filt, tabs)
    cw, cb = p['hy_conv_w'], p['hy_conv_b'].reshape(1, -1)
    b0 = c0 // LANE
    y1 = hyena_order(px, b0, px, b0 + ns, cw, cb, 0, ns, p['hy_bias'], hr, hi, 0, tabs, True)
    return hyena_order(y1, 0, px, b0 + 2 * ns, cw, cb, 0, 2 * ns, p['hy_bias'], hr, hi, 1, tabs, False)


def _hyena_ctx_kernel(zv_ref, z1_ref, z2_ref, wv_ref, bv_ref, w1_ref, b1_ref, w2_ref, b2_ref, bias_ref,
                      filt_ref, ff_ref, finv_ref, o_ref):
    n = zv_ref.shape[1]
    hp = lax.Precision.HIGHEST
    zero8 = jnp.zeros((8, LANE), F32)
    conv = lambda ref, w, b: _dwconv3_tile(ref[0], zero8, zero8, True, True, w, b)
    ff = ff_ref[...]
    kp = ff.shape[0] // 2
    y = conv(zv_ref, wv_ref, bv_ref)
    for o, (g_ref, w, b) in enumerate(((z1_ref, w1_ref, b1_ref), (z2_ref, w2_ref, b2_ref))):
        hsp = jnp.dot(ff, filt_ref[o], preferred_element_type=F32, precision=hp)
        ysp = jnp.dot(ff[:, :n], y, preferred_element_type=F32, precision=hp)
        hr, hi, yr, yi = hsp[:kp], hsp[kp:], ysp[:kp], ysp[kp:]
        z = jnp.concatenate([yr * hr - yi * hi, yr * hi + yi * hr], axis=0)
        cv = jnp.dot(finv_ref[...], z, preferred_element_type=F32, precision=hp)
        y = conv(g_ref, w, b) * (cv + y * bias_ref[o])
    o_ref[0] = y


def hyena_context(pc, c0, filt, p):
    bsz, n = pc.shape[:2]
    w = p['hy_bias'].shape[-1]
    ns = w // LANE
    kp = -(-(n + 1) // 8) * 8
    kk = jnp.arange(kp, dtype=jnp.int32)[:, None]
    mm = jnp.arange(2 * n, dtype=jnp.int32)[None, :]
    ang = (math.pi / n) * ((kk * mm) % (2 * n)).astype(F32)
    valid = (kk <= n).astype(F32)
    ff = jnp.concatenate([jnp.cos(ang) * valid, -jnp.sin(ang) * valid], axis=0)
    wk = jnp.where((kk == 0) | (kk == n), 1.0, 2.0) * valid / (2 * n)
    finv = jnp.concatenate([(jnp.cos(ang) * wk).T[:n], (-jnp.sin(ang) * wk).T[:n]], axis=1)
    b0 = c0 // LANE
    zs = lambda c: pl.BlockSpec((1, n, LANE), lambda j, b: (b, 0, b0 + c + j))
    cw = lambda c: pl.BlockSpec((3, LANE), lambda j, b: (0, c + j))
    cb = lambda c: pl.BlockSpec((1, LANE), lambda j, b: (0, c + j))
    conv_w, conv_b = p['hy_conv_w'], p['hy_conv_b'].reshape(1, -1)
    return pl.pallas_call(
        _hyena_ctx_kernel,
        out_shape=jax.ShapeDtypeStruct((bsz, n, w), F32),
        grid=(ns, bsz),
        in_specs=[zs(0), zs(ns), zs(2 * ns), cw(0), cb(0), cw(ns), cb(ns), cw(2 * ns), cb(2 * ns),
                  pl.BlockSpec((2, 1, LANE), lambda j, b: (0, 0, j)),
                  pl.BlockSpec((2, 2 * n, LANE), lambda j, b: (0, 0, j)),
                  pl.BlockSpec(ff.shape, lambda j, b: (0, 0)), pl.BlockSpec(finv.shape, lambda j, b: (0, 0))],
        out_specs=pl.BlockSpec((1, n, LANE), lambda j, b: (b, 0, j)),
        compiler_params=_cparams("parallel", "parallel"),
        name="hyena_context",
    )(pc, pc, pc, conv_w, conv_b, conv_w, conv_b, conv_w, conv_b, p['hy_bias'].reshape(2, 1, w), filt, ff, finv)


def _to_col_major(t):
    bsz, n = t.shape[:2]
    rows = n // GRID_W
    return t.reshape(bsz, rows, GRID_W, *t.shape[2:]).swapaxes(1, 2).reshape(bsz, n, *t.shape[2:])


def _from_col_major(t):
    bsz, n = t.shape[:2]
    rows = n // GRID_W
    return t.reshape(bsz, GRID_W, rows, *t.shape[2:]).swapaxes(1, 2).reshape(bsz, n, *t.shape[2:])


def _hyena_filt_time(n, p):
    width = p['hy_bias'].shape[-1]
    order = p['hy_bias'].shape[0]
    pos = jnp.arange(n, dtype=F32)
    t = pos / max(n - 1, 1)
    freqs = jnp.linspace(1e-4, HY_POS_FREQS - 1, HY_POS_FREQS, dtype=F32)
    ang = (2.0 * math.pi / n) * pos[:, None] * freqs[None]
    feats = jnp.concatenate([t[:, None], jnp.cos(ang), -jnp.sin(ang)], axis=-1)
    h = jnp.sin(p['hy_f_freq1'] * (feats @ p['hy_f_w1'] + p['hy_f_b1']))
    h = jnp.sin(p['hy_f_freq2'] * (h @ p['hy_f_w2'] + p['hy_f_b2']))
    h = (h @ p['hy_f_w3'] + p['hy_f_b3']).reshape(n, order, 2, width)
    rates = jnp.abs(jnp.linspace(math.log(HY_DECAY_TARGET) / HY_LONG_DECAY_PCT,
                                 math.log(HY_DECAY_TARGET) / HY_SHORT_DECAY_PCT, width, dtype=F32))
    h = h * jnp.exp(-t[:, None] * rates)[:, None, None, :]
    fwd, bwd = h[:, :, 0], h[:, :, 1]
    filt = jnp.concatenate([fwd, jnp.zeros_like(fwd[:1]), jnp.flip(bwd[1:], axis=0)], axis=0)
    filt = filt / (jnp.sum(jnp.abs(filt), axis=0, keepdims=True) + EPS)
    return filt


def kernel(x, c, ctx, c_ctx, w_mod, b_mod, norm1_g, norm2_g, w_in, s5_a_re, s5_a_im, s5_log_step, s5_b_re, s5_b_im, s5_c_re, s5_c_im, s5_d, s5_glu_w, s5_glu_b, hy_conv_w, hy_conv_b, hy_f_w1, hy_f_b1, hy_f_freq1, hy_f_w2, hy_f_b2, hy_f_freq2, hy_f_w3, hy_f_b3, hy_bias, gla_wg, gla_bg, gla_norm_g, w_branch, w_out, ff_w_up, ff_conv_w, ff_conv_b, ff_w_down, final_norm_g):
    bsz, seq, d = x.shape
    nctx = ctx.shape[1]
    depth = w_mod.shape[0]
    q4 = d // 4
    rank2 = 2 * gla_wg.shape[2]
    gpad = max(LANE, q4 // 2)
    assert bsz + 1 <= 8

    c_gk = q4
    c_gv = 2 * q4
    c_gg = 4 * q4
    c_gq = c_gg + rank2
    c_gr = c_gq + q4
    c_hy = c_gr + 2 * q4
    c_mg = c_hy + 3 * q4

    rows = jnp.zeros((8, d), F32).at[:bsz].set(c).at[bsz].set(c_ctx)
    mod = modulation(rows, w_mod, b_mod)

    xs = x.reshape(bsz * seq, d)
    cs = ctx.reshape(bsz * nctx, d)
    for l in range(depth):
        ctx_out = l < depth - 1
        sh1, s1, g1, sh2, s2, g2 = [mod[l, :, i * d:(i + 1) * d][:, None, :] for i in range(6)]
        gs1 = norm1_g[l] * (1.0 + s1)
        gs2 = norm2_g[l] * (1.0 + s2)
        bx = slice(0, bsz)
        bc = slice(bsz, bsz + 1)

        wl = w_in[l]
        w_pack = jnp.concatenate([
            wl[:, 0:c_gk], wl[:, c_hy:c_mg], wl[:, c_gr:c_hy], wl[:, c_gk:c_gv], wl[:, c_gq:c_gr],
            wl[:, c_gv:c_gg], wl[:, c_gg:c_gq],
            jnp.zeros((d, gpad - rank2), F32)], axis=1).astype(BF16)
        w_gate = wl[:, c_mg:].reshape(d, 3, d).swapaxes(0, 1).astype(BF16)
        wb = w_branch[l].astype(BF16)
        w_o = w_out[l].astype(BF16)
        w_up = ff_w_up[l].astype(BF16)
        w_dn = ff_w_down[l].astype(BF16)

        p = dict(s5_a_re=s5_a_re[l], s5_a_im=s5_a_im[l], s5_log_step=s5_log_step[l],
                 s5_b_re=s5_b_re[l], s5_b_im=s5_b_im[l], s5_c_re=s5_c_re[l], s5_c_im=s5_c_im[l],
                 s5_d=s5_d[l], s5_glu_w=s5_glu_w[l], s5_glu_b=s5_glu_b[l],
                 hy_conv_w=hy_conv_w[l], hy_conv_b=hy_conv_b[l], hy_f_w1=hy_f_w1[l], hy_f_b1=hy_f_b1[l],
                 hy_f_freq1=hy_f_freq1[l], hy_f_w2=hy_f_w2[l], hy_f_b2=hy_f_b2[l],
                 hy_f_freq2=hy_f_freq2[l], hy_f_w3=hy_f_w3[l], hy_f_b3=hy_f_b3[l], hy_bias=hy_bias[l],
                 gla_wg=gla_wg[l], gla_bg=gla_bg[l], gla_norm_g=gla_norm_g[l])

        px = norm_matmul(xs, gs1[bx], sh1[bx], w_pack).reshape(bsz, seq, -1)
        pc = norm_matmul(cs, gs1[bc], sh1[bc], w_pack).reshape(bsz, nctx, -1)

        ya_c, ya_x = s5_mixer(pc, px, q4, p)
        oc_c, oc_x = gla_mixer(pc, px, 6 * q4, q4, gpad, p)
        yb_x = hyena_latent(px, q4, _hyena_filt_time(seq, p).swapaxes(0, 1), p)

        flat = lambda t: t.reshape(-1, t.shape[-1])
        mx = merge(xs, gs1[bx], sh1[bx], flat(ya_x), flat(yb_x), flat(oc_x), flat(px), 2, w_gate, wb)
        xs = matmul_residual(mx, w_o, xs, g1[bx])
        ab = norm_matmul(xs, gs2[bx], sh2[bx], w_up)
        act = ffn_act(ab, ff_conv_w[l], ff_conv_b[l], bsz)
        xs = matmul_residual(act, w_dn, xs, g2[bx])

        if ctx_out:
            yb_c = hyena_context(pc, q4, _hyena_filt_time(nctx, p).swapaxes(0, 1), p)
            mc = merge(cs, gs1[bc], sh1[bc], flat(ya_c), flat(yb_c), flat(oc_c), flat(pc), 2, w_gate, wb)
            cs = matmul_residual(mc, w_o, cs, g1[bc])
            abc = norm_matmul(cs, gs2[bc], sh2[bc], w_up)
            actc = ffn_act(abc, ff_conv_w[l], ff_conv_b[l], bsz)
            cs = matmul_residual(actc, w_dn, cs, g2[bc])

    return final_rmsnorm(xs, final_norm_g).reshape(bsz, seq, d)
```

```python
import functools
import math

import jax
import jax.numpy as jnp
from jax import lax
from jax.experimental import pallas as pl
from jax.experimental.pallas import tpu as pltpu

F32 = jnp.float32
BF16 = jnp.bfloat16
EPS = 1e-6
GRID_W = 64
S5_GROUP = 16
GLA_CHUNK = 64
GLA_GATE_TEMP = 16.0
HY_POS_FREQS = 16
HY_DECAY_TARGET = 1e-2
HY_SHORT_DECAY_PCT = 0.3
HY_LONG_DECAY_PCT = 1.5
LANE = 128
VMEM_LIMIT = 56 * 1024 * 1024


def _pick_tile(n, cap, mult=LANE):
    best = None
    for t in range(mult, min(n, cap) + 1, mult):
        if n % t == 0:
            best = t
    assert best is not None, (n, cap, mult)
    return best


def _cparams(*sem):
    return pltpu.CompilerParams(dimension_semantics=sem, vmem_limit_bytes=VMEM_LIMIT)


def _mod_kernel(r_ref, w_ref, b_ref, o_ref):
    r = r_ref[...]
    s = r * jax.nn.sigmoid(r)
    o_ref[0] = jnp.dot(s, w_ref[0], preferred_element_type=F32,
                       precision=lax.Precision.HIGHEST) + b_ref[0]


def modulation(rows, w_mod, b_mod):
    depth, d, n = w_mod.shape
    tn = _pick_tile(n, 1024)
    return pl.pallas_call(
        _mod_kernel,
        out_shape=jax.ShapeDtypeStruct((depth, 8, n), F32),
        grid=(depth, n // tn),
        in_specs=[pl.BlockSpec((8, d), lambda l, j: (0, 0)),
                  pl.BlockSpec((1, d, tn), lambda l, j: (l, 0, j)),
                  pl.BlockSpec((1, 1, tn), lambda l, j: (l, 0, j))],
        out_specs=pl.BlockSpec((1, 8, tn), lambda l, j: (l, 0, j)),
        compiler_params=_cparams("parallel", "parallel"),
        name="modulation",
    )(rows, w_mod, b_mod.reshape(depth, 1, n))


def _norm_mm_kernel(x_ref, gs_ref, sh_ref, w_ref, o_ref, h_ref):
    @pl.when(pl.program_id(1) == 0)
    def _():
        x = x_ref[...]
        ms = jnp.mean(x * x, axis=-1, keepdims=True)
        h_ref[...] = (x * lax.rsqrt(ms + EPS) * gs_ref[0] + sh_ref[0]).astype(BF16)

    o_ref[...] = jnp.dot(h_ref[...], w_ref[...], preferred_element_type=F32).astype(o_ref.dtype)


def norm_matmul(x, gs, sh, w, out_dtype=F32):
    m, d = x.shape
    n = w.shape[1]
    nb = gs.shape[0]
    tm = _pick_tile(m // nb, 1024, 8)
    tn = _pick_tile(n, 1024)
    tpb = (m // nb) // tm
    return pl.pallas_call(
        _norm_mm_kernel,
        out_shape=jax.ShapeDtypeStruct((m, n), out_dtype),
        grid=(m // tm, n // tn),
        in_specs=[pl.BlockSpec((tm, d), lambda i, j: (i, 0)),
                  pl.BlockSpec((1, 1, d), lambda i, j: (i // tpb, 0, 0)),
                  pl.BlockSpec((1, 1, d), lambda i, j: (i // tpb, 0, 0)),
                  pl.BlockSpec((d, tn), lambda i, j: (0, j))],
        out_specs=pl.BlockSpec((tm, tn), lambda i, j: (i, j)),
        scratch_shapes=[pltpu.VMEM((tm, d), BF16)],
        compiler_params=_cparams("parallel", "arbitrary"),
        name="norm_matmul",
    )(x, gs, sh, w)


def _mm_res_kernel(a_ref, w_ref, r_ref, g_ref, o_ref):
    acc = jnp.dot(a_ref[...], w_ref[...], preferred_element_type=F32)
    o_ref[...] = r_ref[...] + g_ref[0] * acc


def matmul_residual(a, w, res, g):
    m, k = a.shape
    n = w.shape[1]
    nb = g.shape[0]
    tm = _pick_tile(m // nb, 1024, 8)
    tn = _pick_tile(n, 512)
    tpb = (m // nb) // tm
    return pl.pallas_call(
        _mm_res_kernel,
        out_shape=jax.ShapeDtypeStruct((m, n), F32),
        grid=(m // tm, n // tn),
        in_specs=[pl.BlockSpec((tm, k), lambda i, j: (i, 0)),
                  pl.BlockSpec((k, tn), lambda i, j: (0, j)),
                  pl.BlockSpec((tm, tn), lambda i, j: (i, j)),
                  pl.BlockSpec((1, 1, tn), lambda i, j: (i // tpb, 0, j))],
        out_specs=pl.BlockSpec((tm, tn), lambda i, j: (i, j)),
        compiler_params=_cparams("parallel", "arbitrary"),
        name="matmul_residual",
    )(a, w, res, g)


def _merge_kernel(x_ref, gs_ref, sh_ref, ya_ref, yb_ref, oc_ref, r_ref, wg_ref, wb_ref, o_ref, h_ref, y_ref):
    wa, wb = ya_ref.shape[1], yb_ref.shape[1]

    @pl.when(pl.program_id(1) == 0)
    def _():
        x = x_ref[...]
        ms = jnp.mean(x * x, axis=-1, keepdims=True)
        h_ref[...] = (x * lax.rsqrt(ms + EPS) * gs_ref[0] + sh_ref[0]).astype(BF16)
        r = r_ref[...]
        y_ref[:, 0:wa] = ya_ref[...].astype(BF16)
        y_ref[:, wa:wa + wb] = yb_ref[...].astype(BF16)
        y_ref[:, wa + wb:] = (oc_ref[...] * (r * jax.nn.sigmoid(r))).astype(BF16)

    h = h_ref[...]
    m = None
    for i, (lo, hi) in enumerate(((0, wa), (wa, wa + wb), (wa + wb, y_ref.shape[1]))):
        gate = jax.nn.sigmoid(jnp.dot(h, wg_ref[i], preferred_element_type=F32))
        br = jnp.dot(y_ref[:, lo:hi], wb_ref[lo:hi, :], preferred_element_type=F32)
        m = gate * br if m is None else m + gate * br
    o_ref[...] = m.astype(o_ref.dtype)


def merge(x, gs, sh, ya, yb, oc, pr, rcol, wg, wb):
    m, d = x.shape
    nb = gs.shape[0]
    tm = _pick_tile(m // nb, 512, 8)
    tn = _pick_tile(d, 512)
    tpb = (m // nb) // tm
    row = lambda i, j: (i, 0)
    wy = wb.shape[0]
    return pl.pallas_call(
        _merge_kernel,
        out_shape=jax.ShapeDtypeStruct((m, d), BF16),
        grid=(m // tm, d // tn),
        in_specs=[pl.BlockSpec((tm, d), row),
                  pl.BlockSpec((1, 1, d), lambda i, j: (i // tpb, 0, 0)),
                  pl.BlockSpec((1, 1, d), lambda i, j: (i // tpb, 0, 0)),
                  pl.BlockSpec((tm, ya.shape[1]), row),
                  pl.BlockSpec((tm, yb.shape[1]), row),
                  pl.BlockSpec((tm, oc.shape[1]), row),
                  pl.BlockSpec((tm, oc.shape[1]), lambda i, j: (i, rcol)),
                  pl.BlockSpec((3, d, tn), lambda i, j: (0, 0, j)),
                  pl.BlockSpec((wy, tn), lambda i, j: (0, j))],
        out_specs=pl.BlockSpec((tm, tn), lambda i, j: (i, j)),
        scratch_shapes=[pltpu.VMEM((tm, d), BF16), pltpu.VMEM((tm, wy), BF16)],
        compiler_params=_cparams("parallel", "arbitrary"),
        name="merge",
    )(x, gs, sh, ya, yb, oc, pr, wg, wb)


def _dwconv3_tile(a, prev8, next8, first, last, w_ref, cb_ref):
    tr = a.shape[0]
    row = lax.broadcasted_iota(jnp.int32, a.shape, 0)
    before = jnp.where(first, 0.0, prev8[7:8, :])
    after = jnp.where(last, 0.0, next8[0:1, :])
    prev = jnp.where(row == 0, before, pltpu.roll(a, 1, 0))
    nxt = jnp.where(row == tr - 1, after, pltpu.roll(a, tr - 1, 0))
    return prev * w_ref[0:1, :] + a * w_ref[1:2, :] + nxt * w_ref[2:3, :] + cb_ref[...]


def _ffn_kernel(x_ref, xp_ref, xn_ref, gs_ref, sh_ref, wa_ref, wb_ref, cw_ref, cb_ref, wd_ref, g_ref,
                o_ref, h_ref, hh_ref, *, tpb):
    i, f = pl.program_id(0), pl.program_id(1)

    def normed(x):
        ms = jnp.mean(x * x, axis=-1, keepdims=True)
        return (x * lax.rsqrt(ms + EPS) * gs_ref[0] + sh_ref[0]).astype(BF16)

    @pl.when(f == 0)
    def _():
        h_ref[...] = normed(x_ref[...])
        hh_ref[0:8, :] = normed(xp_ref[...])
        hh_ref[8:16, :] = normed(xn_ref[...])

    wa = wa_ref[...]
    a = jnp.dot(h_ref[...], wa, preferred_element_type=F32)
    ah = jnp.dot(hh_ref[...], wa, preferred_element_type=F32)
    b = jnp.dot(h_ref[...], wb_ref[...], preferred_element_type=F32)
    pos = i % tpb
    conv = _dwconv3_tile(a, ah[0:8], ah[8:16], pos == 0, pos == tpb - 1, cw_ref, cb_ref)
    act = (conv * jax.nn.sigmoid(conv) * b).astype(BF16)
    part = jnp.dot(act, wd_ref[...], preferred_element_type=F32)

    @pl.when(f == 0)
    def _():
        o_ref[...] = part

    @pl.when(f > 0)
    def _():
        o_ref[...] += part

    @pl.when(f == pl.num_programs(1) - 1)
    def _():
        o_ref[...] = x_ref[...] + g_ref[0] * o_ref[...]


def ffn(x, nseq, gs, sh, w_up, conv_w, conv_b, w_dn, g):
    m, d = x.shape
    fh = w_dn.shape[0]
    nb = gs.shape[0]
    tm = _pick_tile(m // nseq, 1024, 8)
    tf = _pick_tile(fh, 256)
    tpb = (m // nseq) // tm
    tpm = (m // nb) // tm
    nf = fh // tf
    r8 = tm // 8
    last8 = m // 8 - 1
    mod = lambda i, f: (i // tpm, 0, 0)
    return pl.pallas_call(
        functools.partial(_ffn_kernel, tpb=tpb),
        out_shape=jax.ShapeDtypeStruct((m, d), F32),
        grid=(m // tm, nf),
        in_specs=[pl.BlockSpec((tm, d), lambda i, f: (i, 0), pipeline_mode=pl.Buffered(1)),
                  pl.BlockSpec((8, d), lambda i, f: (jnp.maximum(i * r8 - 1, 0), 0)),
                  pl.BlockSpec((8, d), lambda i, f: (jnp.minimum((i + 1) * r8, last8), 0)),
                  pl.BlockSpec((1, 1, d), mod), pl.BlockSpec((1, 1, d), mod),
                  pl.BlockSpec((d, tf), lambda i, f: (0, f)),
                  pl.BlockSpec((d, tf), lambda i, f: (0, f + nf)),
                  pl.BlockSpec((3, tf), lambda i, f: (0, f)),
                  pl.BlockSpec((1, tf), lambda i, f: (0, f)),
                  pl.BlockSpec((tf, d), lambda i, f: (f, 0)),
                  pl.BlockSpec((1, 1, d), mod)],
        out_specs=pl.BlockSpec((tm, d), lambda i, f: (i, 0)),
        scratch_shapes=[pltpu.VMEM((tm, d), BF16), pltpu.VMEM((16, d), BF16)],
        compiler_params=_cparams("parallel", "arbitrary"),
        name="ffn",
    )(x, x, x, gs, sh, w_up, w_up, conv_w, conv_b.reshape(1, fh), w_dn, g)


def _rmsnorm_kernel(x_ref, g_ref, o_ref):
    x = x_ref[...]
    ms = jnp.mean(x * x, axis=-1, keepdims=True)
    o_ref[...] = x * lax.rsqrt(ms + EPS) * g_ref[...]


def final_rmsnorm(x, g):
    m, d = x.shape
    tm = _pick_tile(m, 1024, 8)
    return pl.pallas_call(
        _rmsnorm_kernel,
        out_shape=jax.ShapeDtypeStruct((m, d), F32),
        grid=(m // tm,),
        in_specs=[pl.BlockSpec((tm, d), lambda i: (i, 0)), pl.BlockSpec((1, d), lambda i: (0, 0))],
        out_specs=pl.BlockSpec((tm, d), lambda i: (i, 0)),
        compiler_params=_cparams("parallel"),
        name="final_rmsnorm",
    )(x, g.reshape(1, d))


S5_CHUNK = 256


def _const_spec(shape):
    zeros = (0,) * len(shape)
    return pl.BlockSpec(shape, lambda b, k: zeros, pipeline_mode=pl.Buffered(1))


def _s5_kernel(*refs, ncc, reverse, finish):
    if finish:
        (uc_ref, ux_ref, bblk_ref, cblk_ref, enr_ref, eni_ref, epr_ref, epi_ref, ac_ref,
         pc_ref, px_ref, d_ref, gw_ref, gb_ref, yc_ref, yx_ref, h_ref) = refs
    else:
        (uc_ref, ux_ref, bblk_ref, cblk_ref, enr_ref, eni_ref, epr_ref, epi_ref, ac_ref,
         yc_ref, yx_ref, h_ref) = refs
    k = pl.program_id(1)
    t = uc_ref.shape[1]
    gn = enr_ref.shape[1]

    @pl.when(k == 0)
    def _():
        h_ref[...] = jnp.zeros_like(h_ref)

    is_ctx = k < ncc
    u = jnp.where(is_ctx, uc_ref[0], ux_ref[0])
    row = lax.broadcasted_iota(jnp.int32, (t, t), 0)
    col = lax.broadcasted_iota(jnp.int32, (t, t), 1)
    tri = jnp.where((col >= row) if reverse else (col <= row), 1.0, 0.0).astype(BF16)
    last = 0 if reverse else t - 1
    nblk, wb, sb2 = bblk_ref.shape
    sb = sb2 // 2
    ys = []
    for j in range(nblk):
        sc = slice(j * sb, (j + 1) * sb)
        bu = jnp.dot(u[:, j * wb:(j + 1) * wb].astype(BF16), bblk_ref[j], preferred_element_type=F32)
        br, bi = bu[:, :sb], bu[:, sb:]
        enr, eni = enr_ref[:, sc], eni_ref[:, sc]
        z = jnp.concatenate([br * enr - bi * eni, br * eni + bi * enr], axis=1).astype(BF16)
        cs = jnp.dot(tri, z, preferred_element_type=F32)
        hr, hi = h_ref[0:1, sc], h_ref[1:2, sc]
        acr, aci = ac_ref[0:1, sc], ac_ref[1:2, sc]
        sr = cs[:, :sb] + (hr * acr - hi * aci)
        si = cs[:, sb:] + (hr * aci + hi * acr)
        epr, epi = epr_ref[:, sc], epi_ref[:, sc]
        xr = sr * epr - si * epi
        xi = sr * epi + si * epr
        h_ref[0:1, sc] = xr[last:last + 1, :]
        h_ref[1:2, sc] = xi[last:last + 1, :]
        xs = jnp.concatenate([xr, xi], axis=1).astype(BF16)
        ys.append(jnp.dot(xs, cblk_ref[j], preferred_element_type=F32))
    y = ys[0] if nblk == 1 else jnp.concatenate(ys, axis=1)
    if finish:
        y = y + jnp.where(is_ctx, pc_ref[0], px_ref[0]) + d_ref[...] * u
        y = jax.nn.gelu(y)
        gate = jnp.dot(y.astype(BF16), gw_ref[...], preferred_element_type=F32) + gb_ref[...]
        y = y * jax.nn.sigmoid(gate)

    @pl.when(is_ctx)
    def _():
        yc_ref[0] = y.astype(yc_ref.dtype)

    @pl.when(jnp.logical_not(is_ctx))
    def _():
        yx_ref[0] = y.astype(yx_ref.dtype)


def _s5_tables(a_re, a_im, log_step, b_re, b_im, c_re, c_im, t, reverse):
    g, n, h = b_re.shape
    gpb = min(g, max(1, LANE // h))
    nblk = g // gpb
    eye = jnp.eye(gpb, dtype=F32)
    dt = jnp.exp(log_step)[:, None]
    ldr, ldi = a_re * dt, a_im * dt
    mag = jnp.exp(ldr)
    abr, abi = mag * jnp.cos(ldi), mag * jnp.sin(ldi)
    den = a_re * a_re + a_im * a_im
    nr, ni = abr - 1.0, abi
    fr = (nr * a_re + ni * a_im) / den
    fi = (ni * a_re - nr * a_im) / den
    bbr = fr[..., None] * b_re - fi[..., None] * b_im
    bbi = fr[..., None] * b_im + fi[..., None] * b_re
    blk_b = lambda m: jnp.einsum('jgnh,gk->jghkn', m.reshape(nblk, gpb, n, h), eye).reshape(nblk, gpb * h, gpb * n)
    bblk = jnp.concatenate([blk_b(bbr), blk_b(bbi)], axis=2).astype(BF16)
    blk_c = lambda m: jnp.einsum('jghn,gk->jgnkh', m.reshape(nblk, gpb, h, n), eye).reshape(nblk, gpb * n, gpb * h)
    cblk = jnp.concatenate([blk_c(c_re), -blk_c(c_im)], axis=1).astype(BF16)
    centre = float(t // 2)
    pos = jnp.arange(t, dtype=F32)[:, None]
    steps = ((t - pos) if reverse else (pos + 1.0)) - centre
    lr, li = ldr.reshape(1, g * n), ldi.reshape(1, g * n)
    er, ei = steps * lr, steps * li
    epr, epi = jnp.exp(er) * jnp.cos(ei), jnp.exp(er) * jnp.sin(ei)
    enr, eni = jnp.exp(-er) * jnp.cos(ei), -jnp.exp(-er) * jnp.sin(ei)
    ac = jnp.concatenate([jnp.exp(centre * lr) * jnp.cos(centre * li),
                          jnp.exp(centre * lr) * jnp.sin(centre * li)], axis=0)
    return bblk, cblk, (enr, eni, epr, epi, ac)


def _s5_pass(pc, px, col, width, tabs, reverse, fin=None, out_dtype=F32):
    bsz, nctx = pc.shape[:2]
    seq = px.shape[1]
    t = S5_CHUNK
    ncc, ncx = nctx // t, seq // t
    bblk, cblk, (enr, eni, epr, epi, ac) = tabs
    gn = enr.shape[1]
    if reverse:
        cidx = lambda k: jnp.maximum(ncc - 1 - k, 0)
        xidx = lambda k: jnp.minimum(ncx - 1 - (k - ncc), ncx - 1)
    else:
        cidx = lambda k: jnp.minimum(k, ncc - 1)
        xidx = lambda k: jnp.maximum(k - ncc, 0)
    in_specs = [pl.BlockSpec((1, t, width), lambda b, k: (b, cidx(k), col)),
                pl.BlockSpec((1, t, width), lambda b, k: (b, xidx(k), col)),
                _const_spec(bblk.shape), _const_spec(cblk.shape),
                _const_spec(enr.shape), _const_spec(eni.shape), _const_spec(epr.shape), _const_spec(epi.shape),
                _const_spec(ac.shape)]
    args = [pc, px, bblk, cblk, enr, eni, epr, epi, ac]
    if fin is not None:
        prev_c, prev_x, dvec, gw, gb = fin
        in_specs += [pl.BlockSpec((1, t, width), lambda b, k: (b, cidx(k), 0)),
                     pl.BlockSpec((1, t, width), lambda b, k: (b, xidx(k), 0)),
                     _const_spec((1, width)), _const_spec(gw.shape), _const_spec((1, width))]
        args += [prev_c, prev_x, dvec.reshape(1, width), gw, gb.reshape(1, width)]
    return pl.pallas_call(
        functools.partial(_s5_kernel, ncc=ncc, reverse=reverse, finish=fin is not None),
        out_shape=(jax.ShapeDtypeStruct((bsz, nctx, width), out_dtype),
                   jax.ShapeDtypeStruct((bsz, seq, width), out_dtype)),
        grid=(bsz, ncc + ncx),
        in_specs=in_specs,
        out_specs=(pl.BlockSpec((1, t, width), lambda b, k: (b, cidx(k), 0)),
                   pl.BlockSpec((1, t, width), lambda b, k: (b, xidx(k), 0))),
        scratch_shapes=[pltpu.VMEM((2, gn), F32)],
        compiler_params=_cparams("parallel", "arbitrary"),
        name="s5_rev" if reverse else "s5_fwd",
    )(*args)


def s5_mixer(pc, px, width, p):
    tabs = [_s5_tables(p['s5_a_re'][d], p['s5_a_im'][d], p['s5_log_step'][d], p['s5_b_re'][d], p['s5_b_im'][d],
                       p['s5_c_re'][d], p['s5_c_im'][d], S5_CHUNK, d == 1) for d in range(2)]
    bc, bx = _s5_pass(pc, px, 0, width, tabs[1], reverse=True)
    return _s5_pass(pc, px, 0, width, tabs[0], reverse=False,
                    fin=(bc, bx, p['s5_d'], p['s5_glu_w'].astype(BF16), p['s5_glu_b']))


GLA_STEP = 256


def _gla_kernel(*refs, nh, dk, dv, ncs, reverse, finish, scale):
    if finish:
        k_ref, q_ref, v_ref, l_ref, wg_ref, bg_ref, prev_ref, ng_ref, o_ref, s_ref = refs
    else:
        k_ref, q_ref, v_ref, l_ref, wg_ref, bg_ref, o_ref, s_ref = refs
    ts = k_ref.shape[1]
    nchunk = ts // GLA_CHUNK

    @pl.when(pl.program_id(1) == 0)
    def _():
        s_ref[...] = jnp.zeros_like(s_ref)

    pre = jnp.dot(l_ref[0], wg_ref[...], preferred_element_type=F32,
                  precision=lax.Precision.HIGHEST) + bg_ref[...]
    g = jax.nn.log_sigmoid(pre) * (1.0 / GLA_GATE_TEMP)
    row = lax.broadcasted_iota(jnp.int32, (ts, ts), 0)
    col = lax.broadcasted_iota(jnp.int32, (ts, ts), 1)
    same = (row // GLA_CHUNK) == (col // GLA_CHUNK)
    order = (col >= row) if reverse else (col <= row)
    cum = jnp.where(same & order, 1.0, 0.0)
    b = jnp.dot(cum, g, preferred_element_type=F32, precision=lax.Precision.HIGHEST)
    eb = jnp.exp(b)
    enb = jnp.exp(-b)
    q_in = q_ref[0] * scale * eb
    k_out = k_ref[0] * enb
    v = v_ref[0]
    ci = lax.broadcasted_iota(jnp.int32, (GLA_CHUNK, GLA_CHUNK), 0)
    cj = lax.broadcasted_iota(jnp.int32, (GLA_CHUNK, GLA_CHUNK), 1)
    keep = (cj >= ci) if reverse else (cj <= ci)
    nt = (((1,), (1,)), ((), ()))
    tn = (((0,), (0,)), ((), ()))
    for c in (range(nchunk - 1, -1, -1) if reverse else range(nchunk)):
        r0 = c * GLA_CHUNK
        rows = slice(r0, r0 + GLA_CHUNK)
        end = r0 if reverse else r0 + GLA_CHUNK - 1
        etot = jnp.exp(b[end:end + 1, :])
        k_kv = k_out[rows] * etot
        for h in range(nh):
            kc = slice(h * dk, (h + 1) * dk)
            vc = slice(h * dv, (h + 1) * dv)
            qh = q_in[rows, kc].astype(BF16)
            kh = k_out[rows, kc].astype(BF16)
            kkv = k_kv[:, kc].astype(BF16)
            vh = v[rows, vc].astype(BF16)
            st = s_ref[h]
            sc = lax.dot_general(qh, kh, nt, preferred_element_type=F32)
            sc = jnp.where(keep, sc, 0.0).astype(BF16)
            o = (jnp.dot(sc, vh, preferred_element_type=F32)
                 + lax.dot_general(qh, st.astype(BF16), nt, preferred_element_type=F32))
            kvt = lax.dot_general(vh, kkv, tn, preferred_element_type=F32)
            s_ref[h] = st * etot[:, kc] + kvt
            if finish:
                o = o + prev_ref[0, rows, vc]
                ms = jnp.mean(o * o, axis=-1, keepdims=True)
                o = o * lax.rsqrt(ms + EPS) * ng_ref[...]
            o_ref[0, rows, vc] = o


def _gla_pass(gall, wgp, bg, ncs, nh, dk, dv, gpad, reverse, fin=None):
    bsz, ntot, _ = gall.shape
    ts = GLA_STEP
    nsteps = ntot // ts
    key, val = nh * dk, nh * dv
    if reverse:
        idx = lambda k: jnp.where(k < ncs, ncs - 1 - k, nsteps - 1 - (k - ncs))
    else:
        idx = lambda k: k
    in_specs = [pl.BlockSpec((1, ts, key), lambda b, k: (b, idx(k), 0)),
                pl.BlockSpec((1, ts, key), lambda b, k: (b, idx(k), 1)),
                pl.BlockSpec((1, ts, val), lambda b, k: (b, idx(k), 1)),
                pl.BlockSpec((1, ts, gpad), lambda b, k: (b, idx(k), (2 * key + val) // gpad)),
                _const_spec(wgp.shape), _const_spec((1, key))]
    args = [gall, gall, gall, gall, wgp, bg.reshape(1, key)]
    if fin is not None:
        prev, ng = fin
        in_specs += [pl.BlockSpec((1, ts, val), lambda b, k: (b, idx(k), 0)), _const_spec((1, dv))]
        args += [prev, ng.reshape(1, dv)]
    return pl.pallas_call(
        functools.partial(_gla_kernel, nh=nh, dk=dk, dv=dv, ncs=ncs, reverse=reverse, finish=fin is not None,
                          scale=dk ** -0.5),
        out_shape=jax.ShapeDtypeStruct((bsz, ntot, val), F32),
        grid=(bsz, nsteps),
        in_specs=in_specs,
        out_specs=pl.BlockSpec((1, ts, val), lambda b, k: (b, idx(k), 0)),
        scratch_shapes=[pltpu.VMEM((nh, dv, dk), F32)],
        compiler_params=_cparams("parallel", "arbitrary"),
        name="gla_rev" if reverse else "gla_fwd",
    )(*args)


def gla_mixer(pc, px, c0, q4, gpad, p):
    nctx = pc.shape[1]
    dv = p['gla_norm_g'].shape[0]
    nh = (2 * q4) // dv
    dk = q4 // nh
    rank = p['gla_wg'].shape[1]
    gall = jnp.concatenate([pc[..., c0:], _to_col_major(px[..., c0:])], axis=1)
    wgp = [jnp.zeros((gpad, q4), F32).at[d * rank:(d + 1) * rank].set(p['gla_wg'][d]) for d in range(2)]
    kw = dict(ncs=nctx // GLA_STEP, nh=nh, dk=dk, dv=dv, gpad=gpad)
    o_rev = _gla_pass(gall, wgp[1], p['gla_bg'][1], reverse=True, **kw)
    o = _gla_pass(gall, wgp[0], p['gla_bg'][0], reverse=False, fin=(o_rev, p['gla_norm_g']), **kw)
    return o[:, :nctx], _from_col_major(o[:, nctx:])


HY_N2 = 128
HY_PAD = 8


def _hy_dims(n):
    n1 = 2 * n // HY_N2
    k1 = n1 // 2 + 1
    k1p = -(-k1 // 8) * 8
    return n1, k1, k1p


def _hy_tables(n):
    n1, k1, k1p = _hy_dims(n)
    big = 2 * n
    kk = jnp.arange(k1p, dtype=jnp.int32)[:, None]
    mm = jnp.arange(n1, dtype=jnp.int32)[None, :]
    ang = (2.0 * math.pi / n1) * ((kk * mm) % n1).astype(F32)
    valid = (kk < k1).astype(F32)
    f1 = jnp.concatenate([jnp.cos(ang) * valid, -jnp.sin(ang) * valid], axis=0)
    wk = jnp.where((kk == 0) | (kk == n1 // 2), 1.0, 2.0) * valid / big
    f1inv = jnp.concatenate([(jnp.cos(ang) * wk).T, (-jnp.sin(ang) * wk).T], axis=1)
    k1i = jnp.arange(k1, dtype=jnp.int32)[:, None, None]
    k2i = jnp.arange(HY_N2, dtype=jnp.int32)[None, :, None]
    n2i = jnp.arange(HY_N2, dtype=jnp.int32)[None, None, :]
    ph = (2.0 * math.pi / big) * ((n2i * (k1i + n1 * k2i)) % big).astype(F32)
    gr, gi = jnp.cos(ph), -jnp.sin(ph)
    gs = jnp.concatenate([gr, gi], axis=1)
    gts = jnp.concatenate([gr.swapaxes(1, 2), gi.swapaxes(1, 2)], axis=1)
    return f1.astype(BF16), f1inv.astype(BF16), gs.astype(BF16), gts.astype(BF16)


def _hy_stage1(src_ref, f1, a_r, a_i, nslab, k1p):
    pitch = HY_N2 + HY_PAD

    def body(i, carry):
        n2 = 2 * i
        rows = jnp.concatenate([src_ref[pl.ds(n2, nslab, stride=pitch), :],
                                src_ref[pl.ds(n2 + 1, nslab, stride=pitch), :]], axis=1)
        out = jnp.dot(f1, rows.astype(BF16), preferred_element_type=F32)
        base = pl.multiple_of(n2 * k1p, 8)
        a_r[pl.ds(base, k1p), :] = out[:k1p, :LANE]
        a_i[pl.ds(base, k1p), :] = out[k1p:, :LANE]
        base1 = pl.multiple_of(base + k1p, 8)
        a_r[pl.ds(base1, k1p), :] = out[:k1p, LANE:]
        a_i[pl.ds(base1, k1p), :] = out[k1p:, LANE:]
        return carry

    lax.fori_loop(0, HY_N2 // 2, body, 0, unroll=4)


def _hy_stage2(a_r, a_i, gs_ref, k, k1p):
    ar = a_r[pl.ds(k, HY_N2, stride=k1p), :]
    ai = a_i[pl.ds(k, HY_N2, stride=k1p), :]
    rhs = jnp.concatenate([ar, ai], axis=1).astype(BF16)
    out = jnp.dot(gs_ref[k], rhs, preferred_element_type=F32)
    h = HY_N2
    return out[:h, :LANE] - out[h:, LANE:], out[:h, LANE:] + out[h:, :LANE]


def _hyena_conv_kernel(y_ref, g_ref, wy_ref, by_ref, wg_ref, bg_ref, bias_ref, hr_ref, hi_ref,
                       f1_ref, f1inv_ref, gs_ref, gts_ref, o_ref, ypad, zbuf, a_r, a_i, *, conv_y):
    n = y_ref.shape[1]
    nslab = n // HY_N2
    pitch = HY_N2 + HY_PAD
    k1 = gs_ref.shape[0]
    k1p = f1_ref.shape[0] // 2
    h = HY_N2

    def slab_conv(ref, i, w_ref, b_ref):
        r0 = pl.multiple_of(i * HY_N2, HY_N2)
        prev8 = ref[0, pl.ds(pl.multiple_of(jnp.maximum(r0 - 8, 0), 8), 8), :]
        next8 = ref[0, pl.ds(pl.multiple_of(jnp.minimum(r0 + HY_N2, n - 8), 8), 8), :]
        return _dwconv3_tile(ref[0, pl.ds(r0, HY_N2), :], prev8, next8, i == 0, i == nslab - 1, w_ref, b_ref)

    def y_slab(i):
        if conv_y:
            return slab_conv(y_ref, i, wy_ref, by_ref)
        return y_ref[0, pl.ds(pl.multiple_of(i * HY_N2, HY_N2), HY_N2), :]

    def fill(i, carry):
        ypad[pl.ds(pl.multiple_of(i * pitch, 8), HY_N2), :] = y_slab(i)
        return carry

    lax.fori_loop(0, nslab, fill, 0, unroll=2)
    _hy_stage1(ypad, f1_ref[...], a_r, a_i, nslab, k1p)

    def freq_fwd(k, carry):
        xr, xi = _hy_stage2(a_r, a_i, gs_ref, k, k1p)
        hr, hi = hr_ref[0, k], hi_ref[0, k]
        zbuf[k] = jnp.concatenate([xr * hr - xi * hi, xr * hi + xi * hr], axis=1).astype(BF16)
        return carry

    lax.fori_loop(0, k1, freq_fwd, 0, unroll=2)

    def freq_inv(k, carry):
        out = jnp.dot(gts_ref[k], zbuf[k], preferred_element_type=F32)
        a_r[pl.ds(k, HY_N2, stride=k1p), :] = out[:h, :LANE] + out[h:, LANE:]
        a_i[pl.ds(k, HY_N2, stride=k1p), :] = out[:h, LANE:] - out[h:, :LANE]
        return carry

    lax.fori_loop(0, k1, freq_inv, 0, unroll=2)
    f1inv = f1inv_ref[...][:nslab]

    def inv1(i, carry):
        n2 = 2 * i
        b0 = pl.multiple_of(n2 * k1p, 8)
        b1 = pl.multiple_of(b0 + k1p, 8)
        rhs = jnp.concatenate(
            [jnp.concatenate([a_r[pl.ds(b0, k1p), :], a_i[pl.ds(b0, k1p), :]], axis=0),
             jnp.concatenate([a_r[pl.ds(b1, k1p), :], a_i[pl.ds(b1, k1p), :]], axis=0)], axis=1).astype(BF16)
        out = jnp.dot(f1inv, rhs, preferred_element_type=F32)
        ypad[pl.ds(n2, nslab, stride=pitch), :] = out[:, :LANE]
        ypad[pl.ds(n2 + 1, nslab, stride=pitch), :] = out[:, LANE:]
        return carry

    lax.fori_loop(0, HY_N2 // 2, inv1, 0, unroll=4)

    def finish(i, carry):
        p0 = pl.multiple_of(i * pitch, 8)
        gate = slab_conv(g_ref, i, wg_ref, bg_ref)
        o_ref[0, pl.ds(pl.multiple_of(i * HY_N2, HY_N2), HY_N2), :] = gate * (ypad[pl.ds(p0, HY_N2), :]
                                                                                + y_slab(i) * bias_ref[0])
        return carry

    lax.fori_loop(0, nslab, finish, 0, unroll=2)


def _hyena_spectrum_kernel(f_ref, f1_ref, gs_ref, hr_ref, hi_ref, a_r, a_i, fpad):
    big = f_ref.shape[1]
    nslab = big // HY_N2
    pitch = HY_N2 + HY_PAD
    k1 = gs_ref.shape[0]
    k1p = f1_ref.shape[0] // 2

    def fill(i, carry):
        fpad[pl.ds(pl.multiple_of(i * pitch, 8), HY_N2), :] = f_ref[0, pl.ds(pl.multiple_of(i * HY_N2, HY_N2), HY_N2), :]
        return carry

    lax.fori_loop(0, nslab, fill, 0, unroll=2)
    _hy_stage1(fpad, f1_ref[...], a_r, a_i, nslab, k1p)

    def freq(k, carry):
        xr, xi = _hy_stage2(a_r, a_i, gs_ref, k, k1p)
        hr_ref[0, k] = xr
        hi_ref[0, k] = xi
        return carry

    lax.fori_loop(0, k1, freq, 0, unroll=2)


def _one(shape, index_map):
    return pl.BlockSpec(shape, index_map, pipeline_mode=pl.Buffered(1))


def hyena_spectrum(filt, tabs):
    f1, _, gs, _ = tabs
    r, big, w = filt.shape
    ns = w // LANE
    n1, k1, k1p = _hy_dims(big // 2)
    shp = jax.ShapeDtypeStruct((r * ns, k1, HY_N2, LANE), F32)
    spec_o = pl.BlockSpec((1, k1, HY_N2, LANE), lambda i, j: (i * ns + j, 0, 0, 0))
    return pl.pallas_call(
        _hyena_spectrum_kernel,
        out_shape=(shp, shp),
        grid=(r, ns),
        in_specs=[_one((1, big, LANE), lambda i, j: (i, 0, j)),
                  _one(f1.shape, lambda i, j: (0, 0)),
                  _one(gs.shape, lambda i, j: (0, 0, 0))],
        out_specs=(spec_o, spec_o),
        scratch_shapes=[pltpu.VMEM((HY_N2 * k1p, LANE), F32), pltpu.VMEM((HY_N2 * k1p, LANE), F32),
                        pltpu.VMEM((n1 * (HY_N2 + HY_PAD), LANE), F32)],
        compiler_params=_cparams("parallel", "parallel"),
        name="hyena_spectrum",
    )(filt, f1, gs)


def hyena_order(y, ycol, g, gcol, conv_w, conv_b, cy, cg, bias, hr, hi, order, tabs, conv_y):
    f1, f1inv, gs, gts = tabs
    bsz, n = y.shape[:2]
    w = bias.shape[-1]
    ns = w // LANE
    n1, k1, k1p = _hy_dims(n)
    f1d = f1[:, :n1 // 2]
    pitch = HY_N2 + HY_PAD
    nslab = n // HY_N2
    cw = lambda c: _one((3, LANE), lambda j, b: (0, c + j))
    cb = lambda c: _one((1, LANE), lambda j, b: (0, c + j))
    hspec = _one((1, k1, HY_N2, LANE), lambda j, b: (order * ns + j, 0, 0, 0))
    return pl.pallas_call(
        functools.partial(_hyena_conv_kernel, conv_y=conv_y),
        out_shape=jax.ShapeDtypeStruct((bsz, n, w), F32),
        grid=(ns, bsz),
        in_specs=[_one((1, n, LANE), lambda j, b: (b, 0, ycol + j)),
                  _one((1, n, LANE), lambda j, b: (b, 0, gcol + j)),
                  cw(cy), cb(cy), cw(cg), cb(cg),
                  _one((1, 1, LANE), lambda j, b: (order, 0, j)),
                  hspec, hspec,
                  _one(f1d.shape, lambda j, b: (0, 0)), _one(f1inv.shape, lambda j, b: (0, 0)),
                  _one(gs.shape, lambda j, b: (0, 0, 0)), _one(gts.shape, lambda j, b: (0, 0, 0))],
        out_specs=pl.BlockSpec((1, n, LANE), lambda j, b: (b, 0, j)),
        scratch_shapes=[pltpu.VMEM((nslab * pitch, LANE), F32), pltpu.VMEM((k1, HY_N2, 2 * LANE), BF16),
                        pltpu.VMEM((HY_N2 * k1p, LANE), F32), pltpu.VMEM((HY_N2 * k1p, LANE), F32)],
        compiler_params=pltpu.CompilerParams(dimension_semantics=("parallel", "parallel"),
                                             vmem_limit_bytes=HY_VMEM_LIMIT),
        name="hyena_order",
    )(y, g, conv_w, conv_b, conv_w, conv_b, bias.reshape(bias.shape[0], 1, w), hr, hi, f1d, f1inv, gs, gts)


HY_VMEM_LIMIT = 60 * 1024 * 1024


def hyena_latent(px, c0, filt, p):
    w = p['hy_bias'].shape[-1]
    ns = w // LANE
    n = px.shape[1]
    tabs = _hy_tables(n)
    hr, hi = hyena_spectrum(filt, tabs)
    cw, cb = p['hy_conv_w'], p['hy_conv_b'].reshape(1, -1)
    b0 = c0 // LANE
    y1 = hyena_order(px, b0, px, b0 + ns, cw, cb, 0, ns, p['hy_bias'], hr, hi, 0, tabs, True)
    return hyena_order(y1, 0, px, b0 + 2 * ns, cw, cb, 0, 2 * ns, p['hy_bias'], hr, hi, 1, tabs, False)


def _hyena_ctx_kernel(zv_ref, z1_ref, z2_ref, wv_ref, bv_ref, w1_ref, b1_ref, w2_ref, b2_ref, bias_ref,
                      filt_ref, ff_ref, finv_ref, o_ref):
    n = zv_ref.shape[1]
    hp = lax.Precision.HIGHEST
    zero8 = jnp.zeros((8, LANE), F32)
    conv = lambda ref, w, b: _dwconv3_tile(ref[0], zero8, zero8, True, True, w, b)
    ff = ff_ref[...]
    kp = ff.shape[0] // 2
    y = conv(zv_ref, wv_ref, bv_ref)
    for o, (g_ref, w, b) in enumerate(((z1_ref, w1_ref, b1_ref), (z2_ref, w2_ref, b2_ref))):
        hsp = jnp.dot(ff, filt_ref[o], preferred_element_type=F32, precision=hp)
        ysp = jnp.dot(ff[:, :n], y, preferred_element_type=F32, precision=hp)
        hr, hi, yr, yi = hsp[:kp], hsp[kp:], ysp[:kp], ysp[kp:]
        z = jnp.concatenate([yr * hr - yi * hi, yr * hi + yi * hr], axis=0)
        cv = jnp.dot(finv_ref[...], z, preferred_element_type=F32, precision=hp)
        y = conv(g_ref, w, b) * (cv + y * bias_ref[o])
    o_ref[0] = y


def hyena_context(pc, c0, filt, p):
    bsz, n = pc.shape[:2]
    w = p['hy_bias'].shape[-1]
    ns = w // LANE
    kp = -(-(n + 1) // 8) * 8
    kk = jnp.arange(kp, dtype=jnp.int32)[:, None]
    mm = jnp.arange(2 * n, dtype=jnp.int32)[None, :]
    ang = (math.pi / n) * ((kk * mm) % (2 * n)).astype(F32)
    valid = (kk <= n).astype(F32)
    ff = jnp.concatenate([jnp.cos(ang) * valid, -jnp.sin(ang) * valid], axis=0)
    wk = jnp.where((kk == 0) | (kk == n), 1.0, 2.0) * valid / (2 * n)
    finv = jnp.concatenate([(jnp.cos(ang) * wk).T[:n], (-jnp.sin(ang) * wk).T[:n]], axis=1)
    b0 = c0 // LANE
    zs = lambda c: pl.BlockSpec((1, n, LANE), lambda j, b: (b, 0, b0 + c + j))
    cw = lambda c: pl.BlockSpec((3, LANE), lambda j, b: (0, c + j))
    cb = lambda c: pl.BlockSpec((1, LANE), lambda j, b: (0, c + j))
    conv_w, conv_b = p['hy_conv_w'], p['hy_conv_b'].reshape(1, -1)
    return pl.pallas_call(
        _hyena_ctx_kernel,
        out_shape=jax.ShapeDtypeStruct((bsz, n, w), F32),
        grid=(ns, bsz),
        in_specs=[zs(0), zs(ns), zs(2 * ns), cw(0), cb(0), cw(ns), cb(ns), cw(2 * ns), cb(2 * ns),
                  pl.BlockSpec((2, 1, LANE), lambda j, b: (0, 0, j)),
                  pl.BlockSpec((2, 2 * n, LANE), lambda j, b: (0, 0, j)),
                  pl.BlockSpec(ff.shape, lambda j, b: (0, 0)), pl.BlockSpec(finv.shape, lambda j, b: (0, 0))],
        out_specs=pl.BlockSpec((1, n, LANE), lambda j, b: (b, 0, j)),
        compiler_params=_cparams("parallel", "parallel"),
        name="hyena_context",
    )(pc, pc, pc, conv_w, conv_b, conv_w, conv_b, conv_w, conv_b, p['hy_bias'].reshape(2, 1, w), filt, ff, finv)


def _to_col_major(t):
    bsz, n = t.shape[:2]
    rows = n // GRID_W
    return t.reshape(bsz, rows, GRID_W, *t.shape[2:]).swapaxes(1, 2).reshape(bsz, n, *t.shape[2:])


def _from_col_major(t):
    bsz, n = t.shape[:2]
    rows = n // GRID_W
    return t.reshape(bsz, GRID_W, rows, *t.shape[2:]).swapaxes(1, 2).reshape(bsz, n, *t.shape[2:])


def _filter_positions(n):
    r = jnp.arange(2 * n, dtype=jnp.int32)
    return jnp.where(r < n, r, 2 * n - r).astype(F32)


def _filter_feats(n):
    pos = _filter_positions(n)
    t = pos / max(n - 1, 1)
    freqs = jnp.linspace(1e-4, HY_POS_FREQS - 1, HY_POS_FREQS, dtype=F32)
    ang = (2.0 * math.pi / n) * pos[:, None] * freqs[None]
    feats = jnp.concatenate([t[:, None], jnp.cos(ang), -jnp.sin(ang)], axis=-1)
    return jnp.pad(feats, ((0, 0), (0, LANE - feats.shape[1])))


def _filter_hidden_kernel(f_ref, w1_ref, b1_ref, s1_ref, w2_ref, b2_ref, s2_ref, o_ref):
    hp = lax.Precision.HIGHEST
    h = jnp.sin(s1_ref[...] * (jnp.dot(f_ref[...], w1_ref[...], preferred_element_type=F32, precision=hp)
                               + b1_ref[...]))
    o_ref[...] = jnp.sin(s2_ref[...] * (jnp.dot(h, w2_ref[...], preferred_element_type=F32, precision=hp)
                                        + b2_ref[...]))


def _filter_out_kernel(h_ref, wf_ref, wb_ref, bf_ref, bb_ref, rate_ref, o_ref, *, n, chunk):
    nchunk = 2 * n // chunk
    inv_span = 1.0 / max(n - 1, 1)

    def emit(c, acc):
        r0 = pl.multiple_of(c * chunk, chunk)
        past = c < nchunk // 2
        w = jnp.where(past, wf_ref[...], wb_ref[...])
        bias = jnp.where(past, bf_ref[...], bb_ref[...])
        val = jnp.dot(h_ref[pl.ds(r0, chunk), :], w, preferred_element_type=F32,
                      precision=lax.Precision.HIGHEST) + bias
        r = r0 + lax.broadcasted_iota(jnp.int32, (chunk, LANE), 0)
        t = jnp.where(r < n, r, 2 * n - r).astype(F32) * inv_span
        val = jnp.where(r == n, 0.0, val * jnp.exp(-t * rate_ref[...]))
        o_ref[0, pl.ds(r0, chunk), :] = val
        return acc + jnp.sum(jnp.abs(val), axis=0, keepdims=True)

    total = lax.fori_loop(0, nchunk, emit, jnp.zeros((1, LANE), F32))
    scale = 1.0 / (total + EPS)

    def rescale(c, carry):
        r0 = pl.multiple_of(c * chunk, chunk)
        o_ref[0, pl.ds(r0, chunk), :] = o_ref[0, pl.ds(r0, chunk), :] * scale
        return carry

    lax.fori_loop(0, nchunk, rescale, 0)


def hyena_filters(n, p):
    order, width = p['hy_bias'].shape
    ns = width // LANE
    hid = p['hy_f_w2'].shape[0]
    feats = _filter_feats(n)
    padc = lambda a: jnp.pad(a.reshape(1, -1), ((0, 0), (0, LANE - a.shape[-1])))
    w1 = jnp.pad(p['hy_f_w1'], ((0, LANE - p['hy_f_w1'].shape[0]), (0, LANE - hid)))
    w2 = jnp.pad(p['hy_f_w2'], ((0, LANE - hid), (0, LANE - hid)))
    w3 = jnp.pad(p['hy_f_w3'], ((0, LANE - hid), (0, 0)))
    tr = _pick_tile(2 * n, 2048, 8)
    full = lambda i: (0, 0)
    hidden = pl.pallas_call(
        _filter_hidden_kernel,
        out_shape=jax.ShapeDtypeStruct((2 * n, LANE), F32),
        grid=(2 * n // tr,),
        in_specs=[pl.BlockSpec((tr, LANE), lambda i: (i, 0)),
                  pl.BlockSpec((LANE, LANE), full), pl.BlockSpec((1, LANE), full), pl.BlockSpec((1, LANE), full),
                  pl.BlockSpec((LANE, LANE), full), pl.BlockSpec((1, LANE), full), pl.BlockSpec((1, LANE), full)],
        out_specs=pl.BlockSpec((tr, LANE), lambda i: (i, 0)),
        compiler_params=_cparams("parallel"),
        name="hyena_filter_hidden",
    )(feats, w1, padc(p['hy_f_b1']), padc(p['hy_f_freq1']), w2, padc(p['hy_f_b2']), padc(p['hy_f_freq2']))
    rates = jnp.abs(jnp.linspace(math.log(HY_DECAY_TARGET) / HY_LONG_DECAY_PCT,
                                 math.log(HY_DECAY_TARGET) / HY_SHORT_DECAY_PCT, width, dtype=F32)).reshape(1, width)
    b3 = p['hy_f_b3'].reshape(1, -1)
    return pl.pallas_call(
        functools.partial(_filter_out_kernel, n=n, chunk=min(1024, n)),
        out_shape=jax.ShapeDtypeStruct((order, 2 * n, width), F32),
        grid=(order, ns),
        in_specs=[_one((2 * n, LANE), lambda o, j: (0, 0)),
                  pl.BlockSpec((LANE, LANE), lambda o, j: (0, 2 * o * ns + j)),
                  pl.BlockSpec((LANE, LANE), lambda o, j: (0, (2 * o + 1) * ns + j)),
                  pl.BlockSpec((1, LANE), lambda o, j: (0, 2 * o * ns + j)),
                  pl.BlockSpec((1, LANE), lambda o, j: (0, (2 * o + 1) * ns + j)),
                  pl.BlockSpec((1, LANE), lambda o, j: (0, j))],
        out_specs=pl.BlockSpec((1, 2 * n, LANE), lambda o, j: (o, 0, j)),
        compiler_params=_cparams("parallel", "parallel"),
        name="hyena_filter_out",
    )(hidden, w3, w3, b3, b3, rates)


def kernel(x, c, ctx, c_ctx, w_mod, b_mod, norm1_g, norm2_g, w_in, s5_a_re, s5_a_im, s5_log_step, s5_b_re, s5_b_im, s5_c_re, s5_c_im, s5_d, s5_glu_w, s5_glu_b, hy_conv_w, hy_conv_b, hy_f_w1, hy_f_b1, hy_f_freq1, hy_f_w2, hy_f_b2, hy_f_freq2, hy_f_w3, hy_f_b3, hy_bias, gla_wg, gla_bg, gla_norm_g, w_branch, w_out, ff_w_up, ff_conv_w, ff_conv_b, ff_w_down, final_norm_g):
    bsz, seq, d = x.shape
    nctx = ctx.shape[1]
    depth = w_mod.shape[0]
    q4 = d // 4
    rank2 = 2 * gla_wg.shape[2]
    gpad = max(LANE, q4 // 2)
    assert bsz + 1 <= 8

    c_gk = q4
    c_gv = 2 * q4
    c_gg = 4 * q4
    c_gq = c_gg + rank2
    c_gr = c_gq + q4
    c_hy = c_gr + 2 * q4
    c_mg = c_hy + 3 * q4

    rows = jnp.zeros((8, d), F32).at[:bsz].set(c).at[bsz].set(c_ctx)
    mod = modulation(rows, w_mod, b_mod)

    xs = x.reshape(bsz * seq, d)
    cs = ctx.reshape(bsz * nctx, d)
    for l in range(depth):
        ctx_out = l < depth - 1
        sh1, s1, g1, sh2, s2, g2 = [mod[l, :, i * d:(i + 1) * d][:, None, :] for i in range(6)]
        gs1 = norm1_g[l] * (1.0 + s1)
        gs2 = norm2_g[l] * (1.0 + s2)
        bx = slice(0, bsz)
        bc = slice(bsz, bsz + 1)

        wl = w_in[l]
        w_pack = jnp.concatenate([
            wl[:, 0:c_gk], wl[:, c_hy:c_mg], wl[:, c_gr:c_hy], wl[:, c_gk:c_gv], wl[:, c_gq:c_gr],
            wl[:, c_gv:c_gg], wl[:, c_gg:c_gq],
            jnp.zeros((d, gpad - rank2), F32)], axis=1).astype(BF16)
        w_gate = wl[:, c_mg:].reshape(d, 3, d).swapaxes(0, 1).astype(BF16)
        wb = w_branch[l].astype(BF16)
        w_o = w_out[l].astype(BF16)
        w_up = ff_w_up[l].astype(BF16)
        w_dn = ff_w_down[l].astype(BF16)

        p = dict(s5_a_re=s5_a_re[l], s5_a_im=s5_a_im[l], s5_log_step=s5_log_step[l],
                 s5_b_re=s5_b_re[l], s5_b_im=s5_b_im[l], s5_c_re=s5_c_re[l], s5_c_im=s5_c_im[l],
                 s5_d=s5_d[l], s5_glu_w=s5_glu_w[l], s5_glu_b=s5_glu_b[l],
                 hy_conv_w=hy_conv_w[l], hy_conv_b=hy_conv_b[l], hy_f_w1=hy_f_w1[l], hy_f_b1=hy_f_b1[l],
                 hy_f_freq1=hy_f_freq1[l], hy_f_w2=hy_f_w2[l], hy_f_b2=hy_f_b2[l],
                 hy_f_freq2=hy_f_freq2[l], hy_f_w3=hy_f_w3[l], hy_f_b3=hy_f_b3[l], hy_bias=hy_bias[l],
                 gla_wg=gla_wg[l], gla_bg=gla_bg[l], gla_norm_g=gla_norm_g[l])

        px = norm_matmul(xs, gs1[bx], sh1[bx], w_pack).reshape(bsz, seq, -1)
        pc = norm_matmul(cs, gs1[bc], sh1[bc], w_pack).reshape(bsz, nctx, -1)

        ya_c, ya_x = s5_mixer(pc, px, q4, p)
        oc_c, oc_x = gla_mixer(pc, px, 6 * q4, q4, gpad, p)
        yb_x = hyena_latent(px, q4, hyena_filters(seq, p), p)

        flat = lambda t: t.reshape(-1, t.shape[-1])
        mx = merge(xs, gs1[bx], sh1[bx], flat(ya_x), flat(yb_x), flat(oc_x), flat(px), 2, w_gate, wb)
        xs = matmul_residual(mx, w_o, xs, g1[bx])
        xs = ffn(xs, bsz, gs2[bx], sh2[bx], w_up, ff_conv_w[l], ff_conv_b[l], w_dn, g2[bx])

        if ctx_out:
            yb_c = hyena_context(pc, q4, hyena_filters(nctx, p), p)
            mc = merge(cs, gs1[bc], sh1[bc], flat(ya_c), flat(yb_c), flat(oc_c), flat(pc), 2, w_gate, wb)
            cs = matmul_residual(mc, w_o, cs, g1[bc])
            cs = ffn(cs, bsz, gs2[bc], sh2[bc], w_up, ff_conv_w[l], ff_conv_b[l], w_dn, g2[bc])

    return final_rmsnorm(xs, final_norm_g).reshape(bsz, seq, d)
```

```python
import functools
import math

import jax
import jax.numpy as jnp
from jax import lax
from jax.experimental import pallas as pl
from jax.experimental.pallas import tpu as pltpu

F32 = jnp.float32
BF16 = jnp.bfloat16
EPS = 1e-6
GRID_W = 64
S5_GROUP = 16
GLA_CHUNK = 64
GLA_GATE_TEMP = 16.0
HY_POS_FREQS = 16
HY_DECAY_TARGET = 1e-2
HY_SHORT_DECAY_PCT = 0.3
HY_LONG_DECAY_PCT = 1.5
LANE = 128
VMEM_LIMIT = 56 * 1024 * 1024
BIG_VMEM_LIMIT = 60 * 1024 * 1024
FFN_TF = 512


def _pick_tile(n, cap, mult=LANE):
    best = None
    for t in range(mult, min(n, cap) + 1, mult):
        if n % t == 0:
            best = t
    assert best is not None, (n, cap, mult)
    return best


def _col_blocks(w, tn):
    nj = w.shape[-1] // tn
    return jnp.moveaxis(w.reshape(*w.shape[:-1], nj, tn), -2, 0)


def _cparams(*sem):
    return pltpu.CompilerParams(dimension_semantics=sem, vmem_limit_bytes=VMEM_LIMIT)


def _mod_kernel(r_ref, w_ref, b_ref, o_ref):
    r = r_ref[...]
    s = r * jax.nn.sigmoid(r)
    o_ref[0] = jnp.dot(s, w_ref[0], preferred_element_type=F32,
                       precision=lax.Precision.HIGHEST) + b_ref[0]


def modulation(rows, w_mod, b_mod):
    depth, d, n = w_mod.shape
    tn = _pick_tile(n, 1024)
    return pl.pallas_call(
        _mod_kernel,
        out_shape=jax.ShapeDtypeStruct((depth, 8, n), F32),
        grid=(depth, n // tn),
        in_specs=[pl.BlockSpec((8, d), lambda l, j: (0, 0)),
                  pl.BlockSpec((1, d, tn), lambda l, j: (l, 0, j)),
                  pl.BlockSpec((1, 1, tn), lambda l, j: (l, 0, j))],
        out_specs=pl.BlockSpec((1, 8, tn), lambda l, j: (l, 0, j)),
        compiler_params=_cparams("parallel", "parallel"),
        name="modulation",
    )(rows, w_mod, b_mod.reshape(depth, 1, n))


def _norm_mm_kernel(x_ref, gs_ref, sh_ref, w_ref, o_ref, h_ref):
    @pl.when(pl.program_id(1) == 0)
    def _():
        x = x_ref[...]
        ms = jnp.mean(x * x, axis=-1, keepdims=True)
        h_ref[...] = (x * lax.rsqrt(ms + EPS) * gs_ref[0] + sh_ref[0]).astype(BF16)

    o_ref[...] = jnp.dot(h_ref[...], w_ref[0], preferred_element_type=F32).astype(o_ref.dtype)


def norm_matmul(x, gs, sh, w, out_dtype=F32):
    m, d = x.shape
    n = w.shape[1]
    nb = gs.shape[0]
    tm = _pick_tile(m // nb, 1024, 8)
    tn = _pick_tile(n, 1024)
    tpb = (m // nb) // tm
    return pl.pallas_call(
        _norm_mm_kernel,
        out_shape=jax.ShapeDtypeStruct((m, n), out_dtype),
        grid=(m // tm, n // tn),
        in_specs=[pl.BlockSpec((tm, d), lambda i, j: (i, 0)),
                  pl.BlockSpec((1, 1, d), lambda i, j: (i // tpb, 0, 0)),
                  pl.BlockSpec((1, 1, d), lambda i, j: (i // tpb, 0, 0)),
                  pl.BlockSpec((1, d, tn), lambda i, j: (j, 0, 0))],
        out_specs=pl.BlockSpec((tm, tn), lambda i, j: (i, j)),
        scratch_shapes=[pltpu.VMEM((tm, d), BF16)],
        compiler_params=_cparams("parallel", "arbitrary"),
        name="norm_matmul",
    )(x, gs, sh, _col_blocks(w, tn))


def _mm_res_kernel(a_ref, w_ref, r_ref, g_ref, o_ref):
    acc = jnp.dot(a_ref[...], w_ref[0], preferred_element_type=F32)
    o_ref[...] = r_ref[...] + g_ref[0] * acc


def matmul_residual(a, w, res, g):
    m, k = a.shape
    n = w.shape[1]
    nb = g.shape[0]
    tm = _pick_tile(m // nb, 1024, 8)
    tn = _pick_tile(n, 1024)
    tpb = (m // nb) // tm
    return pl.pallas_call(
        _mm_res_kernel,
        out_shape=jax.ShapeDtypeStruct((m, n), F32),
        grid=(m // tm, n // tn),
        in_specs=[pl.BlockSpec((tm, k), lambda i, j: (i, 0)),
                  pl.BlockSpec((1, k, tn), lambda i, j: (j, 0, 0)),
                  pl.BlockSpec((tm, tn), lambda i, j: (i, j)),
                  pl.BlockSpec((1, 1, tn), lambda i, j: (i // tpb, 0, j))],
        out_specs=pl.BlockSpec((tm, tn), lambda i, j: (i, j)),
        compiler_params=_cparams("parallel", "arbitrary"),
        name="matmul_residual",
    )(a, _col_blocks(w, tn), res, g)


def _merge_kernel(x_ref, gs_ref, sh_ref, ya_ref, yb_ref, oc_ref, r_ref, wg_ref, wb_ref, o_ref, h_ref, y_ref):
    wa, wb = ya_ref.shape[1], yb_ref.shape[1]

    @pl.when(pl.program_id(1) == 0)
    def _():
        x = x_ref[...]
        ms = jnp.mean(x * x, axis=-1, keepdims=True)
        h_ref[...] = (x * lax.rsqrt(ms + EPS) * gs_ref[0] + sh_ref[0]).astype(BF16)
        r = r_ref[...]
        y_ref[:, 0:wa] = ya_ref[...].astype(BF16)
        y_ref[:, wa:wa + wb] = yb_ref[...].astype(BF16)
        y_ref[:, wa + wb:] = (oc_ref[...] * (r * jax.nn.sigmoid(r))).astype(BF16)

    h = h_ref[...]
    m = None
    for i, (lo, hi) in enumerate(((0, wa), (wa, wa + wb), (wa + wb, y_ref.shape[1]))):
        gate = jax.nn.sigmoid(jnp.dot(h, wg_ref[0, i], preferred_element_type=F32))
        br = jnp.dot(y_ref[:, lo:hi], wb_ref[0, lo:hi, :], preferred_element_type=F32)
        m = gate * br if m is None else m + gate * br
    o_ref[...] = m.astype(o_ref.dtype)


def merge(x, gs, sh, ya, yb, oc, pr, rcol, wg, wb):
    m, d = x.shape
    nb = gs.shape[0]
    tm = _pick_tile(m // nb, 512, 8)
    tn = _pick_tile(d, 512)
    tpb = (m // nb) // tm
    row = lambda i, j: (i, 0)
    wy = wb.shape[0]
    return pl.pallas_call(
        _merge_kernel,
        out_shape=jax.ShapeDtypeStruct((m, d), BF16),
        grid=(m // tm, d // tn),
        in_specs=[pl.BlockSpec((tm, d), row),
                  pl.BlockSpec((1, 1, d), lambda i, j: (i // tpb, 0, 0)),
                  pl.BlockSpec((1, 1, d), lambda i, j: (i // tpb, 0, 0)),
                  pl.BlockSpec((tm, ya.shape[1]), row),
                  pl.BlockSpec((tm, yb.shape[1]), row),
                  pl.BlockSpec((tm, oc.shape[1]), row),
                  pl.BlockSpec((tm, oc.shape[1]), lambda i, j: (i, rcol)),
                  pl.BlockSpec((1, 3, d, tn), lambda i, j: (j, 0, 0, 0)),
                  pl.BlockSpec((1, wy, tn), lambda i, j: (j, 0, 0))],
        out_specs=pl.BlockSpec((tm, tn), lambda i, j: (i, j)),
        scratch_shapes=[pltpu.VMEM((tm, d), BF16), pltpu.VMEM((tm, wy), BF16)],
        compiler_params=_cparams("parallel", "arbitrary"),
        name="merge",
    )(x, gs, sh, ya, yb, oc, pr, _col_blocks(wg, tn), _col_blocks(wb, tn))


def _dwconv3_tile(a, prev8, next8, first, last, w_ref, cb_ref):
    tr = a.shape[0]
    row = lax.broadcasted_iota(jnp.int32, a.shape, 0)
    before = jnp.where(first, 0.0, prev8[7:8, :])
    after = jnp.where(last, 0.0, next8[0:1, :])
    prev = jnp.where(row == 0, before, pltpu.roll(a, 1, 0))
    nxt = jnp.where(row == tr - 1, after, pltpu.roll(a, tr - 1, 0))
    return prev * w_ref[0:1, :] + a * w_ref[1:2, :] + nxt * w_ref[2:3, :] + cb_ref[...]


def _ffn_kernel(x_ref, xp_ref, xn_ref, gs_ref, sh_ref, wa_ref, wb_ref, cw_ref, cb_ref, wd_ref, g_ref,
                o_ref, h_ref, hh_ref, *, tpb):
    i, f = pl.program_id(0), pl.program_id(1)

    def normed(x):
        ms = jnp.mean(x * x, axis=-1, keepdims=True)
        return (x * lax.rsqrt(ms + EPS) * gs_ref[0] + sh_ref[0]).astype(BF16)

    @pl.when(f == 0)
    def _():
        h_ref[...] = normed(x_ref[...])
        hh_ref[0:8, :] = normed(xp_ref[...])
        hh_ref[8:16, :] = normed(xn_ref[...])
        o_ref[...] = jnp.zeros_like(o_ref)

    wa = wa_ref[0]
    a = jnp.dot(h_ref[...], wa, preferred_element_type=F32)
    ah = jnp.dot(hh_ref[...], wa, preferred_element_type=F32)
    b = jnp.dot(h_ref[...], wb_ref[0], preferred_element_type=F32)
    pos = i % tpb
    conv = _dwconv3_tile(a, ah[0:8], ah[8:16], pos == 0, pos == tpb - 1, cw_ref, cb_ref)
    act = (conv * jax.nn.sigmoid(conv) * b).astype(BF16)
    half = o_ref.shape[1] // 2
    for lo in (0, half):
        o_ref[:, lo:lo + half] += jnp.dot(act, wd_ref[:, lo:lo + half], preferred_element_type=F32)

    @pl.when(f == pl.num_programs(1) - 1)
    def _():
        o_ref[...] = x_ref[...] + g_ref[0] * o_ref[...]


def ffn(x, nseq, gs, sh, w_up, conv_w, conv_b, w_dn, g):
    m, d = x.shape
    fh = w_dn.shape[0]
    nb = gs.shape[0]
    tm = _pick_tile(m // nseq, 1024, 8)
    tf = _pick_tile(fh, FFN_TF)
    tpb = (m // nseq) // tm
    tpm = (m // nb) // tm
    nf = fh // tf
    w_up = w_up.reshape(d, 2 * nf, tf).swapaxes(0, 1)
    r8 = tm // 8
    last8 = m // 8 - 1
    mod = lambda i, f: (i // tpm, 0, 0)
    return pl.pallas_call(
        functools.partial(_ffn_kernel, tpb=tpb),
        out_shape=jax.ShapeDtypeStruct((m, d), F32),
        grid=(m // tm, nf),
        in_specs=[pl.BlockSpec((tm, d), lambda i, f: (i, 0), pipeline_mode=pl.Buffered(1)),
                  pl.BlockSpec((8, d), lambda i, f: (jnp.maximum(i * r8 - 1, 0), 0)),
                  pl.BlockSpec((8, d), lambda i, f: (jnp.minimum((i + 1) * r8, last8), 0)),
                  pl.BlockSpec((1, 1, d), mod), pl.BlockSpec((1, 1, d), mod),
                  pl.BlockSpec((1, d, tf), lambda i, f: (f, 0, 0)),
                  pl.BlockSpec((1, d, tf), lambda i, f: (f + nf, 0, 0)),
                  pl.BlockSpec((3, tf), lambda i, f: (0, f)),
                  pl.BlockSpec((1, tf), lambda i, f: (0, f)),
                  pl.BlockSpec((tf, d), lambda i, f: (f, 0)),
                  pl.BlockSpec((1, 1, d), mod)],
        out_specs=pl.BlockSpec((tm, d), lambda i, f: (i, 0)),
        scratch_shapes=[pltpu.VMEM((tm, d), BF16), pltpu.VMEM((16, d), BF16)],
        compiler_params=pltpu.CompilerParams(dimension_semantics=("parallel", "arbitrary"),
                                             vmem_limit_bytes=BIG_VMEM_LIMIT),
        name="ffn",
    )(x, x, x, gs, sh, w_up, w_up, conv_w, conv_b.reshape(1, fh), w_dn, g)


def _rmsnorm_kernel(x_ref, g_ref, o_ref):
    x = x_ref[...]
    ms = jnp.mean(x * x, axis=-1, keepdims=True)
    o_ref[...] = x * lax.rsqrt(ms + EPS) * g_ref[...]


def final_rmsnorm(x, g):
    m, d = x.shape
    tm = _pick_tile(m, 1024, 8)
    return pl.pallas_call(
        _rmsnorm_kernel,
        out_shape=jax.ShapeDtypeStruct((m, d), F32),
        grid=(m // tm,),
        in_specs=[pl.BlockSpec((tm, d), lambda i: (i, 0)), pl.BlockSpec((1, d), lambda i: (0, 0))],
        out_specs=pl.BlockSpec((tm, d), lambda i: (i, 0)),
        compiler_params=_cparams("parallel"),
        name="final_rmsnorm",
    )(x, g.reshape(1, d))


S5_CHUNK = 256


def _const_spec(shape):
    zeros = (0,) * len(shape)
    return pl.BlockSpec(shape, lambda b, k: zeros, pipeline_mode=pl.Buffered(1))


def _s5_kernel(*refs, ncc, reverse, finish):
    if finish:
        (uc_ref, ux_ref, bblk_ref, cblk_ref, enr_ref, eni_ref, epr_ref, epi_ref, ac_ref,
         pc_ref, px_ref, d_ref, gw_ref, gb_ref, yc_ref, yx_ref, h_ref) = refs
    else:
        (uc_ref, ux_ref, bblk_ref, cblk_ref, enr_ref, eni_ref, epr_ref, epi_ref, ac_ref,
         yc_ref, yx_ref, h_ref) = refs
    k = pl.program_id(1)
    t = uc_ref.shape[1]
    gn = enr_ref.shape[1]

    @pl.when(k == 0)
    def _():
        h_ref[...] = jnp.zeros_like(h_ref)

    is_ctx = k < ncc
    u = jnp.where(is_ctx, uc_ref[0], ux_ref[0])
    row = lax.broadcasted_iota(jnp.int32, (t, t), 0)
    col = lax.broadcasted_iota(jnp.int32, (t, t), 1)
    tri = jnp.where((col >= row) if reverse else (col <= row), 1.0, 0.0).astype(BF16)
    last = 0 if reverse else t - 1
    nblk, wb, sb2 = bblk_ref.shape
    sb = sb2 // 2
    ys = []
    for j in range(nblk):
        sc = slice(j * sb, (j + 1) * sb)
        bu = jnp.dot(u[:, j * wb:(j + 1) * wb].astype(BF16), bblk_ref[j], preferred_element_type=F32)
        br, bi = bu[:, :sb], bu[:, sb:]
        enr, eni = enr_ref[:, sc], eni_ref[:, sc]
        z = jnp.concatenate([br * enr - bi * eni, br * eni + bi * enr], axis=1).astype(BF16)
        cs = jnp.dot(tri, z, preferred_element_type=F32)
        hr, hi = h_ref[0:1, sc], h_ref[1:2, sc]
        acr, aci = ac_ref[0:1, sc], ac_ref[1:2, sc]
        sr = cs[:, :sb] + (hr * acr - hi * aci)
        si = cs[:, sb:] + (hr * aci + hi * acr)
        epr, epi = epr_ref[:, sc], epi_ref[:, sc]
        xr = sr * epr - si * epi
        xi = sr * epi + si * epr
        h_ref[0:1, sc] = xr[last:last + 1, :]
        h_ref[1:2, sc] = xi[last:last + 1, :]
        xs = jnp.concatenate([xr, xi], axis=1).astype(BF16)
        ys.append(jnp.dot(xs, cblk_ref[j], preferred_element_type=F32))
    y = ys[0] if nblk == 1 else jnp.concatenate(ys, axis=1)
    if finish:
        y = y + jnp.where(is_ctx, pc_ref[0], px_ref[0]) + d_ref[...] * u
        y = jax.nn.gelu(y)
        gate = jnp.dot(y.astype(BF16), gw_ref[...], preferred_element_type=F32) + gb_ref[...]
        y = y * jax.nn.sigmoid(gate)

    @pl.when(is_ctx)
    def _():
        yc_ref[0] = y.astype(yc_ref.dtype)

    @pl.when(jnp.logical_not(is_ctx))
    def _():
        yx_ref[0] = y.astype(yx_ref.dtype)


def _s5_tables(a_re, a_im, log_step, b_re, b_im, c_re, c_im, t, reverse):
    g, n, h = b_re.shape
    gpb = min(g, max(1, LANE // h))
    nblk = g // gpb
    eye = jnp.eye(gpb, dtype=F32)
    dt = jnp.exp(log_step)[:, None]
    ldr, ldi = a_re * dt, a_im * dt
    mag = jnp.exp(ldr)
    abr, abi = mag * jnp.cos(ldi), mag * jnp.sin(ldi)
    den = a_re * a_re + a_im * a_im
    nr, ni = abr - 1.0, abi
    fr = (nr * a_re + ni * a_im) / den
    fi = (ni * a_re - nr * a_im) / den
    bbr = fr[..., None] * b_re - fi[..., None] * b_im
    bbi = fr[..., None] * b_im + fi[..., None] * b_re
    blk_b = lambda m: jnp.einsum('jgnh,gk->jghkn', m.reshape(nblk, gpb, n, h), eye).reshape(nblk, gpb * h, gpb * n)
    bblk = jnp.concatenate([blk_b(bbr), blk_b(bbi)], axis=2).astype(BF16)
    blk_c = lambda m: jnp.einsum('jghn,gk->jgnkh', m.reshape(nblk, gpb, h, n), eye).reshape(nblk, gpb * n, gpb * h)
    cblk = jnp.concatenate([blk_c(c_re), -blk_c(c_im)], axis=1).astype(BF16)
    centre = float(t // 2)
    pos = jnp.arange(t, dtype=F32)[:, None]
    steps = ((t - pos) if reverse else (pos + 1.0)) - centre
    lr, li = ldr.reshape(1, g * n), ldi.reshape(1, g * n)
    er, ei = steps * lr, steps * li
    epr, epi = jnp.exp(er) * jnp.cos(ei), jnp.exp(er) * jnp.sin(ei)
    enr, eni = jnp.exp(-er) * jnp.cos(ei), -jnp.exp(-er) * jnp.sin(ei)
    ac = jnp.concatenate([jnp.exp(centre * lr) * jnp.cos(centre * li),
                          jnp.exp(centre * lr) * jnp.sin(centre * li)], axis=0)
    return bblk, cblk, (enr, eni, epr, epi, ac)


def _s5_pass(pc, px, col, width, tabs, reverse, fin=None, out_dtype=F32):
    bsz, nctx = pc.shape[:2]
    seq = px.shape[1]
    t = S5_CHUNK
    ncc, ncx = nctx // t, seq // t
    bblk, cblk, (enr, eni, epr, epi, ac) = tabs
    gn = enr.shape[1]
    if reverse:
        cidx = lambda k: jnp.maximum(ncc - 1 - k, 0)
        xidx = lambda k: jnp.minimum(ncx - 1 - (k - ncc), ncx - 1)
    else:
        cidx = lambda k: jnp.minimum(k, ncc - 1)
        xidx = lambda k: jnp.maximum(k - ncc, 0)
    in_specs = [pl.BlockSpec((1, t, width), lambda b, k: (b, cidx(k), col)),
                pl.BlockSpec((1, t, width), lambda b, k: (b, xidx(k), col)),
                _const_spec(bblk.shape), _const_spec(cblk.shape),
                _const_spec(enr.shape), _const_spec(eni.shape), _const_spec(epr.shape), _const_spec(epi.shape),
                _const_spec(ac.shape)]
    args = [pc, px, bblk, cblk, enr, eni, epr, epi, ac]
    if fin is not None:
        prev_c, prev_x, dvec, gw, gb = fin
        in_specs += [pl.BlockSpec((1, t, width), lambda b, k: (b, cidx(k), 0)),
                     pl.BlockSpec((1, t, width), lambda b, k: (b, xidx(k), 0)),
                     _const_spec((1, width)), _const_spec(gw.shape), _const_spec((1, width))]
        args += [prev_c, prev_x, dvec.reshape(1, width), gw, gb.reshape(1, width)]
    return pl.pallas_call(
        functools.partial(_s5_kernel, ncc=ncc, reverse=reverse, finish=fin is not None),
        out_shape=(jax.ShapeDtypeStruct((bsz, nctx, width), out_dtype),
                   jax.ShapeDtypeStruct((bsz, seq, width), out_dtype)),
        grid=(bsz, ncc + ncx),
        in_specs=in_specs,
        out_specs=(pl.BlockSpec((1, t, width), lambda b, k: (b, cidx(k), 0)),
                   pl.BlockSpec((1, t, width), lambda b, k: (b, xidx(k), 0))),
        scratch_shapes=[pltpu.VMEM((2, gn), F32)],
        compiler_params=_cparams("parallel", "arbitrary"),
        name="s5_rev" if reverse else "s5_fwd",
    )(*args)


def s5_mixer(pc, px, width, p):
    tabs = [_s5_tables(p['s5_a_re'][d], p['s5_a_im'][d], p['s5_log_step'][d], p['s5_b_re'][d], p['s5_b_im'][d],
                       p['s5_c_re'][d], p['s5_c_im'][d], S5_CHUNK, d == 1) for d in range(2)]
    bc, bx = _s5_pass(pc, px, 0, width, tabs[1], reverse=True)
    return _s5_pass(pc, px, 0, width, tabs[0], reverse=False,
                    fin=(bc, bx, p['s5_d'], p['s5_glu_w'].astype(BF16), p['s5_glu_b']))


GLA_STEP = 256


def _gla_kernel(*refs, nh, dk, dv, ncs, reverse, finish, scale):
    if finish:
        k_ref, q_ref, v_ref, l_ref, wg_ref, bg_ref, prev_ref, ng_ref, o_ref, s_ref = refs
    else:
        k_ref, q_ref, v_ref, l_ref, wg_ref, bg_ref, o_ref, s_ref = refs
    ts = k_ref.shape[1]
    nchunk = ts // GLA_CHUNK

    @pl.when(pl.program_id(1) == 0)
    def _():
        s_ref[...] = jnp.zeros_like(s_ref)

    hp = lax.Precision.HIGHEST
    rank2 = wg_ref.shape[0]
    pre = jnp.dot(l_ref[0][:, :rank2], wg_ref[...], preferred_element_type=F32, precision=hp) + bg_ref[...]
    g = jax.nn.log_sigmoid(pre) * (1.0 / GLA_GATE_TEMP)
    ci = lax.broadcasted_iota(jnp.int32, (GLA_CHUNK, GLA_CHUNK), 0)
    cj = lax.broadcasted_iota(jnp.int32, (GLA_CHUNK, GLA_CHUNK), 1)
    keep = (cj >= ci) if reverse else (cj <= ci)
    cum = jnp.where(keep, 1.0, 0.0)
    b = jnp.concatenate([jnp.dot(cum, g[c * GLA_CHUNK:(c + 1) * GLA_CHUNK], preferred_element_type=F32, precision=hp)
                         for c in range(nchunk)], axis=0)
    eb = jnp.exp(b)
    enb = jnp.exp(-b)
    q_in = q_ref[0] * scale * eb
    k_out = k_ref[0] * enb
    v = v_ref[0]
    nt = (((1,), (1,)), ((), ()))
    tn = (((0,), (0,)), ((), ()))
    for c in (range(nchunk - 1, -1, -1) if reverse else range(nchunk)):
        r0 = c * GLA_CHUNK
        rows = slice(r0, r0 + GLA_CHUNK)
        end = r0 if reverse else r0 + GLA_CHUNK - 1
        etot = jnp.exp(b[end:end + 1, :])
        k_kv = k_out[rows] * etot
        for h in range(nh):
            kc = slice(h * dk, (h + 1) * dk)
            vc = slice(h * dv, (h + 1) * dv)
            qh = q_in[rows, kc].astype(BF16)
            kh = k_out[rows, kc].astype(BF16)
            kkv = k_kv[:, kc].astype(BF16)
            vh = v[rows, vc].astype(BF16)
            st = s_ref[h]
            sc = lax.dot_general(qh, kh, nt, preferred_element_type=F32)
            sc = jnp.where(keep, sc, 0.0).astype(BF16)
            o = (jnp.dot(sc, vh, preferred_element_type=F32)
                 + lax.dot_general(qh, st.astype(BF16), nt, preferred_element_type=F32))
            kvt = lax.dot_general(vh, kkv, tn, preferred_element_type=F32)
            s_ref[h] = st * etot[:, kc] + kvt
            if finish:
                o = o + prev_ref[0, rows, vc]
                ms = jnp.mean(o * o, axis=-1, keepdims=True)
                o = o * lax.rsqrt(ms + EPS) * ng_ref[...]
            o_ref[0, rows, vc] = o


def _gla_pass(gall, wgp, bg, ncs, nh, dk, dv, gpad, reverse, fin=None):
    bsz, ntot, _ = gall.shape
    ts = GLA_STEP
    nsteps = ntot // ts
    key, val = nh * dk, nh * dv
    if reverse:
        idx = lambda k: jnp.where(k < ncs, ncs - 1 - k, nsteps - 1 - (k - ncs))
    else:
        idx = lambda k: k
    in_specs = [pl.BlockSpec((1, ts, key), lambda b, k: (b, idx(k), 0)),
                pl.BlockSpec((1, ts, key), lambda b, k: (b, idx(k), 1)),
                pl.BlockSpec((1, ts, val), lambda b, k: (b, idx(k), 1)),
                pl.BlockSpec((1, ts, gpad), lambda b, k: (b, idx(k), (2 * key + val) // gpad)),
                _const_spec(wgp.shape), _const_spec((1, key))]
    args = [gall, gall, gall, gall, wgp, bg.reshape(1, key)]
    if fin is not None:
        prev, ng = fin
        in_specs += [pl.BlockSpec((1, ts, val), lambda b, k: (b, idx(k), 0)), _const_spec((1, dv))]
        args += [prev, ng.reshape(1, dv)]
    return pl.pallas_call(
        functools.partial(_gla_kernel, nh=nh, dk=dk, dv=dv, ncs=ncs, reverse=reverse, finish=fin is not None,
                          scale=dk ** -0.5),
        out_shape=jax.ShapeDtypeStruct((bsz, ntot, val), F32),
        grid=(bsz, nsteps),
        in_specs=in_specs,
        out_specs=pl.BlockSpec((1, ts, val), lambda b, k: (b, idx(k), 0)),
        scratch_shapes=[pltpu.VMEM((nh, dv, dk), F32)],
        compiler_params=_cparams("parallel", "arbitrary"),
        name="gla_rev" if reverse else "gla_fwd",
    )(*args)


def gla_mixer(pc, px, c0, q4, gpad, p):
    nctx = pc.shape[1]
    dv = p['gla_norm_g'].shape[0]
    nh = (2 * q4) // dv
    dk = q4 // nh
    rank = p['gla_wg'].shape[1]
    gall = jnp.concatenate([pc[..., c0:], _to_col_major(px[..., c0:])], axis=1)
    wgp = [jnp.zeros((2 * rank, q4), F32).at[d * rank:(d + 1) * rank].set(p['gla_wg'][d]) for d in range(2)]
    kw = dict(ncs=nctx // GLA_STEP, nh=nh, dk=dk, dv=dv, gpad=gpad)
    o_rev = _gla_pass(gall, wgp[1], p['gla_bg'][1], reverse=True, **kw)
    o = _gla_pass(gall, wgp[0], p['gla_bg'][0], reverse=False, fin=(o_rev, p['gla_norm_g']), **kw)
    return o[:, :nctx], _from_col_major(o[:, nctx:])


HY_N2 = 128
HY_PAD = 8


def _hy_dims(n):
    n1 = 2 * n // HY_N2
    k1 = n1 // 2 + 1
    k1p = -(-k1 // 8) * 8
    return n1, k1, k1p


def _hy_tables(n):
    n1, k1, k1p = _hy_dims(n)
    big = 2 * n
    kk = jnp.arange(k1p, dtype=jnp.int32)[:, None]
    mm = jnp.arange(n1, dtype=jnp.int32)[None, :]
    ang = (2.0 * math.pi / n1) * ((kk * mm) % n1).astype(F32)
    valid = (kk < k1).astype(F32)
    f1 = jnp.concatenate([jnp.cos(ang) * valid, -jnp.sin(ang) * valid], axis=0)
    wk = jnp.where((kk == 0) | (kk == n1 // 2), 1.0, 2.0) * valid / big
    f1inv = jnp.concatenate([(jnp.cos(ang) * wk).T, (-jnp.sin(ang) * wk).T], axis=1)
    k1i = jnp.arange(k1, dtype=jnp.int32)[:, None, None]
    k2i = jnp.arange(HY_N2, dtype=jnp.int32)[None, :, None]
    n2i = jnp.arange(HY_N2, dtype=jnp.int32)[None, None, :]
    ph = (2.0 * math.pi / big) * ((n2i * (k1i + n1 * k2i)) % big).astype(F32)
    gr, gi = jnp.cos(ph), -jnp.sin(ph)
    gs = jnp.concatenate([gr, gi], axis=1)
    gts = jnp.concatenate([gr.swapaxes(1, 2), gi.swapaxes(1, 2)], axis=1)
    return f1.astype(BF16), f1inv.astype(BF16), gs.astype(BF16), gts.astype(BF16)


def _hy_stage1(src_ref, f1, a_r, a_i, nslab, k1p):
    pitch = HY_N2 + HY_PAD

    def body(i, carry):
        n2 = 2 * i
        rows = jnp.concatenate([src_ref[pl.ds(n2, nslab, stride=pitch), :],
                                src_ref[pl.ds(n2 + 1, nslab, stride=pitch), :]], axis=1)
        out = jnp.dot(f1, rows.astype(BF16), preferred_element_type=F32)
        base = pl.multiple_of(n2 * k1p, 8)
        a_r[pl.ds(base, k1p), :] = out[:k1p, :LANE]
        a_i[pl.ds(base, k1p), :] = out[k1p:, :LANE]
        base1 = pl.multiple_of(base + k1p, 8)
        a_r[pl.ds(base1, k1p), :] = out[:k1p, LANE:]
        a_i[pl.ds(base1, k1p), :] = out[k1p:, LANE:]
        return carry

    lax.fori_loop(0, HY_N2 // 2, body, 0, unroll=4)


def _hy_stage2(a_r, a_i, gs_ref, k, k1p):
    ar = a_r[pl.ds(k, HY_N2, stride=k1p), :]
    ai = a_i[pl.ds(k, HY_N2, stride=k1p), :]
    rhs = jnp.concatenate([ar, ai], axis=1).astype(BF16)
    out = jnp.dot(gs_ref[k], rhs, preferred_element_type=F32)
    h = HY_N2
    return out[:h, :LANE] - out[h:, LANE:], out[:h, LANE:] + out[h:, :LANE]


def _hyena_conv_kernel(y_ref, g_ref, wy_ref, by_ref, wg_ref, bg_ref, bias_ref, hr_ref, hi_ref,
                       f1_ref, f1inv_ref, gs_ref, gts_ref, o_ref, ypad, zbuf, a_r, a_i, *, conv_y):
    n = y_ref.shape[1]
    nslab = n // HY_N2
    pitch = HY_N2 + HY_PAD
    k1 = gs_ref.shape[0]
    k1p = f1_ref.shape[0] // 2
    h = HY_N2

    def slab_conv(ref, i, w_ref, b_ref):
        r0 = pl.multiple_of(i * HY_N2, HY_N2)
        prev8 = ref[0, pl.ds(pl.multiple_of(jnp.maximum(r0 - 8, 0), 8), 8), :]
        next8 = ref[0, pl.ds(pl.multiple_of(jnp.minimum(r0 + HY_N2, n - 8), 8), 8), :]
        return _dwconv3_tile(ref[0, pl.ds(r0, HY_N2), :], prev8, next8, i == 0, i == nslab - 1, w_ref, b_ref)

    def y_slab(i):
        if conv_y:
            return slab_conv(y_ref, i, wy_ref, by_ref)
        return y_ref[0, pl.ds(pl.multiple_of(i * HY_N2, HY_N2), HY_N2), :]

    def fill(i, carry):
        ypad[pl.ds(pl.multiple_of(i * pitch, 8), HY_N2), :] = y_slab(i)
        return carry

    lax.fori_loop(0, nslab, fill, 0, unroll=2)
    _hy_stage1(ypad, f1_ref[...], a_r, a_i, nslab, k1p)

    def freq_fwd(k, carry):
        xr, xi = _hy_stage2(a_r, a_i, gs_ref, k, k1p)
        hr, hi = hr_ref[0, k], hi_ref[0, k]
        zbuf[k] = jnp.concatenate([xr * hr - xi * hi, xr * hi + xi * hr], axis=1).astype(BF16)
        return carry

    lax.fori_loop(0, k1, freq_fwd, 0, unroll=2)

    def freq_inv(k, carry):
        out = jnp.dot(gts_ref[k], zbuf[k], preferred_element_type=F32)
        a_r[pl.ds(k, HY_N2, stride=k1p), :] = out[:h, :LANE] + out[h:, LANE:]
        a_i[pl.ds(k, HY_N2, stride=k1p), :] = out[:h, LANE:] - out[h:, :LANE]
        return carry

    lax.fori_loop(0, k1, freq_inv, 0, unroll=2)
    f1inv = f1inv_ref[...][:nslab]

    def inv1(i, carry):
        n2 = 2 * i
        b0 = pl.multiple_of(n2 * k1p, 8)
        b1 = pl.multiple_of(b0 + k1p, 8)
        rhs = jnp.concatenate(
            [jnp.concatenate([a_r[pl.ds(b0, k1p), :], a_i[pl.ds(b0, k1p), :]], axis=0),
             jnp.concatenate([a_r[pl.ds(b1, k1p), :], a_i[pl.ds(b1, k1p), :]], axis=0)], axis=1).astype(BF16)
        out = jnp.dot(f1inv, rhs, preferred_element_type=F32)
        ypad[pl.ds(n2, nslab, stride=pitch), :] = out[:, :LANE]
        ypad[pl.ds(n2 + 1, nslab, stride=pitch), :] = out[:, LANE:]
        return carry

    lax.fori_loop(0, HY_N2 // 2, inv1, 0, unroll=4)

    def finish(i, carry):
        p0 = pl.multiple_of(i * pitch, 8)
        gate = slab_conv(g_ref, i, wg_ref, bg_ref)
        o_ref[0, pl.ds(pl.multiple_of(i * HY_N2, HY_N2), HY_N2), :] = gate * (ypad[pl.ds(p0, HY_N2), :]
                                                                                + y_slab(i) * bias_ref[0])
        return carry

    lax.fori_loop(0, nslab, finish, 0, unroll=2)


def _hyena_spectrum_kernel(f_ref, f1_ref, gs_ref, hr_ref, hi_ref, a_r, a_i, fpad):
    big = f_ref.shape[1]
    nslab = big // HY_N2
    pitch = HY_N2 + HY_PAD
    k1 = gs_ref.shape[0]
    k1p = f1_ref.shape[0] // 2

    def fill(i, carry):
        fpad[pl.ds(pl.multiple_of(i * pitch, 8), HY_N2), :] = f_ref[0, pl.ds(pl.multiple_of(i * HY_N2, HY_N2), HY_N2), :]
        return carry

    lax.fori_loop(0, nslab, fill, 0, unroll=2)
    _hy_stage1(fpad, f1_ref[...], a_r, a_i, nslab, k1p)

    def freq(k, carry):
        xr, xi = _hy_stage2(a_r, a_i, gs_ref, k, k1p)
        hr_ref[0, k] = xr
        hi_ref[0, k] = xi
        return carry

    lax.fori_loop(0, k1, freq, 0, unroll=2)


def _one(shape, index_map):
    return pl.BlockSpec(shape, index_map, pipeline_mode=pl.Buffered(1))


def hyena_spectrum(filt, tabs):
    f1, _, gs, _ = tabs
    r, big, w = filt.shape
    ns = w // LANE
    n1, k1, k1p = _hy_dims(big // 2)
    shp = jax.ShapeDtypeStruct((r * ns, k1, HY_N2, LANE), F32)
    spec_o = pl.BlockSpec((1, k1, HY_N2, LANE), lambda i, j: (i * ns + j, 0, 0, 0))
    return pl.pallas_call(
        _hyena_spectrum_kernel,
        out_shape=(shp, shp),
        grid=(r, ns),
        in_specs=[_one((1, big, LANE), lambda i, j: (i, 0, j)),
                  _one(f1.shape, lambda i, j: (0, 0)),
                  _one(gs.shape, lambda i, j: (0, 0, 0))],
        out_specs=(spec_o, spec_o),
        scratch_shapes=[pltpu.VMEM((HY_N2 * k1p, LANE), F32), pltpu.VMEM((HY_N2 * k1p, LANE), F32),
                        pltpu.VMEM((n1 * (HY_N2 + HY_PAD), LANE), F32)],
        compiler_params=_cparams("parallel", "parallel"),
        name="hyena_spectrum",
    )(filt, f1, gs)


def hyena_order(y, ycol, g, gcol, conv_w, conv_b, cy, cg, bias, hr, hi, order, tabs, conv_y):
    f1, f1inv, gs, gts = tabs
    bsz, n = y.shape[:2]
    w = bias.shape[-1]
    ns = w // LANE
    n1, k1, k1p = _hy_dims(n)
    f1d = f1[:, :n1 // 2]
    pitch = HY_N2 + HY_PAD
    nslab = n // HY_N2
    cw = lambda c: _one((3, LANE), lambda j, b: (0, c + j))
    cb = lambda c: _one((1, LANE), lambda j, b: (0, c + j))
    hspec = _one((1, k1, HY_N2, LANE), lambda j, b: (order * ns + j, 0, 0, 0))
    return pl.pallas_call(
        functools.partial(_hyena_conv_kernel, conv_y=conv_y),
        out_shape=jax.ShapeDtypeStruct((bsz, n, w), F32),
        grid=(ns, bsz),
        in_specs=[_one((1, n, LANE), lambda j, b: (b, 0, ycol + j)),
                  _one((1, n, LANE), lambda j, b: (b, 0, gcol + j)),
                  cw(cy), cb(cy), cw(cg), cb(cg),
                  _one((1, 1, LANE), lambda j, b: (order, 0, j)),
                  hspec, hspec,
                  _one(f1d.shape, lambda j, b: (0, 0)), _one(f1inv.shape, lambda j, b: (0, 0)),
                  _one(gs.shape, lambda j, b: (0, 0, 0)), _one(gts.shape, lambda j, b: (0, 0, 0))],
        out_specs=pl.BlockSpec((1, n, LANE), lambda j, b: (b, 0, j)),
        scratch_shapes=[pltpu.VMEM((nslab * pitch, LANE), F32), pltpu.VMEM((k1, HY_N2, 2 * LANE), BF16),
                        pltpu.VMEM((HY_N2 * k1p, LANE), F32), pltpu.VMEM((HY_N2 * k1p, LANE), F32)],
        compiler_params=pltpu.CompilerParams(dimension_semantics=("parallel", "parallel"),
                                             vmem_limit_bytes=BIG_VMEM_LIMIT),
        name="hyena_order",
    )(y, g, conv_w, conv_b, conv_w, conv_b, bias.reshape(bias.shape[0], 1, w), hr, hi, f1d, f1inv, gs, gts)


def hyena_latent(px, c0, filt, p):
    w = p['hy_bias'].shape[-1]
    ns = w // LANE
    n = px.shape[1]
    tabs = _hy_tables(n)
    hr, hi = hyena_spectrum(filt, tabs)
    cw, cb = p['hy_conv_w'], p['hy_conv_b'].reshape(1, -1)
    b0 = c0 // LANE
    y1 = hyena_order(px, b0, px, b0 + ns, cw, cb, 0, ns, p['hy_bias'], hr, hi, 0, tabs, True)
    return hyena_order(y1, 0, px, b0 + 2 * ns, cw, cb, 0, 2 * ns, p['hy_bias'], hr, hi, 1, tabs, False)


def _hyena_ctx_kernel(zv_ref, z1_ref, z2_ref, wv_ref, bv_ref, w1_ref, b1_ref, w2_ref, b2_ref, bias_ref,
                      filt_ref, ff_ref, finv_ref, o_ref):
    n = zv_ref.shape[1]
    hp = lax.Precision.HIGHEST
    zero8 = jnp.zeros((8, LANE), F32)
    conv = lambda ref, w, b: _dwconv3_tile(ref[0], zero8, zero8, True, True, w, b)
    ff = ff_ref[...]
    kp = ff.shape[0] // 2
    y = conv(zv_ref, wv_ref, bv_ref)
    for o, (g_ref, w, b) in enumerate(((z1_ref, w1_ref, b1_ref), (z2_ref, w2_ref, b2_ref))):
        hsp = jnp.dot(ff, filt_ref[o], preferred_element_type=F32, precision=hp)
        ysp = jnp.dot(ff[:, :n], y, preferred_element_type=F32, precision=hp)
        hr, hi, yr, yi = hsp[:kp], hsp[kp:], ysp[:kp], ysp[kp:]
        z = jnp.concatenate([yr * hr - yi * hi, yr * hi + yi * hr], axis=0)
        cv = jnp.dot(finv_ref[...], z, preferred_element_type=F32, precision=hp)
        y = conv(g_ref, w, b) * (cv + y * bias_ref[o])
    o_ref[0] = y


def hyena_context(pc, c0, filt, p):
    bsz, n = pc.shape[:2]
    w = p['hy_bias'].shape[-1]
    ns = w // LANE
    kp = -(-(n + 1) // 8) * 8
    kk = jnp.arange(kp, dtype=jnp.int32)[:, None]
    mm = jnp.arange(2 * n, dtype=jnp.int32)[None, :]
    ang = (math.pi / n) * ((kk * mm) % (2 * n)).astype(F32)
    valid = (kk <= n).astype(F32)
    ff = jnp.concatenate([jnp.cos(ang) * valid, -jnp.sin(ang) * valid], axis=0)
    wk = jnp.where((kk == 0) | (kk == n), 1.0, 2.0) * valid / (2 * n)
    finv = jnp.concatenate([(jnp.cos(ang) * wk).T[:n], (-jnp.sin(ang) * wk).T[:n]], axis=1)
    b0 = c0 // LANE
    zs = lambda c: pl.BlockSpec((1, n, LANE), lambda j, b: (b, 0, b0 + c + j))
    cw = lambda c: pl.BlockSpec((3, LANE), lambda j, b: (0, c + j))
    cb = lambda c: pl.BlockSpec((1, LANE), lambda j, b: (0, c + j))
    conv_w, conv_b = p['hy_conv_w'], p['hy_conv_b'].reshape(1, -1)
    return pl.pallas_call(
        _hyena_ctx_kernel,
        out_shape=jax.ShapeDtypeStruct((bsz, n, w), F32),
        grid=(ns, bsz),
        in_specs=[zs(0), zs(ns), zs(2 * ns), cw(0), cb(0), cw(ns), cb(ns), cw(2 * ns), cb(2 * ns),
                  pl.BlockSpec((2, 1, LANE), lambda j, b: (0, 0, j)),
                  pl.BlockSpec((2, 2 * n, LANE), lambda j, b: (0, 0, j)),
                  pl.BlockSpec(ff.shape, lambda j, b: (0, 0)), pl.BlockSpec(finv.shape, lambda j, b: (0, 0))],
        out_specs=pl.BlockSpec((1, n, LANE), lambda j, b: (b, 0, j)),
        compiler_params=_cparams("parallel", "parallel"),
        name="hyena_context",
    )(pc, pc, pc, conv_w, conv_b, conv_w, conv_b, conv_w, conv_b, p['hy_bias'].reshape(2, 1, w), filt, ff, finv)


def _to_col_major(t):
    bsz, n = t.shape[:2]
    rows = n // GRID_W
    return t.reshape(bsz, rows, GRID_W, *t.shape[2:]).swapaxes(1, 2).reshape(bsz, n, *t.shape[2:])


def _from_col_major(t):
    bsz, n = t.shape[:2]
    rows = n // GRID_W
    return t.reshape(bsz, GRID_W, rows, *t.shape[2:]).swapaxes(1, 2).reshape(bsz, n, *t.shape[2:])


def _filter_positions(n):
    r = jnp.arange(2 * n, dtype=jnp.int32)
    return jnp.where(r < n, r, 2 * n - r).astype(F32)


def _filter_feats(n):
    pos = _filter_positions(n)
    t = pos / max(n - 1, 1)
    freqs = jnp.linspace(1e-4, HY_POS_FREQS - 1, HY_POS_FREQS, dtype=F32)
    ang = (2.0 * math.pi / n) * pos[:, None] * freqs[None]
    feats = jnp.concatenate([t[:, None], jnp.cos(ang), -jnp.sin(ang)], axis=-1)
    return jnp.pad(feats, ((0, 0), (0, LANE - feats.shape[1])))


def _filter_hidden_kernel(f_ref, w1_ref, b1_ref, s1_ref, w2_ref, b2_ref, s2_ref, o_ref):
    hp = lax.Precision.HIGHEST
    h = jnp.sin(s1_ref[...] * (jnp.dot(f_ref[...], w1_ref[...], preferred_element_type=F32, precision=hp)
                               + b1_ref[...]))
    o_ref[...] = jnp.sin(s2_ref[...] * (jnp.dot(h, w2_ref[...], preferred_element_type=F32, precision=hp)
                                        + b2_ref[...]))


def _filter_out_kernel(h_ref, wf_ref, wb_ref, bf_ref, bb_ref, rate_ref, o_ref, *, n, chunk):
    nchunk = 2 * n // chunk
    inv_span = 1.0 / max(n - 1, 1)

    def emit(c, acc):
        r0 = pl.multiple_of(c * chunk, chunk)
        past = c < nchunk // 2
        w = jnp.where(past, wf_ref[...], wb_ref[...])
        bias = jnp.where(past, bf_ref[...], bb_ref[...])
        val = jnp.dot(h_ref[pl.ds(r0, chunk), :], w, preferred_element_type=F32,
                      precision=lax.Precision.HIGHEST) + bias
        r = r0 + lax.broadcasted_iota(jnp.int32, (chunk, LANE), 0)
        t = jnp.where(r < n, r, 2 * n - r).astype(F32) * inv_span
        val = jnp.where(r == n, 0.0, val * jnp.exp(-t * rate_ref[...]))
        o_ref[0, pl.ds(r0, chunk), :] = val
        return acc + jnp.sum(jnp.abs(val), axis=0, keepdims=True)

    total = lax.fori_loop(0, nchunk, emit, jnp.zeros((1, LANE), F32))
    scale = 1.0 / (total + EPS)

    def rescale(c, carry):
        r0 = pl.multiple_of(c * chunk, chunk)
        o_ref[0, pl.ds(r0, chunk), :] = o_ref[0, pl.ds(r0, chunk), :] * scale
        return carry

    lax.fori_loop(0, nchunk, rescale, 0)


def hyena_filters(n, p):
    order, width = p['hy_bias'].shape
    ns = width // LANE
    hid = p['hy_f_w2'].shape[0]
    feats = _filter_feats(n)
    padc = lambda a: jnp.pad(a.reshape(1, -1), ((0, 0), (0, LANE - a.shape[-1])))
    w1 = jnp.pad(p['hy_f_w1'], ((0, LANE - p['hy_f_w1'].shape[0]), (0, LANE - hid)))
    w2 = jnp.pad(p['hy_f_w2'], ((0, LANE - hid), (0, LANE - hid)))
    w3 = jnp.pad(p['hy_f_w3'], ((0, LANE - hid), (0, 0)))
    tr = _pick_tile(2 * n, 2048, 8)
    full = lambda i: (0, 0)
    hidden = pl.pallas_call(
        _filter_hidden_kernel,
        out_shape=jax.ShapeDtypeStruct((2 * n, LANE), F32),
        grid=(2 * n // tr,),
        in_specs=[pl.BlockSpec((tr, LANE), lambda i: (i, 0)),
                  pl.BlockSpec((LANE, LANE), full), pl.BlockSpec((1, LANE), full), pl.BlockSpec((1, LANE), full),
                  pl.BlockSpec((LANE, LANE), full), pl.BlockSpec((1, LANE), full), pl.BlockSpec((1, LANE), full)],
        out_specs=pl.BlockSpec((tr, LANE), lambda i: (i, 0)),
        compiler_params=_cparams("parallel"),
        name="hyena_filter_hidden",
    )(feats, w1, padc(p['hy_f_b1']), padc(p['hy_f_freq1']), w2, padc(p['hy_f_b2']), padc(p['hy_f_freq2']))
    rates = jnp.abs(jnp.linspace(math.log(HY_DECAY_TARGET) / HY_LONG_DECAY_PCT,
                                 math.log(HY_DECAY_TARGET) / HY_SHORT_DECAY_PCT, width, dtype=F32)).reshape(1, width)
    b3 = p['hy_f_b3'].reshape(1, -1)
    return pl.pallas_call(
        functools.partial(_filter_out_kernel, n=n, chunk=min(1024, n)),
        out_shape=jax.ShapeDtypeStruct((order, 2 * n, width), F32),
        grid=(order, ns),
        in_specs=[_one((2 * n, LANE), lambda o, j: (0, 0)),
                  pl.BlockSpec((LANE, LANE), lambda o, j: (0, 2 * o * ns + j)),
                  pl.BlockSpec((LANE, LANE), lambda o, j: (0, (2 * o + 1) * ns + j)),
                  pl.BlockSpec((1, LANE), lambda o, j: (0, 2 * o * ns + j)),
                  pl.BlockSpec((1, LANE), lambda o, j: (0, (2 * o + 1) * ns + j)),
                  pl.BlockSpec((1, LANE), lambda o, j: (0, j))],
        out_specs=pl.BlockSpec((1, 2 * n, LANE), lambda o, j: (o, 0, j)),
        compiler_params=_cparams("parallel", "parallel"),
        name="hyena_filter_out",
    )(hidden, w3, w3, b3, b3, rates)


def kernel(x, c, ctx, c_ctx, w_mod, b_mod, norm1_g, norm2_g, w_in, s5_a_re, s5_a_im, s5_log_step, s5_b_re, s5_b_im, s5_c_re, s5_c_im, s5_d, s5_glu_w, s5_glu_b, hy_conv_w, hy_conv_b, hy_f_w1, hy_f_b1, hy_f_freq1, hy_f_w2, hy_f_b2, hy_f_freq2, hy_f_w3, hy_f_b3, hy_bias, gla_wg, gla_bg, gla_norm_g, w_branch, w_out, ff_w_up, ff_conv_w, ff_conv_b, ff_w_down, final_norm_g):
    bsz, seq, d = x.shape
    nctx = ctx.shape[1]
    depth = w_mod.shape[0]
    q4 = d // 4
    rank2 = 2 * gla_wg.shape[2]
    gpad = max(LANE, q4 // 2)
    assert bsz + 1 <= 8

    c_gk = q4
    c_gv = 2 * q4
    c_gg = 4 * q4
    c_gq = c_gg + rank2
    c_gr = c_gq + q4
    c_hy = c_gr + 2 * q4
    c_mg = c_hy + 3 * q4

    rows = jnp.zeros((8, d), F32).at[:bsz].set(c).at[bsz].set(c_ctx)
    mod = modulation(rows, w_mod, b_mod)

    xs = x.reshape(bsz * seq, d)
    cs = ctx.reshape(bsz * nctx, d)
    for l in range(depth):
        ctx_out = l < depth - 1
        sh1, s1, g1, sh2, s2, g2 = [mod[l, :, i * d:(i + 1) * d][:, None, :] for i in range(6)]
        gs1 = norm1_g[l] * (1.0 + s1)
        gs2 = norm2_g[l] * (1.0 + s2)
        bx = slice(0, bsz)
        bc = slice(bsz, bsz + 1)

        wl = w_in[l]
        w_pack = jnp.concatenate([
            wl[:, 0:c_gk], wl[:, c_hy:c_mg], wl[:, c_gr:c_hy], wl[:, c_gk:c_gv], wl[:, c_gq:c_gr],
            wl[:, c_gv:c_gg], wl[:, c_gg:c_gq],
            jnp.zeros((d, gpad - rank2), F32)], axis=1).astype(BF16)
        w_gate = wl[:, c_mg:].reshape(d, 3, d).swapaxes(0, 1).astype(BF16)
        wb = w_branch[l].astype(BF16)
        w_o = w_out[l].astype(BF16)
        w_up = ff_w_up[l].astype(BF16)
        w_dn = ff_w_down[l].astype(BF16)

        p = dict(s5_a_re=s5_a_re[l], s5_a_im=s5_a_im[l], s5_log_step=s5_log_step[l],
                 s5_b_re=s5_b_re[l], s5_b_im=s5_b_im[l], s5_c_re=s5_c_re[l], s5_c_im=s5_c_im[l],
                 s5_d=s5_d[l], s5_glu_w=s5_glu_w[l], s5_glu_b=s5_glu_b[l],
                 hy_conv_w=hy_conv_w[l], hy_conv_b=hy_conv_b[l], hy_f_w1=hy_f_w1[l], hy_f_b1=hy_f_b1[l],
                 hy_f_freq1=hy_f_freq1[l], hy_f_w2=hy_f_w2[l], hy_f_b2=hy_f_b2[l],
                 hy_f_freq2=hy_f_freq2[l], hy_f_w3=hy_f_w3[l], hy_f_b3=hy_f_b3[l], hy_bias=hy_bias[l],
                 gla_wg=gla_wg[l], gla_bg=gla_bg[l], gla_norm_g=gla_norm_g[l])

        px = norm_matmul(xs, gs1[bx], sh1[bx], w_pack).reshape(bsz, seq, -1)
        pc = norm_matmul(cs, gs1[bc], sh1[bc], w_pack).reshape(bsz, nctx, -1)

        ya_c, ya_x = s5_mixer(pc, px, q4, p)
        oc_c, oc_x = gla_mixer(pc, px, 6 * q4, q4, gpad, p)
        yb_x = hyena_latent(px, q4, hyena_filters(seq, p), p)

        flat = lambda t: t.reshape(-1, t.shape[-1])
        mx = merge(xs, gs1[bx], sh1[bx], flat(ya_x), flat(yb_x), flat(oc_x), flat(px), 2, w_gate, wb)
        xs = matmul_residual(mx, w_o, xs, g1[bx])
        xs = ffn(xs, bsz, gs2[bx], sh2[bx], w_up, ff_conv_w[l], ff_conv_b[l], w_dn, g2[bx])

        if ctx_out:
            yb_c = hyena_context(pc, q4, hyena_filters(nctx, p), p)
            mc = merge(cs, gs1[bc], sh1[bc], flat(ya_c), flat(yb_c), flat(oc_c), flat(pc), 2, w_gate, wb)
            cs = matmul_residual(mc, w_o, cs, g1[bc])
            cs = ffn(cs, bsz, gs2[bc], sh2[bc], w_up, ff_conv_w[l], ff_conv_b[l], w_dn, g2[bc])

    return final_rmsnorm(xs, final_norm_g).reshape(bsz, seq, d)
```

```python
import functools
import math

import jax
import jax.numpy as jnp
from jax import lax
from jax.experimental import pallas as pl
from jax.experimental.pallas import tpu as pltpu

F32 = jnp.float32
BF16 = jnp.bfloat16
EPS = 1e-6
GRID_W = 64
S5_GROUP = 16
GLA_CHUNK = 64
GLA_GATE_TEMP = 16.0
HY_POS_FREQS = 16
HY_DECAY_TARGET = 1e-2
HY_SHORT_DECAY_PCT = 0.3
HY_LONG_DECAY_PCT = 1.5
LANE = 128
VMEM_LIMIT = 56 * 1024 * 1024
BIG_VMEM_LIMIT = 60 * 1024 * 1024
FFN_TF = 512


def _pick_tile(n, cap, mult=LANE):
    best = None
    for t in range(mult, min(n, cap) + 1, mult):
        if n % t == 0:
            best = t
    assert best is not None, (n, cap, mult)
    return best


def _col_blocks(w, tn):
    nj = w.shape[-1] // tn
    return jnp.moveaxis(w.reshape(*w.shape[:-1], nj, tn), -2, 0)


def _cparams(*sem):
    return pltpu.CompilerParams(dimension_semantics=sem, vmem_limit_bytes=VMEM_LIMIT)


def _mod_kernel(r_ref, w_ref, b_ref, o_ref):
    r = r_ref[...]
    s = r * jax.nn.sigmoid(r)
    o_ref[0] = jnp.dot(s, w_ref[0], preferred_element_type=F32,
                       precision=lax.Precision.HIGHEST) + b_ref[0]


def modulation(rows, w_mod, b_mod):
    depth, d, n = w_mod.shape
    tn = _pick_tile(n, 1024)
    return pl.pallas_call(
        _mod_kernel,
        out_shape=jax.ShapeDtypeStruct((depth, 8, n), F32),
        grid=(depth, n // tn),
        in_specs=[pl.BlockSpec((8, d), lambda l, j: (0, 0)),
                  pl.BlockSpec((1, d, tn), lambda l, j: (l, 0, j)),
                  pl.BlockSpec((1, 1, tn), lambda l, j: (l, 0, j))],
        out_specs=pl.BlockSpec((1, 8, tn), lambda l, j: (l, 0, j)),
        compiler_params=_cparams("parallel", "parallel"),
        name="modulation",
    )(rows, w_mod, b_mod.reshape(depth, 1, n))


def _norm_mm_kernel(x_ref, gs_ref, sh_ref, w_ref, o_ref, h_ref):
    @pl.when(pl.program_id(1) == 0)
    def _():
        x = x_ref[...]
        ms = jnp.mean(x * x, axis=-1, keepdims=True)
        h_ref[...] = (x * lax.rsqrt(ms + EPS) * gs_ref[0] + sh_ref[0]).astype(BF16)

    o_ref[...] = jnp.dot(h_ref[...], w_ref[0], preferred_element_type=F32).astype(o_ref.dtype)


def norm_matmul(x, gs, sh, w, out_dtype=F32):
    m, d = x.shape
    n = w.shape[1]
    nb = gs.shape[0]
    tm = _pick_tile(m // nb, 1024, 8)
    tn = _pick_tile(n, 2048)
    tpb = (m // nb) // tm
    return pl.pallas_call(
        _norm_mm_kernel,
        out_shape=jax.ShapeDtypeStruct((m, n), out_dtype),
        grid=(m // tm, n // tn),
        in_specs=[pl.BlockSpec((tm, d), lambda i, j: (i, 0)),
                  pl.BlockSpec((1, 1, d), lambda i, j: (i // tpb, 0, 0)),
                  pl.BlockSpec((1, 1, d), lambda i, j: (i // tpb, 0, 0)),
                  pl.BlockSpec((1, d, tn), lambda i, j: (j, 0, 0))],
        out_specs=pl.BlockSpec((tm, tn), lambda i, j: (i, j)),
        scratch_shapes=[pltpu.VMEM((tm, d), BF16)],
        compiler_params=_cparams("parallel", "arbitrary"),
        name="norm_matmul",
    )(x, gs, sh, _col_blocks(w, tn))


def _mm_res_kernel(a_ref, w_ref, r_ref, g_ref, o_ref):
    acc = jnp.dot(a_ref[...], w_ref[0], preferred_element_type=F32)
    o_ref[...] = r_ref[...] + g_ref[0] * acc


def matmul_residual(a, w, res, g):
    m, k = a.shape
    n = w.shape[1]
    nb = g.shape[0]
    tm = _pick_tile(m // nb, 1024, 8)
    tn = _pick_tile(n, 1024)
    tpb = (m // nb) // tm
    return pl.pallas_call(
        _mm_res_kernel,
        out_shape=jax.ShapeDtypeStruct((m, n), F32),
        grid=(m // tm, n // tn),
        in_specs=[pl.BlockSpec((tm, k), lambda i, j: (i, 0)),
                  pl.BlockSpec((1, k, tn), lambda i, j: (j, 0, 0)),
                  pl.BlockSpec((tm, tn), lambda i, j: (i, j)),
                  pl.BlockSpec((1, 1, tn), lambda i, j: (i // tpb, 0, j))],
        out_specs=pl.BlockSpec((tm, tn), lambda i, j: (i, j)),
        compiler_params=_cparams("parallel", "arbitrary"),
        name="matmul_residual",
    )(a, _col_blocks(w, tn), res, g)


def _merge_kernel(x_ref, gs_ref, sh_ref, ya_ref, yb_ref, oc_ref, r_ref, wg_ref, wb_ref, o_ref, h_ref, y_ref):
    wa, wb = ya_ref.shape[1], yb_ref.shape[1]

    @pl.when(pl.program_id(1) == 0)
    def _():
        x = x_ref[...]
        ms = jnp.mean(x * x, axis=-1, keepdims=True)
        h_ref[...] = (x * lax.rsqrt(ms + EPS) * gs_ref[0] + sh_ref[0]).astype(BF16)
        r = r_ref[...]
        y_ref[:, 0:wa] = ya_ref[...].astype(BF16)
        y_ref[:, wa:wa + wb] = yb_ref[...].astype(BF16)
        y_ref[:, wa + wb:] = (oc_ref[...] * (r * jax.nn.sigmoid(r))).astype(BF16)

    h = h_ref[...]
    m = None
    for i, (lo, hi) in enumerate(((0, wa), (wa, wa + wb), (wa + wb, y_ref.shape[1]))):
        gate = jax.nn.sigmoid(jnp.dot(h, wg_ref[0, i], preferred_element_type=F32))
        br = jnp.dot(y_ref[:, lo:hi], wb_ref[0, lo:hi, :], preferred_element_type=F32)
        m = gate * br if m is None else m + gate * br
    o_ref[...] = m.astype(o_ref.dtype)


def merge(x, gs, sh, ya, yb, oc, pr, rcol, wg, wb):
    m, d = x.shape
    nb = gs.shape[0]
    tm = _pick_tile(m // nb, 512, 8)
    tn = _pick_tile(d, 512)
    tpb = (m // nb) // tm
    row = lambda i, j: (i, 0)
    wy = wb.shape[0]
    return pl.pallas_call(
        _merge_kernel,
        out_shape=jax.ShapeDtypeStruct((m, d), BF16),
        grid=(m // tm, d // tn),
        in_specs=[pl.BlockSpec((tm, d), row),
                  pl.BlockSpec((1, 1, d), lambda i, j: (i // tpb, 0, 0)),
                  pl.BlockSpec((1, 1, d), lambda i, j: (i // tpb, 0, 0)),
                  pl.BlockSpec((tm, ya.shape[1]), row),
                  pl.BlockSpec((tm, yb.shape[1]), row),
                  pl.BlockSpec((tm, oc.shape[1]), row),
                  pl.BlockSpec((tm, oc.shape[1]), lambda i, j: (i, rcol)),
                  pl.BlockSpec((1, 3, d, tn), lambda i, j: (j, 0, 0, 0)),
                  pl.BlockSpec((1, wy, tn), lambda i, j: (j, 0, 0))],
        out_specs=pl.BlockSpec((tm, tn), lambda i, j: (i, j)),
        scratch_shapes=[pltpu.VMEM((tm, d), BF16), pltpu.VMEM((tm, wy), BF16)],
        compiler_params=_cparams("parallel", "arbitrary"),
        name="merge",
    )(x, gs, sh, ya, yb, oc, pr, _col_blocks(wg, tn), _col_blocks(wb, tn))


def _dwconv3_tile(a, prev8, next8, first, last, w_ref, cb_ref):
    tr = a.shape[0]
    row = lax.broadcasted_iota(jnp.int32, a.shape, 0)
    before = jnp.where(first, 0.0, prev8[7:8, :])
    after = jnp.where(last, 0.0, next8[0:1, :])
    prev = jnp.where(row == 0, before, pltpu.roll(a, 1, 0))
    nxt = jnp.where(row == tr - 1, after, pltpu.roll(a, tr - 1, 0))
    return prev * w_ref[0:1, :] + a * w_ref[1:2, :] + nxt * w_ref[2:3, :] + cb_ref[...]


def _ffn_kernel(x_ref, xp_ref, xn_ref, gs_ref, sh_ref, wa_ref, wb_ref, cw_ref, cb_ref, wd_ref, g_ref,
                o_ref, h_ref, hh_ref, *, tpb):
    i, f = pl.program_id(0), pl.program_id(1)

    def normed(x):
        ms = jnp.mean(x * x, axis=-1, keepdims=True)
        return (x * lax.rsqrt(ms + EPS) * gs_ref[0] + sh_ref[0]).astype(BF16)

    @pl.when(f == 0)
    def _():
        h_ref[...] = normed(x_ref[...])
        hh_ref[0:8, :] = normed(xp_ref[...])
        hh_ref[8:16, :] = normed(xn_ref[...])
        o_ref[...] = jnp.zeros_like(o_ref)

    wa = wa_ref[0]
    a = jnp.dot(h_ref[...], wa, preferred_element_type=F32)
    ah = jnp.dot(hh_ref[...], wa, preferred_element_type=F32)
    b = jnp.dot(h_ref[...], wb_ref[0], preferred_element_type=F32)
    pos = i % tpb
    conv = _dwconv3_tile(a, ah[0:8], ah[8:16], pos == 0, pos == tpb - 1, cw_ref, cb_ref)
    act = (conv * jax.nn.sigmoid(conv) * b).astype(BF16)
    half = o_ref.shape[1] // 2
    for lo in (0, half):
        o_ref[:, lo:lo + half] += jnp.dot(act, wd_ref[:, lo:lo + half], preferred_element_type=F32)

    @pl.when(f == pl.num_programs(1) - 1)
    def _():
        o_ref[...] = x_ref[...] + g_ref[0] * o_ref[...]


def ffn(x, nseq, gs, sh, w_up, conv_w, conv_b, w_dn, g):
    m, d = x.shape
    fh = w_dn.shape[0]
    nb = gs.shape[0]
    tm = _pick_tile(m // nseq, 1024, 8)
    tf = _pick_tile(fh, FFN_TF)
    tpb = (m // nseq) // tm
    tpm = (m // nb) // tm
    nf = fh // tf
    w_up = w_up.reshape(d, 2 * nf, tf).swapaxes(0, 1)
    r8 = tm // 8
    last8 = m // 8 - 1
    mod = lambda i, f: (i // tpm, 0, 0)
    return pl.pallas_call(
        functools.partial(_ffn_kernel, tpb=tpb),
        out_shape=jax.ShapeDtypeStruct((m, d), F32),
        grid=(m // tm, nf),
        in_specs=[pl.BlockSpec((tm, d), lambda i, f: (i, 0), pipeline_mode=pl.Buffered(1)),
                  pl.BlockSpec((8, d), lambda i, f: (jnp.maximum(i * r8 - 1, 0), 0)),
                  pl.BlockSpec((8, d), lambda i, f: (jnp.minimum((i + 1) * r8, last8), 0)),
                  pl.BlockSpec((1, 1, d), mod), pl.BlockSpec((1, 1, d), mod),
                  pl.BlockSpec((1, d, tf), lambda i, f: (f, 0, 0)),
                  pl.BlockSpec((1, d, tf), lambda i, f: (f + nf, 0, 0)),
                  pl.BlockSpec((3, tf), lambda i, f: (0, f)),
                  pl.BlockSpec((1, tf), lambda i, f: (0, f)),
                  pl.BlockSpec((tf, d), lambda i, f: (f, 0)),
                  pl.BlockSpec((1, 1, d), mod)],
        out_specs=pl.BlockSpec((tm, d), lambda i, f: (i, 0)),
        scratch_shapes=[pltpu.VMEM((tm, d), BF16), pltpu.VMEM((16, d), BF16)],
        compiler_params=pltpu.CompilerParams(dimension_semantics=("parallel", "arbitrary"),
                                             vmem_limit_bytes=BIG_VMEM_LIMIT),
        name="ffn",
    )(x, x, x, gs, sh, w_up, w_up, conv_w, conv_b.reshape(1, fh), w_dn, g)


def _rmsnorm_kernel(x_ref, g_ref, o_ref):
    x = x_ref[...]
    ms = jnp.mean(x * x, axis=-1, keepdims=True)
    o_ref[...] = x * lax.rsqrt(ms + EPS) * g_ref[...]


def final_rmsnorm(x, g):
    m, d = x.shape
    tm = _pick_tile(m, 1024, 8)
    return pl.pallas_call(
        _rmsnorm_kernel,
        out_shape=jax.ShapeDtypeStruct((m, d), F32),
        grid=(m // tm,),
        in_specs=[pl.BlockSpec((tm, d), lambda i: (i, 0)), pl.BlockSpec((1, d), lambda i: (0, 0))],
        out_specs=pl.BlockSpec((tm, d), lambda i: (i, 0)),
        compiler_params=_cparams("parallel"),
        name="final_rmsnorm",
    )(x, g.reshape(1, d))


S5_CHUNK = 256


def _const_spec(shape):
    zeros = (0,) * len(shape)
    return pl.BlockSpec(shape, lambda b, k: zeros, pipeline_mode=pl.Buffered(1))


def _s5_kernel(*refs, ncc, reverse, finish):
    if finish:
        (uc_ref, ux_ref, bblk_ref, cblk_ref, enr_ref, eni_ref, epr_ref, epi_ref, ac_ref,
         pc_ref, px_ref, d_ref, gw_ref, gb_ref, yc_ref, yx_ref, h_ref) = refs
    else:
        (uc_ref, ux_ref, bblk_ref, cblk_ref, enr_ref, eni_ref, epr_ref, epi_ref, ac_ref,
         yc_ref, yx_ref, h_ref) = refs
    k = pl.program_id(1)
    t = uc_ref.shape[1]
    gn = enr_ref.shape[1]

    @pl.when(k == 0)
    def _():
        h_ref[...] = jnp.zeros_like(h_ref)

    is_ctx = k < ncc
    u = jnp.where(is_ctx, uc_ref[0], ux_ref[0])
    row = lax.broadcasted_iota(jnp.int32, (t, t), 0)
    col = lax.broadcasted_iota(jnp.int32, (t, t), 1)
    tri = jnp.where((col >= row) if reverse else (col <= row), 1.0, 0.0).astype(BF16)
    last = 0 if reverse else t - 1
    nblk, wb, sb2 = bblk_ref.shape
    sb = sb2 // 2
    ys = []
    for j in range(nblk):
        sc = slice(j * sb, (j + 1) * sb)
        bu = jnp.dot(u[:, j * wb:(j + 1) * wb].astype(BF16), bblk_ref[j], preferred_element_type=F32)
        br, bi = bu[:, :sb], bu[:, sb:]
        enr, eni = enr_ref[:, sc], eni_ref[:, sc]
        z = jnp.concatenate([br * enr - bi * eni, br * eni + bi * enr], axis=1).astype(BF16)
        cs = jnp.dot(tri, z, preferred_element_type=F32)
        hr, hi = h_ref[0:1, sc], h_ref[1:2, sc]
        acr, aci = ac_ref[0:1, sc], ac_ref[1:2, sc]
        sr = cs[:, :sb] + (hr * acr - hi * aci)
        si = cs[:, sb:] + (hr * aci + hi * acr)
        epr, epi = epr_ref[:, sc], epi_ref[:, sc]
        xr = sr * epr - si * epi
        xi = sr * epi + si * epr
        h_ref[0:1, sc] = xr[last:last + 1, :]
        h_ref[1:2, sc] = xi[last:last + 1, :]
        xs = jnp.concatenate([xr, xi], axis=1).astype(BF16)
        ys.append(jnp.dot(xs, cblk_ref[j], preferred_element_type=F32))
    y = ys[0] if nblk == 1 else jnp.concatenate(ys, axis=1)
    if finish:
        y = y + jnp.where(is_ctx, pc_ref[0], px_ref[0]) + d_ref[...] * u
        y = jax.nn.gelu(y)
        gate = jnp.dot(y.astype(BF16), gw_ref[...], preferred_element_type=F32) + gb_ref[...]
        y = y * jax.nn.sigmoid(gate)

    @pl.when(is_ctx)
    def _():
        yc_ref[0] = y.astype(yc_ref.dtype)

    @pl.when(jnp.logical_not(is_ctx))
    def _():
        yx_ref[0] = y.astype(yx_ref.dtype)


def _s5_tables(a_re, a_im, log_step, b_re, b_im, c_re, c_im, t, reverse):
    g, n, h = b_re.shape
    gpb = min(g, max(1, LANE // h))
    nblk = g // gpb
    eye = jnp.eye(gpb, dtype=F32)
    dt = jnp.exp(log_step)[:, None]
    ldr, ldi = a_re * dt, a_im * dt
    mag = jnp.exp(ldr)
    abr, abi = mag * jnp.cos(ldi), mag * jnp.sin(ldi)
    den = a_re * a_re + a_im * a_im
    nr, ni = abr - 1.0, abi
    fr = (nr * a_re + ni * a_im) / den
    fi = (ni * a_re - nr * a_im) / den
    bbr = fr[..., None] * b_re - fi[..., None] * b_im
    bbi = fr[..., None] * b_im + fi[..., None] * b_re
    blk_b = lambda m: jnp.einsum('jgnh,gk->jghkn', m.reshape(nblk, gpb, n, h), eye).reshape(nblk, gpb * h, gpb * n)
    bblk = jnp.concatenate([blk_b(bbr), blk_b(bbi)], axis=2).astype(BF16)
    blk_c = lambda m: jnp.einsum('jghn,gk->jgnkh', m.reshape(nblk, gpb, h, n), eye).reshape(nblk, gpb * n, gpb * h)
    cblk = jnp.concatenate([blk_c(c_re), -blk_c(c_im)], axis=1).astype(BF16)
    centre = float(t // 2)
    pos = jnp.arange(t, dtype=F32)[:, None]
    steps = ((t - pos) if reverse else (pos + 1.0)) - centre
    lr, li = ldr.reshape(1, g * n), ldi.reshape(1, g * n)
    er, ei = steps * lr, steps * li
    epr, epi = jnp.exp(er) * jnp.cos(ei), jnp.exp(er) * jnp.sin(ei)
    enr, eni = jnp.exp(-er) * jnp.cos(ei), -jnp.exp(-er) * jnp.sin(ei)
    ac = jnp.concatenate([jnp.exp(centre * lr) * jnp.cos(centre * li),
                          jnp.exp(centre * lr) * jnp.sin(centre * li)], axis=0)
    return bblk, cblk, (enr, eni, epr, epi, ac)


def _s5_pass(pc, px, col, width, tabs, reverse, fin=None, out_dtype=F32):
    bsz, nctx = pc.shape[:2]
    seq = px.shape[1]
    t = S5_CHUNK
    ncc, ncx = nctx // t, seq // t
    bblk, cblk, (enr, eni, epr, epi, ac) = tabs
    gn = enr.shape[1]
    if reverse:
        cidx = lambda k: jnp.maximum(ncc - 1 - k, 0)
        xidx = lambda k: jnp.minimum(ncx - 1 - (k - ncc), ncx - 1)
    else:
        cidx = lambda k: jnp.minimum(k, ncc - 1)
        xidx = lambda k: jnp.maximum(k - ncc, 0)
    in_specs = [pl.BlockSpec((1, t, width), lambda b, k: (b, cidx(k), col)),
                pl.BlockSpec((1, t, width), lambda b, k: (b, xidx(k), col)),
                _const_spec(bblk.shape), _const_spec(cblk.shape),
                _const_spec(enr.shape), _const_spec(eni.shape), _const_spec(epr.shape), _const_spec(epi.shape),
                _const_spec(ac.shape)]
    args = [pc, px, bblk, cblk, enr, eni, epr, epi, ac]
    if fin is not None:
        prev_c, prev_x, dvec, gw, gb = fin
        in_specs += [pl.BlockSpec((1, t, width), lambda b, k: (b, cidx(k), 0)),
                     pl.BlockSpec((1, t, width), lambda b, k: (b, xidx(k), 0)),
                     _const_spec((1, width)), _const_spec(gw.shape), _const_spec((1, width))]
        args += [prev_c, prev_x, dvec.reshape(1, width), gw, gb.reshape(1, width)]
    return pl.pallas_call(
        functools.partial(_s5_kernel, ncc=ncc, reverse=reverse, finish=fin is not None),
        out_shape=(jax.ShapeDtypeStruct((bsz, nctx, width), out_dtype),
                   jax.ShapeDtypeStruct((bsz, seq, width), out_dtype)),
        grid=(bsz, ncc + ncx),
        in_specs=in_specs,
        out_specs=(pl.BlockSpec((1, t, width), lambda b, k: (b, cidx(k), 0)),
                   pl.BlockSpec((1, t, width), lambda b, k: (b, xidx(k), 0))),
        scratch_shapes=[pltpu.VMEM((2, gn), F32)],
        compiler_params=_cparams("parallel", "arbitrary"),
        name="s5_rev" if reverse else "s5_fwd",
    )(*args)


def s5_mixer(pc, px, width, p):
    tabs = [_s5_tables(p['s5_a_re'][d], p['s5_a_im'][d], p['s5_log_step'][d], p['s5_b_re'][d], p['s5_b_im'][d],
                       p['s5_c_re'][d], p['s5_c_im'][d], S5_CHUNK, d == 1) for d in range(2)]
    bc, bx = _s5_pass(pc, px, 0, width, tabs[1], reverse=True)
    return _s5_pass(pc, px, 0, width, tabs[0], reverse=False,
                    fin=(bc, bx, p['s5_d'], p['s5_glu_w'].astype(BF16), p['s5_glu_b']))


GLA_STEP = 256


def _gla_kernel(*refs, nh, dk, dv, ncs, reverse, finish, scale):
    if finish:
        (kc_ref, qc_ref, vc_ref, lc_ref, kx_ref, qx_ref, vx_ref, lx_ref, wg_ref, bg_ref, pc_ref, px_ref, ng_ref,
         oc_ref, ox_ref, s_ref, o_scr) = refs
    else:
        (kc_ref, qc_ref, vc_ref, lc_ref, kx_ref, qx_ref, vx_ref, lx_ref, wg_ref, bg_ref,
         oc_ref, ox_ref, s_ref, o_scr) = refs
    ts = kc_ref.shape[1]
    nchunk = ts // GLA_CHUNK
    is_ctx = pl.program_id(1) < ncs
    pick = lambda c_ref, x_ref: jnp.where(is_ctx, c_ref[0], x_ref[0])

    @pl.when(pl.program_id(1) == 0)
    def _():
        s_ref[...] = jnp.zeros_like(s_ref)

    hp = lax.Precision.HIGHEST
    rank2 = wg_ref.shape[0]
    pre = jnp.dot(pick(lc_ref, lx_ref)[:, :rank2], wg_ref[...], preferred_element_type=F32,
                  precision=hp) + bg_ref[...]
    g = jax.nn.log_sigmoid(pre) * (1.0 / GLA_GATE_TEMP)
    ci = lax.broadcasted_iota(jnp.int32, (GLA_CHUNK, GLA_CHUNK), 0)
    cj = lax.broadcasted_iota(jnp.int32, (GLA_CHUNK, GLA_CHUNK), 1)
    keep = (cj >= ci) if reverse else (cj <= ci)
    cum = jnp.where(keep, 1.0, 0.0)
    b = jnp.concatenate([jnp.dot(cum, g[c * GLA_CHUNK:(c + 1) * GLA_CHUNK], preferred_element_type=F32, precision=hp)
                         for c in range(nchunk)], axis=0)
    eb = jnp.exp(b)
    enb = jnp.exp(-b)
    q_in = pick(qc_ref, qx_ref) * scale * eb
    k_out = pick(kc_ref, kx_ref) * enb
    v = pick(vc_ref, vx_ref)
    if finish:
        prev = pick(pc_ref, px_ref)
    nt = (((1,), (1,)), ((), ()))
    tn = (((0,), (0,)), ((), ()))
    for c in (range(nchunk - 1, -1, -1) if reverse else range(nchunk)):
        r0 = c * GLA_CHUNK
        rows = slice(r0, r0 + GLA_CHUNK)
        end = r0 if reverse else r0 + GLA_CHUNK - 1
        etot = jnp.exp(b[end:end + 1, :])
        k_kv = k_out[rows] * etot
        for h in range(nh):
            kc = slice(h * dk, (h + 1) * dk)
            vc = slice(h * dv, (h + 1) * dv)
            qh = q_in[rows, kc].astype(BF16)
            kh = k_out[rows, kc].astype(BF16)
            kkv = k_kv[:, kc].astype(BF16)
            vh = v[rows, vc].astype(BF16)
            st = s_ref[h]
            sc = lax.dot_general(qh, kh, nt, preferred_element_type=F32)
            sc = jnp.where(keep, sc, 0.0).astype(BF16)
            o = (jnp.dot(sc, vh, preferred_element_type=F32)
                 + lax.dot_general(qh, st.astype(BF16), nt, preferred_element_type=F32))
            kvt = lax.dot_general(vh, kkv, tn, preferred_element_type=F32)
            s_ref[h] = st * etot[:, kc] + kvt
            if finish:
                o = o + prev[rows, vc]
                ms = jnp.mean(o * o, axis=-1, keepdims=True)
                o = o * lax.rsqrt(ms + EPS) * ng_ref[...]
            o_scr[rows, vc] = o

    @pl.when(is_ctx)
    def _():
        oc_ref[0] = o_scr[...]

    @pl.when(jnp.logical_not(is_ctx))
    def _():
        ox_ref[0] = o_scr[...]


def _gla_pass(pc, gx, c0, wgp, bg, nh, dk, dv, gpad, reverse, fin=None):
    bsz, nctx = pc.shape[:2]
    seq = gx.shape[1]
    ts = GLA_STEP
    ncs, nxs = nctx // ts, seq // ts
    key, val = nh * dk, nh * dv
    if reverse:
        cidx = lambda k: jnp.maximum(ncs - 1 - k, 0)
        xidx = lambda k: jnp.minimum(nxs - 1 - (k - ncs), nxs - 1)
    else:
        cidx = lambda k: jnp.minimum(k, ncs - 1)
        xidx = lambda k: jnp.maximum(k - ncs, 0)

    def group(idx, base):
        return [pl.BlockSpec((1, ts, key), lambda b, k: (b, idx(k), base // key)),
                pl.BlockSpec((1, ts, key), lambda b, k: (b, idx(k), base // key + 1)),
                pl.BlockSpec((1, ts, val), lambda b, k: (b, idx(k), (base + 2 * key) // val)),
                pl.BlockSpec((1, ts, gpad), lambda b, k: (b, idx(k), (base + 2 * key + val) // gpad))]

    in_specs = group(cidx, c0) + group(xidx, 0) + [_const_spec(wgp.shape), _const_spec((1, key))]
    args = [pc] * 4 + [gx] * 4 + [wgp, bg.reshape(1, key)]
    if fin is not None:
        prev_c, prev_x, ng = fin
        in_specs += [pl.BlockSpec((1, ts, val), lambda b, k: (b, cidx(k), 0)),
                     pl.BlockSpec((1, ts, val), lambda b, k: (b, xidx(k), 0)), _const_spec((1, dv))]
        args += [prev_c, prev_x, ng.reshape(1, dv)]
    return pl.pallas_call(
        functools.partial(_gla_kernel, nh=nh, dk=dk, dv=dv, ncs=ncs, reverse=reverse, finish=fin is not None,
                          scale=dk ** -0.5),
        out_shape=(jax.ShapeDtypeStruct((bsz, nctx, val), F32), jax.ShapeDtypeStruct((bsz, seq, val), F32)),
        grid=(bsz, ncs + nxs),
        in_specs=in_specs,
        out_specs=(pl.BlockSpec((1, ts, val), lambda b, k: (b, cidx(k), 0)),
                   pl.BlockSpec((1, ts, val), lambda b, k: (b, xidx(k), 0))),
        scratch_shapes=[pltpu.VMEM((nh, dv, dk), F32), pltpu.VMEM((ts, val), F32)],
        compiler_params=_cparams("parallel", "arbitrary"),
        name="gla_rev" if reverse else "gla_fwd",
    )(*args)


def gla_mixer(pc, px, c0, q4, gpad, p):
    dv = p['gla_norm_g'].shape[0]
    nh = (2 * q4) // dv
    dk = q4 // nh
    rank = p['gla_wg'].shape[1]
    gx = _to_col_major(px[..., c0:])
    wgp = [jnp.zeros((2 * rank, q4), F32).at[d * rank:(d + 1) * rank].set(p['gla_wg'][d]) for d in range(2)]
    kw = dict(nh=nh, dk=dk, dv=dv, gpad=gpad)
    rc, rx = _gla_pass(pc, gx, c0, wgp[1], p['gla_bg'][1], reverse=True, **kw)
    oc, ox = _gla_pass(pc, gx, c0, wgp[0], p['gla_bg'][0], reverse=False, fin=(rc, rx, p['gla_norm_g']), **kw)
    return oc, _from_col_major(ox)


HY_N2 = 128
HY_PAD = 8


def _hy_dims(n):
    n1 = 2 * n // HY_N2
    k1 = n1 // 2 + 1
    k1p = -(-k1 // 8) * 8
    return n1, k1, k1p


def _hy_tables(n):
    n1, k1, k1p = _hy_dims(n)
    big = 2 * n
    kk = jnp.arange(k1p, dtype=jnp.int32)[:, None]
    mm = jnp.arange(n1, dtype=jnp.int32)[None, :]
    ang = (2.0 * math.pi / n1) * ((kk * mm) % n1).astype(F32)
    valid = (kk < k1).astype(F32)
    f1 = jnp.concatenate([jnp.cos(ang) * valid, -jnp.sin(ang) * valid], axis=0)
    wk = jnp.where((kk == 0) | (kk == n1 // 2), 1.0, 2.0) * valid / big
    f1inv = jnp.concatenate([(jnp.cos(ang) * wk).T, (-jnp.sin(ang) * wk).T], axis=1)
    k1i = jnp.arange(k1, dtype=jnp.int32)[:, None, None]
    k2i = jnp.arange(HY_N2, dtype=jnp.int32)[None, :, None]
    n2i = jnp.arange(HY_N2, dtype=jnp.int32)[None, None, :]
    ph = (2.0 * math.pi / big) * ((n2i * (k1i + n1 * k2i)) % big).astype(F32)
    gr, gi = jnp.cos(ph), -jnp.sin(ph)
    gs = jnp.concatenate([gr, gi], axis=1)
    gts = jnp.concatenate([gr.swapaxes(1, 2), gi.swapaxes(1, 2)], axis=1)
    return f1.astype(BF16), f1inv.astype(BF16), gs.astype(BF16), gts.astype(BF16)


def _hy_stage1(src_ref, f1, a_r, a_i, nslab, k1p):
    pitch = HY_N2 + HY_PAD

    def body(i, carry):
        n2 = 2 * i
        rows = jnp.concatenate([src_ref[pl.ds(n2, nslab, stride=pitch), :],
                                src_ref[pl.ds(n2 + 1, nslab, stride=pitch), :]], axis=1)
        out = jnp.dot(f1, rows.astype(BF16), preferred_element_type=F32)
        base = pl.multiple_of(n2 * k1p, 8)
        a_r[pl.ds(base, k1p), :] = out[:k1p, :LANE]
        a_i[pl.ds(base, k1p), :] = out[k1p:, :LANE]
        base1 = pl.multiple_of(base + k1p, 8)
        a_r[pl.ds(base1, k1p), :] = out[:k1p, LANE:]
        a_i[pl.ds(base1, k1p), :] = out[k1p:, LANE:]
        return carry

    lax.fori_loop(0, HY_N2 // 2, body, 0, unroll=4)


def _hy_stage2(a_r, a_i, gs_ref, k, k1p):
    ar = a_r[pl.ds(k, HY_N2, stride=k1p), :]
    ai = a_i[pl.ds(k, HY_N2, stride=k1p), :]
    rhs = jnp.concatenate([ar, ai], axis=1).astype(BF16)
    out = jnp.dot(gs_ref[k], rhs, preferred_element_type=F32)
    h = HY_N2
    return out[:h, :LANE] - out[h:, LANE:], out[:h, LANE:] + out[h:, :LANE]


def _hyena_conv_kernel(y_ref, g_ref, wy_ref, by_ref, wg_ref, bg_ref, bias_ref, hr_ref, hi_ref,
                       f1_ref, f1inv_ref, gs_ref, gts_ref, o_ref, ypad, zbuf, a_r, a_i, *, conv_y):
    n = y_ref.shape[1]
    nslab = n // HY_N2
    pitch = HY_N2 + HY_PAD
    k1 = gs_ref.shape[0]
    k1p = f1_ref.shape[0] // 2
    h = HY_N2

    def slab_conv(ref, i, w_ref, b_ref):
        r0 = pl.multiple_of(i * HY_N2, HY_N2)
        prev8 = ref[0, pl.ds(pl.multiple_of(jnp.maximum(r0 - 8, 0), 8), 8), :]
        next8 = ref[0, pl.ds(pl.multiple_of(jnp.minimum(r0 + HY_N2, n - 8), 8), 8), :]
        return _dwconv3_tile(ref[0, pl.ds(r0, HY_N2), :], prev8, next8, i == 0, i == nslab - 1, w_ref, b_ref)

    def y_slab(i):
        if conv_y:
            return slab_conv(y_ref, i, wy_ref, by_ref)
        return y_ref[0, pl.ds(pl.multiple_of(i * HY_N2, HY_N2), HY_N2), :]

    def fill(i, carry):
        ypad[pl.ds(pl.multiple_of(i * pitch, 8), HY_N2), :] = y_slab(i)
        return carry

    lax.fori_loop(0, nslab, fill, 0, unroll=2)
    _hy_stage1(ypad, f1_ref[...], a_r, a_i, nslab, k1p)

    def freq_fwd(k, carry):
        xr, xi = _hy_stage2(a_r, a_i, gs_ref, k, k1p)
        hr, hi = hr_ref[0, k], hi_ref[0, k]
        zbuf[k] = jnp.concatenate([xr * hr - xi * hi, xr * hi + xi * hr], axis=1).astype(BF16)
        return carry

    lax.fori_loop(0, k1, freq_fwd, 0, unroll=4)

    def freq_inv(k, carry):
        out = jnp.dot(gts_ref[k], zbuf[k], preferred_element_type=F32)
        a_r[pl.ds(k, HY_N2, stride=k1p), :] = out[:h, :LANE] + out[h:, LANE:]
        a_i[pl.ds(k, HY_N2, stride=k1p), :] = out[:h, LANE:] - out[h:, :LANE]
        return carry

    lax.fori_loop(0, k1, freq_inv, 0, unroll=4)
    f1inv = f1inv_ref[...][:nslab]

    def inv1(i, carry):
        n2 = 2 * i
        b0 = pl.multiple_of(n2 * k1p, 8)
        b1 = pl.multiple_of(b0 + k1p, 8)
        rhs = jnp.concatenate(
            [jnp.concatenate([a_r[pl.ds(b0, k1p), :], a_i[pl.ds(b0, k1p), :]], axis=0),
             jnp.concatenate([a_r[pl.ds(b1, k1p), :], a_i[pl.ds(b1, k1p), :]], axis=0)], axis=1).astype(BF16)
        out = jnp.dot(f1inv, rhs, preferred_element_type=F32)
        ypad[pl.ds(n2, nslab, stride=pitch), :] = out[:, :LANE]
        ypad[pl.ds(n2 + 1, nslab, stride=pitch), :] = out[:, LANE:]
        return carry

    lax.fori_loop(0, HY_N2 // 2, inv1, 0, unroll=4)

    def finish(i, carry):
        p0 = pl.multiple_of(i * pitch, 8)
        gate = slab_conv(g_ref, i, wg_ref, bg_ref)
        o_ref[0, pl.ds(pl.multiple_of(i * HY_N2, HY_N2), HY_N2), :] = gate * (ypad[pl.ds(p0, HY_N2), :]
                                                                                + y_slab(i) * bias_ref[0])
        return carry

    lax.fori_loop(0, nslab, finish, 0, unroll=2)


def _hyena_spectrum_kernel(f_ref, f1_ref, gs_ref, hr_ref, hi_ref, a_r, a_i, fpad):
    big = f_ref.shape[1]
    nslab = big // HY_N2
    pitch = HY_N2 + HY_PAD
    k1 = gs_ref.shape[0]
    k1p = f1_ref.shape[0] // 2

    def fill(i, carry):
        fpad[pl.ds(pl.multiple_of(i * pitch, 8), HY_N2), :] = f_ref[0, pl.ds(pl.multiple_of(i * HY_N2, HY_N2), HY_N2), :]
        return carry

    lax.fori_loop(0, nslab, fill, 0, unroll=2)
    _hy_stage1(fpad, f1_ref[...], a_r, a_i, nslab, k1p)

    def freq(k, carry):
        xr, xi = _hy_stage2(a_r, a_i, gs_ref, k, k1p)
        hr_ref[0, k] = xr
        hi_ref[0, k] = xi
        return carry

    lax.fori_loop(0, k1, freq, 0, unroll=2)


def _one(shape, index_map):
    return pl.BlockSpec(shape, index_map, pipeline_mode=pl.Buffered(1))


def hyena_spectrum(filt, tabs):
    f1, _, gs, _ = tabs
    r, big, w = filt.shape
    ns = w // LANE
    n1, k1, k1p = _hy_dims(big // 2)
    shp = jax.ShapeDtypeStruct((r * ns, k1, HY_N2, LANE), F32)
    spec_o = pl.BlockSpec((1, k1, HY_N2, LANE), lambda i, j: (i * ns + j, 0, 0, 0))
    return pl.pallas_call(
        _hyena_spectrum_kernel,
        out_shape=(shp, shp),
        grid=(r, ns),
        in_specs=[_one((1, big, LANE), lambda i, j: (i, 0, j)),
                  _one(f1.shape, lambda i, j: (0, 0)),
                  _one(gs.shape, lambda i, j: (0, 0, 0))],
        out_specs=(spec_o, spec_o),
        scratch_shapes=[pltpu.VMEM((HY_N2 * k1p, LANE), F32), pltpu.VMEM((HY_N2 * k1p, LANE), F32),
                        pltpu.VMEM((n1 * (HY_N2 + HY_PAD), LANE), F32)],
        compiler_params=_cparams("parallel", "parallel"),
        name="hyena_spectrum",
    )(filt, f1, gs)


def hyena_order(y, ycol, g, gcol, conv_w, conv_b, cy, cg, bias, hr, hi, order, tabs, conv_y):
    f1, f1inv, gs, gts = tabs
    bsz, n = y.shape[:2]
    w = bias.shape[-1]
    ns = w // LANE
    n1, k1, k1p = _hy_dims(n)
    f1d = f1[:, :n1 // 2]
    pitch = HY_N2 + HY_PAD
    nslab = n // HY_N2
    cw = lambda c: _one((3, LANE), lambda j, b: (0, c + j))
    cb = lambda c: _one((1, LANE), lambda j, b: (0, c + j))
    hspec = _one((1, k1, HY_N2, LANE), lambda j, b: (order * ns + j, 0, 0, 0))
    return pl.pallas_call(
        functools.partial(_hyena_conv_kernel, conv_y=conv_y),
        out_shape=jax.ShapeDtypeStruct((bsz, n, w), F32),
        grid=(ns, bsz),
        in_specs=[_one((1, n, LANE), lambda j, b: (b, 0, ycol + j)),
                  _one((1, n, LANE), lambda j, b: (b, 0, gcol + j)),
                  cw(cy), cb(cy), cw(cg), cb(cg),
                  _one((1, 1, LANE), lambda j, b: (order, 0, j)),
                  hspec, hspec,
                  _one(f1d.shape, lambda j, b: (0, 0)), _one(f1inv.shape, lambda j, b: (0, 0)),
                  _one(gs.shape, lambda j, b: (0, 0, 0)), _one(gts.shape, lambda j, b: (0, 0, 0))],
        out_specs=pl.BlockSpec((1, n, LANE), lambda j, b: (b, 0, j)),
        scratch_shapes=[pltpu.VMEM((nslab * pitch, LANE), F32), pltpu.VMEM((k1, HY_N2, 2 * LANE), BF16),
                        pltpu.VMEM((HY_N2 * k1p, LANE), F32), pltpu.VMEM((HY_N2 * k1p, LANE), F32)],
        compiler_params=pltpu.CompilerParams(dimension_semantics=("parallel", "parallel"),
                                             vmem_limit_bytes=BIG_VMEM_LIMIT),
        name="hyena_order",
    )(y, g, conv_w, conv_b, conv_w, conv_b, bias.reshape(bias.shape[0], 1, w), hr, hi, f1d, f1inv, gs, gts)


def hyena_latent(px, c0, filt, p):
    w = p['hy_bias'].shape[-1]
    ns = w // LANE
    n = px.shape[1]
    tabs = _hy_tables(n)
    hr, hi = hyena_spectrum(filt, tabs)
    cw, cb = p['hy_conv_w'], p['hy_conv_b'].reshape(1, -1)
    b0 = c0 // LANE
    y1 = hyena_order(px, b0, px, b0 + ns, cw, cb, 0, ns, p['hy_bias'], hr, hi, 0, tabs, True)
    return hyena_order(y1, 0, px, b0 + 2 * ns, cw, cb, 0, 2 * ns, p['hy_bias'], hr, hi, 1, tabs, False)


def _hyena_ctx_kernel(zv_ref, z1_ref, z2_ref, wv_ref, bv_ref, w1_ref, b1_ref, w2_ref, b2_ref, bias_ref,
                      filt_ref, ff_ref, finv_ref, o_ref):
    n = zv_ref.shape[1]
    hp = lax.Precision.HIGHEST
    zero8 = jnp.zeros((8, LANE), F32)
    conv = lambda ref, w, b: _dwconv3_tile(ref[0], zero8, zero8, True, True, w, b)
    ff = ff_ref[...]
    kp = ff.shape[0] // 2
    y = conv(zv_ref, wv_ref, bv_ref)
    for o, (g_ref, w, b) in enumerate(((z1_ref, w1_ref, b1_ref), (z2_ref, w2_ref, b2_ref))):
        hsp = jnp.dot(ff, filt_ref[o], preferred_element_type=F32, precision=hp)
        ysp = jnp.dot(ff[:, :n], y, preferred_element_type=F32, precision=hp)
        hr, hi, yr, yi = hsp[:kp], hsp[kp:], ysp[:kp], ysp[kp:]
        z = jnp.concatenate([yr * hr - yi * hi, yr * hi + yi * hr], axis=0)
        cv = jnp.dot(finv_ref[...], z, preferred_element_type=F32, precision=hp)
        y = conv(g_ref, w, b) * (cv + y * bias_ref[o])
    o_ref[0] = y


def hyena_context(pc, c0, filt, p):
    bsz, n = pc.shape[:2]
    w = p['hy_bias'].shape[-1]
    ns = w // LANE
    kp = -(-(n + 1) // 8) * 8
    kk = jnp.arange(kp, dtype=jnp.int32)[:, None]
    mm = jnp.arange(2 * n, dtype=jnp.int32)[None, :]
    ang = (math.pi / n) * ((kk * mm) % (2 * n)).astype(F32)
    valid = (kk <= n).astype(F32)
    ff = jnp.concatenate([jnp.cos(ang) * valid, -jnp.sin(ang) * valid], axis=0)
    wk = jnp.where((kk == 0) | (kk == n), 1.0, 2.0) * valid / (2 * n)
    finv = jnp.concatenate([(jnp.cos(ang) * wk).T[:n], (-jnp.sin(ang) * wk).T[:n]], axis=1)
    b0 = c0 // LANE
    zs = lambda c: pl.BlockSpec((1, n, LANE), lambda j, b: (b, 0, b0 + c + j))
    cw = lambda c: pl.BlockSpec((3, LANE), lambda j, b: (0, c + j))
    cb = lambda c: pl.BlockSpec((1, LANE), lambda j, b: (0, c + j))
    conv_w, conv_b = p['hy_conv_w'], p['hy_conv_b'].reshape(1, -1)
    return pl.pallas_call(
        _hyena_ctx_kernel,
        out_shape=jax.ShapeDtypeStruct((bsz, n, w), F32),
        grid=(ns, bsz),
        in_specs=[zs(0), zs(ns), zs(2 * ns), cw(0), cb(0), cw(ns), cb(ns), cw(2 * ns), cb(2 * ns),
                  pl.BlockSpec((2, 1, LANE), lambda j, b: (0, 0, j)),
                  pl.BlockSpec((2, 2 * n, LANE), lambda j, b: (0, 0, j)),
                  pl.BlockSpec(ff.shape, lambda j, b: (0, 0)), pl.BlockSpec(finv.shape, lambda j, b: (0, 0))],
        out_specs=pl.BlockSpec((1, n, LANE), lambda j, b: (b, 0, j)),
        compiler_params=_cparams("parallel", "parallel"),
        name="hyena_context",
    )(pc, pc, pc, conv_w, conv_b, conv_w, conv_b, conv_w, conv_b, p['hy_bias'].reshape(2, 1, w), filt, ff, finv)


def _to_col_major(t):
    bsz, n = t.shape[:2]
    rows = n // GRID_W
    return t.reshape(bsz, rows, GRID_W, *t.shape[2:]).swapaxes(1, 2).reshape(bsz, n, *t.shape[2:])


def _from_col_major(t):
    bsz, n = t.shape[:2]
    rows = n // GRID_W
    return t.reshape(bsz, GRID_W, rows, *t.shape[2:]).swapaxes(1, 2).reshape(bsz, n, *t.shape[2:])


def _filter_positions(n):
    r = jnp.arange(2 * n, dtype=jnp.int32)
    return jnp.where(r < n, r, 2 * n - r).astype(F32)


def _filter_feats(n):
    pos = _filter_positions(n)
    t = pos / max(n - 1, 1)
    freqs = jnp.linspace(1e-4, HY_POS_FREQS - 1, HY_POS_FREQS, dtype=F32)
    ang = (2.0 * math.pi / n) * pos[:, None] * freqs[None]
    feats = jnp.concatenate([t[:, None], jnp.cos(ang), -jnp.sin(ang)], axis=-1)
    return jnp.pad(feats, ((0, 0), (0, LANE - feats.shape[1])))


def _filter_hidden_kernel(f_ref, w1_ref, b1_ref, s1_ref, w2_ref, b2_ref, s2_ref, o_ref):
    hp = lax.Precision.HIGHEST
    h = jnp.sin(s1_ref[...] * (jnp.dot(f_ref[...], w1_ref[...], preferred_element_type=F32, precision=hp)
                               + b1_ref[...]))
    o_ref[...] = jnp.sin(s2_ref[...] * (jnp.dot(h, w2_ref[...], preferred_element_type=F32, precision=hp)
                                        + b2_ref[...]))


def _filter_out_kernel(h_ref, wf_ref, wb_ref, bf_ref, bb_ref, rate_ref, o_ref, *, n, chunk):
    nchunk = 2 * n // chunk
    inv_span = 1.0 / max(n - 1, 1)

    def emit(c, acc):
        r0 = pl.multiple_of(c * chunk, chunk)
        past = c < nchunk // 2
        w = jnp.where(past, wf_ref[...], wb_ref[...])
        bias = jnp.where(past, bf_ref[...], bb_ref[...])
        val = jnp.dot(h_ref[pl.ds(r0, chunk), :], w, preferred_element_type=F32,
                      precision=lax.Precision.HIGHEST) + bias
        r = r0 + lax.broadcasted_iota(jnp.int32, (chunk, LANE), 0)
        t = jnp.where(r < n, r, 2 * n - r).astype(F32) * inv_span
        val = jnp.where(r == n, 0.0, val * jnp.exp(-t * rate_ref[...]))
        o_ref[0, pl.ds(r0, chunk), :] = val
        return acc + jnp.sum(jnp.abs(val), axis=0, keepdims=True)

    total = lax.fori_loop(0, nchunk, emit, jnp.zeros((1, LANE), F32))
    scale = 1.0 / (total + EPS)

    def rescale(c, carry):
        r0 = pl.multiple_of(c * chunk, chunk)
        o_ref[0, pl.ds(r0, chunk), :] = o_ref[0, pl.ds(r0, chunk), :] * scale
        return carry

    lax.fori_loop(0, nchunk, rescale, 0)


def hyena_filters(n, p):
    order, width = p['hy_bias'].shape
    ns = width // LANE
    hid = p['hy_f_w2'].shape[0]
    feats = _filter_feats(n)
    padc = lambda a: jnp.pad(a.reshape(1, -1), ((0, 0), (0, LANE - a.shape[-1])))
    w1 = jnp.pad(p['hy_f_w1'], ((0, LANE - p['hy_f_w1'].shape[0]), (0, LANE - hid)))
    w2 = jnp.pad(p['hy_f_w2'], ((0, LANE - hid), (0, LANE - hid)))
    w3 = jnp.pad(p['hy_f_w3'], ((0, LANE - hid), (0, 0)))
    tr = _pick_tile(2 * n, 2048, 8)
    full = lambda i: (0, 0)
    hidden = pl.pallas_call(
        _filter_hidden_kernel,
        out_shape=jax.ShapeDtypeStruct((2 * n, LANE), F32),
        grid=(2 * n // tr,),
        in_specs=[pl.BlockSpec((tr, LANE), lambda i: (i, 0)),
                  pl.BlockSpec((LANE, LANE), full), pl.BlockSpec((1, LANE), full), pl.BlockSpec((1, LANE), full),
                  pl.BlockSpec((LANE, LANE), full), pl.BlockSpec((1, LANE), full), pl.BlockSpec((1, LANE), full)],
        out_specs=pl.BlockSpec((tr, LANE), lambda i: (i, 0)),
        compiler_params=_cparams("parallel"),
        name="hyena_filter_hidden",
    )(feats, w1, padc(p['hy_f_b1']), padc(p['hy_f_freq1']), w2, padc(p['hy_f_b2']), padc(p['hy_f_freq2']))
    rates = jnp.abs(jnp.linspace(math.log(HY_DECAY_TARGET) / HY_LONG_DECAY_PCT,
                                 math.log(HY_DECAY_TARGET) / HY_SHORT_DECAY_PCT, width, dtype=F32)).reshape(1, width)
    b3 = p['hy_f_b3'].reshape(1, -1)
    return pl.pallas_call(
        functools.partial(_filter_out_kernel, n=n, chunk=min(1024, n)),
        out_shape=jax.ShapeDtypeStruct((order, 2 * n, width), F32),
        grid=(order, ns),
        in_specs=[_one((2 * n, LANE), lambda o, j: (0, 0)),
                  pl.BlockSpec((LANE, LANE), lambda o, j: (0, 2 * o * ns + j)),
                  pl.BlockSpec((LANE, LANE), lambda o, j: (0, (2 * o + 1) * ns + j)),
                  pl.BlockSpec((1, LANE), lambda o, j: (0, 2 * o * ns + j)),
                  pl.BlockSpec((1, LANE), lambda o, j: (0, (2 * o + 1) * ns + j)),
                  pl.BlockSpec((1, LANE), lambda o, j: (0, j))],
        out_specs=pl.BlockSpec((1, 2 * n, LANE), lambda o, j: (o, 0, j)),
        compiler_params=_cparams("parallel", "parallel"),
        name="hyena_filter_out",
    )(hidden, w3, w3, b3, b3, rates)


def kernel(x, c, ctx, c_ctx, w_mod, b_mod, norm1_g, norm2_g, w_in, s5_a_re, s5_a_im, s5_log_step, s5_b_re, s5_b_im, s5_c_re, s5_c_im, s5_d, s5_glu_w, s5_glu_b, hy_conv_w, hy_conv_b, hy_f_w1, hy_f_b1, hy_f_freq1, hy_f_w2, hy_f_b2, hy_f_freq2, hy_f_w3, hy_f_b3, hy_bias, gla_wg, gla_bg, gla_norm_g, w_branch, w_out, ff_w_up, ff_conv_w, ff_conv_b, ff_w_down, final_norm_g):
    bsz, seq, d = x.shape
    nctx = ctx.shape[1]
    depth = w_mod.shape[0]
    q4 = d // 4
    rank2 = 2 * gla_wg.shape[2]
    gpad = max(LANE, q4 // 2)
    assert bsz + 1 <= 8

    c_gk = q4
    c_gv = 2 * q4
    c_gg = 4 * q4
    c_gq = c_gg + rank2
    c_gr = c_gq + q4
    c_hy = c_gr + 2 * q4
    c_mg = c_hy + 3 * q4

    rows = jnp.zeros((8, d), F32).at[:bsz].set(c).at[bsz].set(c_ctx)
    mod = modulation(rows, w_mod, b_mod)

    xs = x.reshape(bsz * seq, d)
    cs = ctx.reshape(bsz * nctx, d)
    for l in range(depth):
        ctx_out = l < depth - 1
        sh1, s1, g1, sh2, s2, g2 = [mod[l, :, i * d:(i + 1) * d][:, None, :] for i in range(6)]
        gs1 = norm1_g[l] * (1.0 + s1)
        gs2 = norm2_g[l] * (1.0 + s2)
        bx = slice(0, bsz)
        bc = slice(bsz, bsz + 1)

        wl = w_in[l]
        w_pack = jnp.concatenate([
            wl[:, 0:c_gk], wl[:, c_hy:c_mg], wl[:, c_gr:c_hy], wl[:, c_gk:c_gv], wl[:, c_gq:c_gr],
            wl[:, c_gv:c_gg], wl[:, c_gg:c_gq],
            jnp.zeros((d, gpad - rank2), F32)], axis=1).astype(BF16)
        w_gate = wl[:, c_mg:].reshape(d, 3, d).swapaxes(0, 1).astype(BF16)
        wb = w_branch[l].astype(BF16)
        w_o = w_out[l].astype(BF16)
        w_up = ff_w_up[l].astype(BF16)
        w_dn = ff_w_down[l].astype(BF16)

        p = dict(s5_a_re=s5_a_re[l], s5_a_im=s5_a_im[l], s5_log_step=s5_log_step[l],
                 s5_b_re=s5_b_re[l], s5_b_im=s5_b_im[l], s5_c_re=s5_c_re[l], s5_c_im=s5_c_im[l],
                 s5_d=s5_d[l], s5_glu_w=s5_glu_w[l], s5_glu_b=s5_glu_b[l],
                 hy_conv_w=hy_conv_w[l], hy_conv_b=hy_conv_b[l], hy_f_w1=hy_f_w1[l], hy_f_b1=hy_f_b1[l],
                 hy_f_freq1=hy_f_freq1[l], hy_f_w2=hy_f_w2[l], hy_f_b2=hy_f_b2[l],
                 hy_f_freq2=hy_f_freq2[l], hy_f_w3=hy_f_w3[l], hy_f_b3=hy_f_b3[l], hy_bias=hy_bias[l],
                 gla_wg=gla_wg[l], gla_bg=gla_bg[l], gla_norm_g=gla_norm_g[l])

        px = norm_matmul(xs, gs1[bx], sh1[bx], w_pack).reshape(bsz, seq, -1)
        pc = norm_matmul(cs, gs1[bc], sh1[bc], w_pack).reshape(bsz, nctx, -1)

        ya_c, ya_x = s5_mixer(pc, px, q4, p)
        oc_c, oc_x = gla_mixer(pc, px, 6 * q4, q4, gpad, p)
        yb_x = hyena_latent(px, q4, hyena_filters(seq, p), p)

        flat = lambda t: t.reshape(-1, t.shape[-1])
        mx = merge(xs, gs1[bx], sh1[bx], flat(ya_x), flat(yb_x), flat(oc_x), flat(px), 2, w_gate, wb)
        xs = matmul_residual(mx, w_o, xs, g1[bx])
        xs = ffn(xs, bsz, gs2[bx], sh2[bx], w_up, ff_conv_w[l], ff_conv_b[l], w_dn, g2[bx])

        if ctx_out:
            yb_c = hyena_context(pc, q4, hyena_filters(nctx, p), p)
            mc = merge(cs, gs1[bc], sh1[bc], flat(ya_c), flat(yb_c), flat(oc_c), flat(pc), 2, w_gate, wb)
            cs = matmul_residual(mc, w_o, cs, g1[bc])
            cs = ffn(cs, bsz, gs2[bc], sh2[bc], w_up, ff_conv_w[l], ff_conv_b[l], w_dn, g2[bc])

    return final_rmsnorm(xs, final_norm_g).reshape(bsz, seq, d)
```

```python
import functools
import math

import jax
import jax.numpy as jnp
from jax import lax
from jax.experimental import pallas as pl
from jax.experimental.pallas import tpu as pltpu

F32 = jnp.float32
BF16 = jnp.bfloat16
EPS = 1e-6
GRID_W = 64
S5_GROUP = 16
GLA_CHUNK = 64
GLA_GATE_TEMP = 16.0
HY_POS_FREQS = 16
HY_DECAY_TARGET = 1e-2
HY_SHORT_DECAY_PCT = 0.3
HY_LONG_DECAY_PCT = 1.5
LANE = 128
VMEM_LIMIT = 56 * 1024 * 1024
BIG_VMEM_LIMIT = 60 * 1024 * 1024
FFN_TF = 512


def _pick_tile(n, cap, mult=LANE):
    best = None
    for t in range(mult, min(n, cap) + 1, mult):
        if n % t == 0:
            best = t
    assert best is not None, (n, cap, mult)
    return best


def _col_blocks(w, tn):
    nj = w.shape[-1] // tn
    return jnp.moveaxis(w.reshape(*w.shape[:-1], nj, tn), -2, 0)


def _cast_blocks_kernel(w_ref, o_ref):
    o_ref[0] = w_ref[...].astype(o_ref.dtype)


def cast_col_blocks(w, tn):
    k, n = w.shape
    return pl.pallas_call(
        _cast_blocks_kernel,
        out_shape=jax.ShapeDtypeStruct((n // tn, k, tn), BF16),
        grid=(n // tn,),
        in_specs=[pl.BlockSpec((k, tn), lambda j: (0, j))],
        out_specs=pl.BlockSpec((1, k, tn), lambda j: (j, 0, 0)),
        compiler_params=pltpu.CompilerParams(dimension_semantics=("parallel",), vmem_limit_bytes=VMEM_LIMIT),
        name="cast_col_blocks",
    )(w)


def _cparams(*sem):
    return pltpu.CompilerParams(dimension_semantics=sem, vmem_limit_bytes=VMEM_LIMIT)


def _mod_kernel(r_ref, w_ref, b_ref, o_ref):
    r = r_ref[...]
    s = r * jax.nn.sigmoid(r)
    o_ref[0] = jnp.dot(s, w_ref[0], preferred_element_type=F32,
                       precision=lax.Precision.HIGHEST) + b_ref[0]


def modulation(rows, w_mod, b_mod):
    depth, d, n = w_mod.shape
    tn = _pick_tile(n, 1024)
    return pl.pallas_call(
        _mod_kernel,
        out_shape=jax.ShapeDtypeStruct((depth, 8, n), F32),
        grid=(depth, n // tn),
        in_specs=[pl.BlockSpec((8, d), lambda l, j: (0, 0)),
                  pl.BlockSpec((1, d, tn), lambda l, j: (l, 0, j)),
                  pl.BlockSpec((1, 1, tn), lambda l, j: (l, 0, j))],
        out_specs=pl.BlockSpec((1, 8, tn), lambda l, j: (l, 0, j)),
        compiler_params=_cparams("parallel", "parallel"),
        name="modulation",
    )(rows, w_mod, b_mod.reshape(depth, 1, n))


def _norm_mm_kernel(x_ref, gs_ref, sh_ref, w_ref, o_ref, h_ref):
    @pl.when(pl.program_id(1) == 0)
    def _():
        x = x_ref[...]
        ms = jnp.mean(x * x, axis=-1, keepdims=True)
        h_ref[...] = (x * lax.rsqrt(ms + EPS) * gs_ref[0] + sh_ref[0]).astype(BF16)

    o_ref[...] = jnp.dot(h_ref[...], w_ref[0], preferred_element_type=F32).astype(o_ref.dtype)


def norm_matmul(x, gs, sh, w, out_dtype=F32):
    m, d = x.shape
    n = w.shape[1]
    nb = gs.shape[0]
    tm = _pick_tile(m // nb, 1024, 8)
    tn = _pick_tile(n, 2048)
    tpb = (m // nb) // tm
    return pl.pallas_call(
        _norm_mm_kernel,
        out_shape=jax.ShapeDtypeStruct((m, n), out_dtype),
        grid=(m // tm, n // tn),
        in_specs=[pl.BlockSpec((tm, d), lambda i, j: (i, 0)),
                  pl.BlockSpec((1, 1, d), lambda i, j: (i // tpb, 0, 0)),
                  pl.BlockSpec((1, 1, d), lambda i, j: (i // tpb, 0, 0)),
                  pl.BlockSpec((1, d, tn), lambda i, j: (j, 0, 0))],
        out_specs=pl.BlockSpec((tm, tn), lambda i, j: (i, j)),
        scratch_shapes=[pltpu.VMEM((tm, d), BF16)],
        compiler_params=_cparams("parallel", "arbitrary"),
        name="norm_matmul",
    )(x, gs, sh, _col_blocks(w, tn))


def _mm_res_kernel(a_ref, w_ref, r_ref, g_ref, o_ref):
    acc = jnp.dot(a_ref[...], w_ref[0], preferred_element_type=F32)
    o_ref[...] = r_ref[...] + g_ref[0] * acc


def matmul_residual(a, w, res, g):
    m, k = a.shape
    n = w.shape[1]
    nb = g.shape[0]
    tm = _pick_tile(m // nb, 1024, 8)
    tn = _pick_tile(n, 2048)
    tpb = (m // nb) // tm
    resident = pl.Buffered(1) if n == tn else None
    return pl.pallas_call(
        _mm_res_kernel,
        out_shape=jax.ShapeDtypeStruct((m, n), F32),
        grid=(m // tm, n // tn),
        in_specs=[pl.BlockSpec((tm, k), lambda i, j: (i, 0)),
                  pl.BlockSpec((1, k, tn), lambda i, j: (j, 0, 0), pipeline_mode=resident),
                  pl.BlockSpec((tm, tn), lambda i, j: (i, j)),
                  pl.BlockSpec((1, 1, tn), lambda i, j: (i // tpb, 0, j))],
        out_specs=pl.BlockSpec((tm, tn), lambda i, j: (i, j)),
        compiler_params=_cparams("parallel", "arbitrary"),
        name="matmul_residual",
    )(a, _col_blocks(w, tn), res, g)


def _merge_kernel(x_ref, gs_ref, sh_ref, ya_ref, yb_ref, oc_ref, r_ref, wg_ref, wb_ref, o_ref, h_ref, y_ref):
    wa, wb = ya_ref.shape[1], yb_ref.shape[1]

    @pl.when(pl.program_id(1) == 0)
    def _():
        x = x_ref[...]
        ms = jnp.mean(x * x, axis=-1, keepdims=True)
        h_ref[...] = (x * lax.rsqrt(ms + EPS) * gs_ref[0] + sh_ref[0]).astype(BF16)
        r = r_ref[...]
        y_ref[:, 0:wa] = ya_ref[...].astype(BF16)
        y_ref[:, wa:wa + wb] = yb_ref[...].astype(BF16)
        y_ref[:, wa + wb:] = (oc_ref[...] * (r * jax.nn.sigmoid(r))).astype(BF16)

    h = h_ref[...]
    m = None
    for i, (lo, hi) in enumerate(((0, wa), (wa, wa + wb), (wa + wb, y_ref.shape[1]))):
        gate = jax.nn.sigmoid(jnp.dot(h, wg_ref[0, i], preferred_element_type=F32))
        br = jnp.dot(y_ref[:, lo:hi], wb_ref[0, lo:hi, :], preferred_element_type=F32)
        m = gate * br if m is None else m + gate * br
    o_ref[...] = m.astype(o_ref.dtype)


def merge(x, gs, sh, ya, yb, oc, pr, rcol, wg, wb):
    m, d = x.shape
    nb = gs.shape[0]
    tm = _pick_tile(m // nb, 512, 8)
    tn = _pick_tile(d, 512)
    tpb = (m // nb) // tm
    row = lambda i, j: (i, 0)
    wy = wb.shape[0]
    return pl.pallas_call(
        _merge_kernel,
        out_shape=jax.ShapeDtypeStruct((m, d), BF16),
        grid=(m // tm, d // tn),
        in_specs=[pl.BlockSpec((tm, d), row),
                  pl.BlockSpec((1, 1, d), lambda i, j: (i // tpb, 0, 0)),
                  pl.BlockSpec((1, 1, d), lambda i, j: (i // tpb, 0, 0)),
                  pl.BlockSpec((tm, ya.shape[1]), row),
                  pl.BlockSpec((tm, yb.shape[1]), row),
                  pl.BlockSpec((tm, oc.shape[1]), row),
                  pl.BlockSpec((tm, oc.shape[1]), lambda i, j: (i, rcol)),
                  pl.BlockSpec((1, 3, d, tn), lambda i, j: (j, 0, 0, 0)),
                  pl.BlockSpec((1, wy, tn), lambda i, j: (j, 0, 0))],
        out_specs=pl.BlockSpec((tm, tn), lambda i, j: (i, j)),
        scratch_shapes=[pltpu.VMEM((tm, d), BF16), pltpu.VMEM((tm, wy), BF16)],
        compiler_params=_cparams("parallel", "arbitrary"),
        name="merge",
    )(x, gs, sh, ya, yb, oc, pr, _col_blocks(wg, tn), _col_blocks(wb, tn))


def _dwconv3_tile(a, prev8, next8, first, last, w_ref, cb_ref):
    tr = a.shape[0]
    row = lax.broadcasted_iota(jnp.int32, a.shape, 0)
    before = jnp.where(first, 0.0, prev8[7:8, :])
    after = jnp.where(last, 0.0, next8[0:1, :])
    prev = jnp.where(row == 0, before, pltpu.roll(a, 1, 0))
    nxt = jnp.where(row == tr - 1, after, pltpu.roll(a, tr - 1, 0))
    return prev * w_ref[0:1, :] + a * w_ref[1:2, :] + nxt * w_ref[2:3, :] + cb_ref[...]


def _ffn_kernel(x_ref, xp_ref, xn_ref, gs_ref, sh_ref, wa_ref, wb_ref, cw_ref, cb_ref, wd_ref, g_ref,
                o_ref, h_ref, hh_ref, *, tpb):
    i, f = pl.program_id(0), pl.program_id(1)

    def normed(x):
        ms = jnp.mean(x * x, axis=-1, keepdims=True)
        return (x * lax.rsqrt(ms + EPS) * gs_ref[0] + sh_ref[0]).astype(BF16)

    @pl.when(f == 0)
    def _():
        h_ref[...] = normed(x_ref[...])
        hh_ref[0:8, :] = normed(xp_ref[...])
        hh_ref[8:16, :] = normed(xn_ref[...])
        o_ref[...] = jnp.zeros_like(o_ref)

    wa = wa_ref[0]
    a = jnp.dot(h_ref[...], wa, preferred_element_type=F32)
    ah = jnp.dot(hh_ref[...], wa, preferred_element_type=F32)
    b = jnp.dot(h_ref[...], wb_ref[0], preferred_element_type=F32)
    pos = i % tpb
    conv = _dwconv3_tile(a, ah[0:8], ah[8:16], pos == 0, pos == tpb - 1, cw_ref, cb_ref)
    act = (conv * jax.nn.sigmoid(conv) * b).astype(BF16)
    half = o_ref.shape[1] // 2
    for lo in (0, half):
        o_ref[:, lo:lo + half] += jnp.dot(act, wd_ref[:, lo:lo + half], preferred_element_type=F32)

    @pl.when(f == pl.num_programs(1) - 1)
    def _():
        o_ref[...] = x_ref[...] + g_ref[0] * o_ref[...]


def ffn(x, nseq, gs, sh, w_up, conv_w, conv_b, w_dn, g):
    m, d = x.shape
    fh = w_dn.shape[0]
    nb = gs.shape[0]
    tm = _pick_tile(m // nseq, 1024, 8)
    tf = w_up.shape[2]
    tpb = (m // nseq) // tm
    tpm = (m // nb) // tm
    nf = fh // tf
    r8 = tm // 8
    last8 = m // 8 - 1
    mod = lambda i, f: (i // tpm, 0, 0)
    return pl.pallas_call(
        functools.partial(_ffn_kernel, tpb=tpb),
        out_shape=jax.ShapeDtypeStruct((m, d), F32),
        grid=(m // tm, nf),
        in_specs=[pl.BlockSpec((tm, d), lambda i, f: (i, 0), pipeline_mode=pl.Buffered(1)),
                  pl.BlockSpec((8, d), lambda i, f: (jnp.maximum(i * r8 - 1, 0), 0)),
                  pl.BlockSpec((8, d), lambda i, f: (jnp.minimum((i + 1) * r8, last8), 0)),
                  pl.BlockSpec((1, 1, d), mod), pl.BlockSpec((1, 1, d), mod),
                  pl.BlockSpec((1, d, tf), lambda i, f: (f, 0, 0)),
                  pl.BlockSpec((1, d, tf), lambda i, f: (f + nf, 0, 0)),
                  pl.BlockSpec((3, tf), lambda i, f: (0, f)),
                  pl.BlockSpec((1, tf), lambda i, f: (0, f)),
                  pl.BlockSpec((tf, d), lambda i, f: (f, 0)),
                  pl.BlockSpec((1, 1, d), mod)],
        out_specs=pl.BlockSpec((tm, d), lambda i, f: (i, 0)),
        scratch_shapes=[pltpu.VMEM((tm, d), BF16), pltpu.VMEM((16, d), BF16)],
        compiler_params=pltpu.CompilerParams(dimension_semantics=("parallel", "arbitrary"),
                                             vmem_limit_bytes=BIG_VMEM_LIMIT),
        name="ffn",
    )(x, x, x, gs, sh, w_up, w_up, conv_w, conv_b.reshape(1, fh), w_dn, g)


def _rmsnorm_kernel(x_ref, g_ref, o_ref):
    x = x_ref[...]
    ms = jnp.mean(x * x, axis=-1, keepdims=True)
    o_ref[...] = x * lax.rsqrt(ms + EPS) * g_ref[...]


def final_rmsnorm(x, g):
    m, d = x.shape
    tm = _pick_tile(m, 1024, 8)
    return pl.pallas_call(
        _rmsnorm_kernel,
        out_shape=jax.ShapeDtypeStruct((m, d), F32),
        grid=(m // tm,),
        in_specs=[pl.BlockSpec((tm, d), lambda i: (i, 0)), pl.BlockSpec((1, d), lambda i: (0, 0))],
        out_specs=pl.BlockSpec((tm, d), lambda i: (i, 0)),
        compiler_params=_cparams("parallel"),
        name="final_rmsnorm",
    )(x, g.reshape(1, d))


S5_CHUNK = 256


def _const_spec(shape):
    zeros = (0,) * len(shape)
    return pl.BlockSpec(shape, lambda b, k: zeros, pipeline_mode=pl.Buffered(1))


def _s5_kernel(*refs, ncc, reverse, finish):
    if finish:
        (uc_ref, ux_ref, bblk_ref, cblk_ref, enr_ref, eni_ref, epr_ref, epi_ref, ac_ref,
         pc_ref, px_ref, d_ref, gw_ref, gb_ref, yc_ref, yx_ref, h_ref) = refs
    else:
        (uc_ref, ux_ref, bblk_ref, cblk_ref, enr_ref, eni_ref, epr_ref, epi_ref, ac_ref,
         yc_ref, yx_ref, h_ref) = refs
    k = pl.program_id(1)
    t = uc_ref.shape[1]
    gn = enr_ref.shape[1]

    @pl.when(k == 0)
    def _():
        h_ref[...] = jnp.zeros_like(h_ref)

    is_ctx = k < ncc
    u = jnp.where(is_ctx, uc_ref[0], ux_ref[0])
    row = lax.broadcasted_iota(jnp.int32, (t, t), 0)
    col = lax.broadcasted_iota(jnp.int32, (t, t), 1)
    tri = jnp.where((col >= row) if reverse else (col <= row), 1.0, 0.0).astype(BF16)
    last = 0 if reverse else t - 1
    nblk, wb, sb2 = bblk_ref.shape
    sb = sb2 // 2
    ys = []
    for j in range(nblk):
        sc = slice(j * sb, (j + 1) * sb)
        bu = jnp.dot(u[:, j * wb:(j + 1) * wb].astype(BF16), bblk_ref[j], preferred_element_type=F32)
        br, bi = bu[:, :sb], bu[:, sb:]
        enr, eni = enr_ref[:, sc], eni_ref[:, sc]
        z = jnp.concatenate([br * enr - bi * eni, br * eni + bi * enr], axis=1).astype(BF16)
        cs = jnp.dot(tri, z, preferred_element_type=F32)
        hr, hi = h_ref[0:1, sc], h_ref[1:2, sc]
        acr, aci = ac_ref[0:1, sc], ac_ref[1:2, sc]
        sr = cs[:, :sb] + (hr * acr - hi * aci)
        si = cs[:, sb:] + (hr * aci + hi * acr)
        epr, epi = epr_ref[:, sc], epi_ref[:, sc]
        xr = sr * epr - si * epi
        xi = sr * epi + si * epr
        h_ref[0:1, sc] = xr[last:last + 1, :]
        h_ref[1:2, sc] = xi[last:last + 1, :]
        xs = jnp.concatenate([xr, xi], axis=1).astype(BF16)
        ys.append(jnp.dot(xs, cblk_ref[j], preferred_element_type=F32))
    y = ys[0] if nblk == 1 else jnp.concatenate(ys, axis=1)
    if finish:
        y = y + jnp.where(is_ctx, pc_ref[0], px_ref[0]) + d_ref[...] * u
        y = jax.nn.gelu(y)
        gate = jnp.dot(y.astype(BF16), gw_ref[...], preferred_element_type=F32) + gb_ref[...]
        y = y * jax.nn.sigmoid(gate)

    @pl.when(is_ctx)
    def _():
        yc_ref[0] = y.astype(yc_ref.dtype)

    @pl.when(jnp.logical_not(is_ctx))
    def _():
        yx_ref[0] = y.astype(yx_ref.dtype)


def _s5_tables(a_re, a_im, log_step, b_re, b_im, c_re, c_im, t, reverse):
    g, n, h = b_re.shape
    gpb = min(g, max(1, LANE // h))
    nblk = g // gpb
    eye = jnp.eye(gpb, dtype=F32)
    dt = jnp.exp(log_step)[:, None]
    ldr, ldi = a_re * dt, a_im * dt
    mag = jnp.exp(ldr)
    abr, abi = mag * jnp.cos(ldi), mag * jnp.sin(ldi)
    den = a_re * a_re + a_im * a_im
    nr, ni = abr - 1.0, abi
    fr = (nr * a_re + ni * a_im) / den
    fi = (ni * a_re - nr * a_im) / den
    bbr = fr[..., None] * b_re - fi[..., None] * b_im
    bbi = fr[..., None] * b_im + fi[..., None] * b_re
    blk_b = lambda m: jnp.einsum('jgnh,gk->jghkn', m.reshape(nblk, gpb, n, h), eye).reshape(nblk, gpb * h, gpb * n)
    bblk = jnp.concatenate([blk_b(bbr), blk_b(bbi)], axis=2).astype(BF16)
    blk_c = lambda m: jnp.einsum('jghn,gk->jgnkh', m.reshape(nblk, gpb, h, n), eye).reshape(nblk, gpb * n, gpb * h)
    cblk = jnp.concatenate([blk_c(c_re), -blk_c(c_im)], axis=1).astype(BF16)
    centre = float(t // 2)
    pos = jnp.arange(t, dtype=F32)[:, None]
    steps = ((t - pos) if reverse else (pos + 1.0)) - centre
    lr, li = ldr.reshape(1, g * n), ldi.reshape(1, g * n)
    er, ei = steps * lr, steps * li
    epr, epi = jnp.exp(er) * jnp.cos(ei), jnp.exp(er) * jnp.sin(ei)
    enr, eni = jnp.exp(-er) * jnp.cos(ei), -jnp.exp(-er) * jnp.sin(ei)
    ac = jnp.concatenate([jnp.exp(centre * lr) * jnp.cos(centre * li),
                          jnp.exp(centre * lr) * jnp.sin(centre * li)], axis=0)
    return bblk, cblk, (enr, eni, epr, epi, ac)


def _s5_pass(pc, px, col, width, tabs, reverse, fin=None, out_dtype=F32):
    bsz, nctx = pc.shape[:2]
    seq = px.shape[1]
    t = S5_CHUNK
    ncc, ncx = nctx // t, seq // t
    bblk, cblk, (enr, eni, epr, epi, ac) = tabs
    gn = enr.shape[1]
    if reverse:
        cidx = lambda k: jnp.maximum(ncc - 1 - k, 0)
        xidx = lambda k: jnp.minimum(ncx - 1 - (k - ncc), ncx - 1)
    else:
        cidx = lambda k: jnp.minimum(k, ncc - 1)
        xidx = lambda k: jnp.maximum(k - ncc, 0)
    in_specs = [pl.BlockSpec((1, t, width), lambda b, k: (b, cidx(k), col)),
                pl.BlockSpec((1, t, width), lambda b, k: (b, xidx(k), col)),
                _const_spec(bblk.shape), _const_spec(cblk.shape),
                _const_spec(enr.shape), _const_spec(eni.shape), _const_spec(epr.shape), _const_spec(epi.shape),
                _const_spec(ac.shape)]
    args = [pc, px, bblk, cblk, enr, eni, epr, epi, ac]
    if fin is not None:
        prev_c, prev_x, dvec, gw, gb = fin
        in_specs += [pl.BlockSpec((1, t, width), lambda b, k: (b, cidx(k), 0)),
                     pl.BlockSpec((1, t, width), lambda b, k: (b, xidx(k), 0)),
                     _const_spec((1, width)), _const_spec(gw.shape), _const_spec((1, width))]
        args += [prev_c, prev_x, dvec.reshape(1, width), gw, gb.reshape(1, width)]
    return pl.pallas_call(
        functools.partial(_s5_kernel, ncc=ncc, reverse=reverse, finish=fin is not None),
        out_shape=(jax.ShapeDtypeStruct((bsz, nctx, width), out_dtype),
                   jax.ShapeDtypeStruct((bsz, seq, width), out_dtype)),
        grid=(bsz, ncc + ncx),
        in_specs=in_specs,
        out_specs=(pl.BlockSpec((1, t, width), lambda b, k: (b, cidx(k), 0)),
                   pl.BlockSpec((1, t, width), lambda b, k: (b, xidx(k), 0))),
        scratch_shapes=[pltpu.VMEM((2, gn), F32)],
        compiler_params=_cparams("parallel", "arbitrary"),
        name="s5_rev" if reverse else "s5_fwd",
    )(*args)


def s5_mixer(pc, px, width, p):
    tabs = [_s5_tables(p['s5_a_re'][d], p['s5_a_im'][d], p['s5_log_step'][d], p['s5_b_re'][d], p['s5_b_im'][d],
                       p['s5_c_re'][d], p['s5_c_im'][d], S5_CHUNK, d == 1) for d in range(2)]
    bc, bx = _s5_pass(pc, px, 0, width, tabs[1], reverse=True)
    return _s5_pass(pc, px, 0, width, tabs[0], reverse=False,
                    fin=(bc, bx, p['s5_d'], p['s5_glu_w'].astype(BF16), p['s5_glu_b']))


GLA_STEP = 256


def _gla_kernel(*refs, nh, dk, dv, ncs, reverse, finish, scale):
    if finish:
        (kc_ref, qc_ref, vc_ref, lc_ref, kx_ref, qx_ref, vx_ref, lx_ref, wg_ref, bg_ref, pc_ref, px_ref, ng_ref,
         oc_ref, ox_ref, s_ref, o_scr) = refs
    else:
        (kc_ref, qc_ref, vc_ref, lc_ref, kx_ref, qx_ref, vx_ref, lx_ref, wg_ref, bg_ref,
         oc_ref, ox_ref, s_ref, o_scr) = refs
    ts = kc_ref.shape[1]
    nchunk = ts // GLA_CHUNK
    is_ctx = pl.program_id(1) < ncs
    pick = lambda c_ref, x_ref: jnp.where(is_ctx, c_ref[0], x_ref[0])

    @pl.when(pl.program_id(1) == 0)
    def _():
        s_ref[...] = jnp.zeros_like(s_ref)

    hp = lax.Precision.HIGHEST
    rank2 = wg_ref.shape[0]
    pre = jnp.dot(pick(lc_ref, lx_ref)[:, :rank2], wg_ref[...], preferred_element_type=F32,
                  precision=hp) + bg_ref[...]
    g = jax.nn.log_sigmoid(pre) * (1.0 / GLA_GATE_TEMP)
    ci = lax.broadcasted_iota(jnp.int32, (GLA_CHUNK, GLA_CHUNK), 0)
    cj = lax.broadcasted_iota(jnp.int32, (GLA_CHUNK, GLA_CHUNK), 1)
    keep = (cj >= ci) if reverse else (cj <= ci)
    cum = jnp.where(keep, 1.0, 0.0)
    b = jnp.concatenate([jnp.dot(cum, g[c * GLA_CHUNK:(c + 1) * GLA_CHUNK], preferred_element_type=F32, precision=hp)
                         for c in range(nchunk)], axis=0)
    eb = jnp.exp(b)
    enb = jnp.exp(-b)
    q_in = pick(qc_ref, qx_ref) * scale * eb
    k_out = pick(kc_ref, kx_ref) * enb
    v = pick(vc_ref, vx_ref)
    if finish:
        prev = pick(pc_ref, px_ref)
    nt = (((1,), (1,)), ((), ()))
    tn = (((0,), (0,)), ((), ()))
    for c in (range(nchunk - 1, -1, -1) if reverse else range(nchunk)):
        r0 = c * GLA_CHUNK
        rows = slice(r0, r0 + GLA_CHUNK)
        end = r0 if reverse else r0 + GLA_CHUNK - 1
        etot = jnp.exp(b[end:end + 1, :])
        k_kv = k_out[rows] * etot
        for h in range(nh):
            kc = slice(h * dk, (h + 1) * dk)
            vc = slice(h * dv, (h + 1) * dv)
            qh = q_in[rows, kc].astype(BF16)
            kh = k_out[rows, kc].astype(BF16)
            kkv = k_kv[:, kc].astype(BF16)
            vh = v[rows, vc].astype(BF16)
            st = s_ref[h]
            sc = lax.dot_general(qh, kh, nt, preferred_element_type=F32)
            sc = jnp.where(keep, sc, 0.0).astype(BF16)
            o = (jnp.dot(sc, vh, preferred_element_type=F32)
                 + lax.dot_general(qh, st.astype(BF16), nt, preferred_element_type=F32))
            kvt = lax.dot_general(vh, kkv, tn, preferred_element_type=F32)
            s_ref[h] = st * etot[:, kc] + kvt
            if finish:
                o = o + prev[rows, vc]
                ms = jnp.mean(o * o, axis=-1, keepdims=True)
                o = o * lax.rsqrt(ms + EPS) * ng_ref[...]
            o_scr[rows, vc] = o

    @pl.when(is_ctx)
    def _():
        oc_ref[0] = o_scr[...]

    @pl.when(jnp.logical_not(is_ctx))
    def _():
        ox_ref[0] = o_scr[...]


def _gla_pass(pc, gx, c0, wgp, bg, nh, dk, dv, gpad, reverse, fin=None):
    bsz, nctx = pc.shape[:2]
    seq = gx.shape[1]
    ts = GLA_STEP
    ncs, nxs = nctx // ts, seq // ts
    key, val = nh * dk, nh * dv
    if reverse:
        cidx = lambda k: jnp.maximum(ncs - 1 - k, 0)
        xidx = lambda k: jnp.minimum(nxs - 1 - (k - ncs), nxs - 1)
    else:
        cidx = lambda k: jnp.minimum(k, ncs - 1)
        xidx = lambda k: jnp.maximum(k - ncs, 0)

    def group(idx, base):
        return [pl.BlockSpec((1, ts, key), lambda b, k: (b, idx(k), base // key)),
                pl.BlockSpec((1, ts, key), lambda b, k: (b, idx(k), base // key + 1)),
                pl.BlockSpec((1, ts, val), lambda b, k: (b, idx(k), (base + 2 * key) // val)),
                pl.BlockSpec((1, ts, gpad), lambda b, k: (b, idx(k), (base + 2 * key + val) // gpad))]

    in_specs = group(cidx, c0) + group(xidx, 0) + [_const_spec(wgp.shape), _const_spec((1, key))]
    args = [pc] * 4 + [gx] * 4 + [wgp, bg.reshape(1, key)]
    if fin is not None:
        prev_c, prev_x, ng = fin
        in_specs += [pl.BlockSpec((1, ts, val), lambda b, k: (b, cidx(k), 0)),
                     pl.BlockSpec((1, ts, val), lambda b, k: (b, xidx(k), 0)), _const_spec((1, dv))]
        args += [prev_c, prev_x, ng.reshape(1, dv)]
    return pl.pallas_call(
        functools.partial(_gla_kernel, nh=nh, dk=dk, dv=dv, ncs=ncs, reverse=reverse, finish=fin is not None,
                          scale=dk ** -0.5),
        out_shape=(jax.ShapeDtypeStruct((bsz, nctx, val), F32), jax.ShapeDtypeStruct((bsz, seq, val), F32)),
        grid=(bsz, ncs + nxs),
        in_specs=in_specs,
        out_specs=(pl.BlockSpec((1, ts, val), lambda b, k: (b, cidx(k), 0)),
                   pl.BlockSpec((1, ts, val), lambda b, k: (b, xidx(k), 0))),
        scratch_shapes=[pltpu.VMEM((nh, dv, dk), F32), pltpu.VMEM((ts, val), F32)],
        compiler_params=_cparams("parallel", "arbitrary"),
        name="gla_rev" if reverse else "gla_fwd",
    )(*args)


def gla_mixer(pc, px, c0, q4, gpad, p):
    dv = p['gla_norm_g'].shape[0]
    nh = (2 * q4) // dv
    dk = q4 // nh
    rank = p['gla_wg'].shape[1]
    gx = _to_col_major(px[..., c0:])
    wgp = [jnp.zeros((2 * rank, q4), F32).at[d * rank:(d + 1) * rank].set(p['gla_wg'][d]) for d in range(2)]
    kw = dict(nh=nh, dk=dk, dv=dv, gpad=gpad)
    rc, rx = _gla_pass(pc, gx, c0, wgp[1], p['gla_bg'][1], reverse=True, **kw)
    oc, ox = _gla_pass(pc, gx, c0, wgp[0], p['gla_bg'][0], reverse=False, fin=(rc, rx, p['gla_norm_g']), **kw)
    return oc, _from_col_major(ox)


HY_N2 = 128
HY_PAD = 8


def _hy_dims(n):
    n1 = 2 * n // HY_N2
    k1 = n1 // 2 + 1
    k1p = -(-k1 // 8) * 8
    return n1, k1, k1p


def _hy_tables(n):
    n1, k1, k1p = _hy_dims(n)
    big = 2 * n
    kk = jnp.arange(k1p, dtype=jnp.int32)[:, None]
    mm = jnp.arange(n1, dtype=jnp.int32)[None, :]
    ang = (2.0 * math.pi / n1) * ((kk * mm) % n1).astype(F32)
    valid = (kk < k1).astype(F32)
    f1 = jnp.concatenate([jnp.cos(ang) * valid, -jnp.sin(ang) * valid], axis=0)
    wk = jnp.where((kk == 0) | (kk == n1 // 2), 1.0, 2.0) * valid / big
    f1inv = jnp.concatenate([(jnp.cos(ang) * wk).T, (-jnp.sin(ang) * wk).T], axis=1)
    k1i = jnp.arange(k1, dtype=jnp.int32)[:, None, None]
    k2i = jnp.arange(HY_N2, dtype=jnp.int32)[None, :, None]
    n2i = jnp.arange(HY_N2, dtype=jnp.int32)[None, None, :]
    ph = (2.0 * math.pi / big) * ((n2i * (k1i + n1 * k2i)) % big).astype(F32)
    gr, gi = jnp.cos(ph), -jnp.sin(ph)
    gs = jnp.concatenate([gr, gi], axis=1)
    gts = jnp.concatenate([gr.swapaxes(1, 2), gi.swapaxes(1, 2)], axis=1)
    return f1.astype(BF16), f1inv.astype(BF16), gs.astype(BF16), gts.astype(BF16)


def _hy_stage1(src_ref, f1, a_r, a_i, nslab, k1p):
    pitch = HY_N2 + HY_PAD

    def body(i, carry):
        n2 = 2 * i
        rows = jnp.concatenate([src_ref[pl.ds(n2, nslab, stride=pitch), :],
                                src_ref[pl.ds(n2 + 1, nslab, stride=pitch), :]], axis=1)
        out = jnp.dot(f1, rows.astype(BF16), preferred_element_type=F32)
        base = pl.multiple_of(n2 * k1p, 8)
        a_r[pl.ds(base, k1p), :] = out[:k1p, :LANE]
        a_i[pl.ds(base, k1p), :] = out[k1p:, :LANE]
        base1 = pl.multiple_of(base + k1p, 8)
        a_r[pl.ds(base1, k1p), :] = out[:k1p, LANE:]
        a_i[pl.ds(base1, k1p), :] = out[k1p:, LANE:]
        return carry

    lax.fori_loop(0, HY_N2 // 2, body, 0, unroll=8)


def _hy_stage2(a_r, a_i, gs_ref, k, k1p):
    ar = a_r[pl.ds(k, HY_N2, stride=k1p), :]
    ai = a_i[pl.ds(k, HY_N2, stride=k1p), :]
    rhs = jnp.concatenate([ar, ai], axis=1).astype(BF16)
    out = jnp.dot(gs_ref[k], rhs, preferred_element_type=F32)
    h = HY_N2
    return out[:h, :LANE] - out[h:, LANE:], out[:h, LANE:] + out[h:, :LANE]


def _hyena_conv_kernel(y_ref, g_ref, wy_ref, by_ref, wg_ref, bg_ref, bias_ref, hr_ref, hi_ref,
                       f1_ref, f1inv_ref, gs_ref, gts_ref, o_ref, ypad, zbuf, a_r, a_i, *, conv_y):
    n = y_ref.shape[1]
    nslab = n // HY_N2
    pitch = HY_N2 + HY_PAD
    k1 = gs_ref.shape[0]
    k1p = f1_ref.shape[0] // 2
    h = HY_N2

    def slab_conv(ref, i, w_ref, b_ref):
        r0 = pl.multiple_of(i * HY_N2, HY_N2)
        prev8 = ref[0, pl.ds(pl.multiple_of(jnp.maximum(r0 - 8, 0), 8), 8), :]
        next8 = ref[0, pl.ds(pl.multiple_of(jnp.minimum(r0 + HY_N2, n - 8), 8), 8), :]
        return _dwconv3_tile(ref[0, pl.ds(r0, HY_N2), :], prev8, next8, i == 0, i == nslab - 1, w_ref, b_ref)

    def y_slab(i):
        if conv_y:
            return slab_conv(y_ref, i, wy_ref, by_ref)
        return y_ref[0, pl.ds(pl.multiple_of(i * HY_N2, HY_N2), HY_N2), :]

    def fill(i, carry):
        ypad[pl.ds(pl.multiple_of(i * pitch, 8), HY_N2), :] = y_slab(i)
        return carry

    lax.fori_loop(0, nslab, fill, 0, unroll=2)
    _hy_stage1(ypad, f1_ref[...], a_r, a_i, nslab, k1p)

    def freq_fwd(k, carry):
        xr, xi = _hy_stage2(a_r, a_i, gs_ref, k, k1p)
        hr, hi = hr_ref[0, k], hi_ref[0, k]
        zbuf[k] = jnp.concatenate([xr * hr - xi * hi, xr * hi + xi * hr], axis=1).astype(BF16)
        return carry

    lax.fori_loop(0, k1, freq_fwd, 0, unroll=4)

    def freq_inv(k, carry):
        out = jnp.dot(gts_ref[k], zbuf[k], preferred_element_type=F32)
        a_r[pl.ds(k, HY_N2, stride=k1p), :] = out[:h, :LANE] + out[h:, LANE:]
        a_i[pl.ds(k, HY_N2, stride=k1p), :] = out[:h, LANE:] - out[h:, :LANE]
        return carry

    lax.fori_loop(0, k1, freq_inv, 0, unroll=4)
    f1inv = f1inv_ref[...][:nslab]

    def inv1(i, carry):
        n2 = 2 * i
        b0 = pl.multiple_of(n2 * k1p, 8)
        b1 = pl.multiple_of(b0 + k1p, 8)
        rhs = jnp.concatenate(
            [jnp.concatenate([a_r[pl.ds(b0, k1p), :], a_i[pl.ds(b0, k1p), :]], axis=0),
             jnp.concatenate([a_r[pl.ds(b1, k1p), :], a_i[pl.ds(b1, k1p), :]], axis=0)], axis=1).astype(BF16)
        out = jnp.dot(f1inv, rhs, preferred_element_type=F32)
        ypad[pl.ds(n2, nslab, stride=pitch), :] = out[:, :LANE]
        ypad[pl.ds(n2 + 1, nslab, stride=pitch), :] = out[:, LANE:]
        return carry

    lax.fori_loop(0, HY_N2 // 2, inv1, 0, unroll=8)

    def finish(i, carry):
        p0 = pl.multiple_of(i * pitch, 8)
        gate = slab_conv(g_ref, i, wg_ref, bg_ref)
        o_ref[0, pl.ds(pl.multiple_of(i * HY_N2, HY_N2), HY_N2), :] = gate * (ypad[pl.ds(p0, HY_N2), :]
                                                                                + y_slab(i) * bias_ref[0])
        return carry

    lax.fori_loop(0, nslab, finish, 0, unroll=2)


def _hyena_spectrum_kernel(f_ref, f1_ref, gs_ref, hr_ref, hi_ref, a_r, a_i, fpad):
    big = f_ref.shape[1]
    nslab = big // HY_N2
    pitch = HY_N2 + HY_PAD
    k1 = gs_ref.shape[0]
    k1p = f1_ref.shape[0] // 2

    def fill(i, carry):
        fpad[pl.ds(pl.multiple_of(i * pitch, 8), HY_N2), :] = f_ref[0, pl.ds(pl.multiple_of(i * HY_N2, HY_N2), HY_N2), :]
        return carry

    lax.fori_loop(0, nslab, fill, 0, unroll=2)
    _hy_stage1(fpad, f1_ref[...], a_r, a_i, nslab, k1p)

    def freq(k, carry):
        xr, xi = _hy_stage2(a_r, a_i, gs_ref, k, k1p)
        hr_ref[0, k] = xr
        hi_ref[0, k] = xi
        return carry

    lax.fori_loop(0, k1, freq, 0, unroll=4)


def _one(shape, index_map):
    return pl.BlockSpec(shape, index_map, pipeline_mode=pl.Buffered(1))


def hyena_spectrum(filt, tabs):
    f1, _, gs, _ = tabs
    r, big, w = filt.shape
    ns = w // LANE
    n1, k1, k1p = _hy_dims(big // 2)
    shp = jax.ShapeDtypeStruct((r * ns, k1, HY_N2, LANE), F32)
    spec_o = pl.BlockSpec((1, k1, HY_N2, LANE), lambda i, j: (i * ns + j, 0, 0, 0))
    return pl.pallas_call(
        _hyena_spectrum_kernel,
        out_shape=(shp, shp),
        grid=(r, ns),
        in_specs=[_one((1, big, LANE), lambda i, j: (i, 0, j)),
                  _one(f1.shape, lambda i, j: (0, 0)),
                  _one(gs.shape, lambda i, j: (0, 0, 0))],
        out_specs=(spec_o, spec_o),
        scratch_shapes=[pltpu.VMEM((HY_N2 * k1p, LANE), F32), pltpu.VMEM((HY_N2 * k1p, LANE), F32),
                        pltpu.VMEM((n1 * (HY_N2 + HY_PAD), LANE), F32)],
        compiler_params=_cparams("parallel", "parallel"),
        name="hyena_spectrum",
    )(filt, f1, gs)


def hyena_order(y, ycol, g, gcol, conv_w, conv_b, cy, cg, bias, hr, hi, order, tabs, conv_y):
    f1, f1inv, gs, gts = tabs
    bsz, n = y.shape[:2]
    w = bias.shape[-1]
    ns = w // LANE
    n1, k1, k1p = _hy_dims(n)
    f1d = f1[:, :n1 // 2]
    pitch = HY_N2 + HY_PAD
    nslab = n // HY_N2
    cw = lambda c: _one((3, LANE), lambda j, b: (0, c + j))
    cb = lambda c: _one((1, LANE), lambda j, b: (0, c + j))
    hspec = _one((1, k1, HY_N2, LANE), lambda j, b: (order * ns + j, 0, 0, 0))
    return pl.pallas_call(
        functools.partial(_hyena_conv_kernel, conv_y=conv_y),
        out_shape=jax.ShapeDtypeStruct((bsz, n, w), F32),
        grid=(ns, bsz),
        in_specs=[_one((1, n, LANE), lambda j, b: (b, 0, ycol + j)),
                  _one((1, n, LANE), lambda j, b: (b, 0, gcol + j)),
                  cw(cy), cb(cy), cw(cg), cb(cg),
                  _one((1, 1, LANE), lambda j, b: (order, 0, j)),
                  hspec, hspec,
                  _one(f1d.shape, lambda j, b: (0, 0)), _one(f1inv.shape, lambda j, b: (0, 0)),
                  _one(gs.shape, lambda j, b: (0, 0, 0)), _one(gts.shape, lambda j, b: (0, 0, 0))],
        out_specs=pl.BlockSpec((1, n, LANE), lambda j, b: (b, 0, j)),
        scratch_shapes=[pltpu.VMEM((nslab * pitch, LANE), F32), pltpu.VMEM((k1, HY_N2, 2 * LANE), BF16),
                        pltpu.VMEM((HY_N2 * k1p, LANE), F32), pltpu.VMEM((HY_N2 * k1p, LANE), F32)],
        compiler_params=pltpu.CompilerParams(dimension_semantics=("parallel", "parallel"),
                                             vmem_limit_bytes=BIG_VMEM_LIMIT),
        name="hyena_order",
    )(y, g, conv_w, conv_b, conv_w, conv_b, bias.reshape(bias.shape[0], 1, w), hr, hi, f1d, f1inv, gs, gts)


def hyena_latent(px, c0, filt, p):
    w = p['hy_bias'].shape[-1]
    ns = w // LANE
    n = px.shape[1]
    tabs = _hy_tables(n)
    hr, hi = hyena_spectrum(filt, tabs)
    cw, cb = p['hy_conv_w'], p['hy_conv_b'].reshape(1, -1)
    b0 = c0 // LANE
    y1 = hyena_order(px, b0, px, b0 + ns, cw, cb, 0, ns, p['hy_bias'], hr, hi, 0, tabs, True)
    return hyena_order(y1, 0, px, b0 + 2 * ns, cw, cb, 0, 2 * ns, p['hy_bias'], hr, hi, 1, tabs, False)


def _hyena_ctx_kernel(zv_ref, z1_ref, z2_ref, wv_ref, bv_ref, w1_ref, b1_ref, w2_ref, b2_ref, bias_ref,
                      filt_ref, ff_ref, finv_ref, o_ref):
    n = zv_ref.shape[1]
    hp = lax.Precision.HIGHEST
    zero8 = jnp.zeros((8, LANE), F32)
    conv = lambda ref, w, b: _dwconv3_tile(ref[0], zero8, zero8, True, True, w, b)
    ff = ff_ref[...]
    kp = ff.shape[0] // 2
    y = conv(zv_ref, wv_ref, bv_ref)
    for o, (g_ref, w, b) in enumerate(((z1_ref, w1_ref, b1_ref), (z2_ref, w2_ref, b2_ref))):
        hsp = jnp.dot(ff, filt_ref[o], preferred_element_type=F32, precision=hp)
        ysp = jnp.dot(ff[:, :n], y, preferred_element_type=F32, precision=hp)
        hr, hi, yr, yi = hsp[:kp], hsp[kp:], ysp[:kp], ysp[kp:]
        z = jnp.concatenate([yr * hr - yi * hi, yr * hi + yi * hr], axis=0)
        cv = jnp.dot(finv_ref[...], z, preferred_element_type=F32, precision=hp)
        y = conv(g_ref, w, b) * (cv + y * bias_ref[o])
    o_ref[0] = y


def hyena_context(pc, c0, filt, p):
    bsz, n = pc.shape[:2]
    w = p['hy_bias'].shape[-1]
    ns = w // LANE
    kp = -(-(n + 1) // 8) * 8
    kk = jnp.arange(kp, dtype=jnp.int32)[:, None]
    mm = jnp.arange(2 * n, dtype=jnp.int32)[None, :]
    ang = (math.pi / n) * ((kk * mm) % (2 * n)).astype(F32)
    valid = (kk <= n).astype(F32)
    ff = jnp.concatenate([jnp.cos(ang) * valid, -jnp.sin(ang) * valid], axis=0)
    wk = jnp.where((kk == 0) | (kk == n), 1.0, 2.0) * valid / (2 * n)
    finv = jnp.concatenate([(jnp.cos(ang) * wk).T[:n], (-jnp.sin(ang) * wk).T[:n]], axis=1)
    b0 = c0 // LANE
    zs = lambda c: pl.BlockSpec((1, n, LANE), lambda j, b: (b, 0, b0 + c + j))
    cw = lambda c: pl.BlockSpec((3, LANE), lambda j, b: (0, c + j))
    cb = lambda c: pl.BlockSpec((1, LANE), lambda j, b: (0, c + j))
    conv_w, conv_b = p['hy_conv_w'], p['hy_conv_b'].reshape(1, -1)
    return pl.pallas_call(
        _hyena_ctx_kernel,
        out_shape=jax.ShapeDtypeStruct((bsz, n, w), F32),
        grid=(ns, bsz),
        in_specs=[zs(0), zs(ns), zs(2 * ns), cw(0), cb(0), cw(ns), cb(ns), cw(2 * ns), cb(2 * ns),
                  pl.BlockSpec((2, 1, LANE), lambda j, b: (0, 0, j)),
                  pl.BlockSpec((2, 2 * n, LANE), lambda j, b: (0, 0, j)),
                  pl.BlockSpec(ff.shape, lambda j, b: (0, 0)), pl.BlockSpec(finv.shape, lambda j, b: (0, 0))],
        out_specs=pl.BlockSpec((1, n, LANE), lambda j, b: (b, 0, j)),
        compiler_params=_cparams("parallel", "parallel"),
        name="hyena_context",
    )(pc, pc, pc, conv_w, conv_b, conv_w, conv_b, conv_w, conv_b, p['hy_bias'].reshape(2, 1, w), filt, ff, finv)


def _to_col_major(t):
    bsz, n = t.shape[:2]
    rows = n // GRID_W
    return t.reshape(bsz, rows, GRID_W, *t.shape[2:]).swapaxes(1, 2).reshape(bsz, n, *t.shape[2:])


def _from_col_major(t):
    bsz, n = t.shape[:2]
    rows = n // GRID_W
    return t.reshape(bsz, GRID_W, rows, *t.shape[2:]).swapaxes(1, 2).reshape(bsz, n, *t.shape[2:])


def _filter_positions(n):
    r = jnp.arange(2 * n, dtype=jnp.int32)
    return jnp.where(r < n, r, 2 * n - r).astype(F32)


def _filter_feats(n):
    pos = _filter_positions(n)
    t = pos / max(n - 1, 1)
    freqs = jnp.linspace(1e-4, HY_POS_FREQS - 1, HY_POS_FREQS, dtype=F32)
    ang = (2.0 * math.pi / n) * pos[:, None] * freqs[None]
    feats = jnp.concatenate([t[:, None], jnp.cos(ang), -jnp.sin(ang)], axis=-1)
    return jnp.pad(feats, ((0, 0), (0, LANE - feats.shape[1])))


def _filter_hidden_kernel(f_ref, w1_ref, b1_ref, s1_ref, w2_ref, b2_ref, s2_ref, o_ref):
    hp = lax.Precision.HIGHEST
    h = jnp.sin(s1_ref[...] * (jnp.dot(f_ref[...], w1_ref[...], preferred_element_type=F32, precision=hp)
                               + b1_ref[...]))
    o_ref[...] = jnp.sin(s2_ref[...] * (jnp.dot(h, w2_ref[...], preferred_element_type=F32, precision=hp)
                                        + b2_ref[...]))


def _filter_out_kernel(h_ref, wf_ref, wb_ref, bf_ref, bb_ref, rate_ref, o_ref, *, n, chunk):
    nchunk = 2 * n // chunk
    inv_span = 1.0 / max(n - 1, 1)

    def emit(c, acc):
        r0 = pl.multiple_of(c * chunk, chunk)
        past = c < nchunk // 2
        w = jnp.where(past, wf_ref[...], wb_ref[...])
        bias = jnp.where(past, bf_ref[...], bb_ref[...])
        val = jnp.dot(h_ref[pl.ds(r0, chunk), :], w, preferred_element_type=F32,
                      precision=lax.Precision.HIGHEST) + bias
        r = r0 + lax.broadcasted_iota(jnp.int32, (chunk, LANE), 0)
        t = jnp.where(r < n, r, 2 * n - r).astype(F32) * inv_span
        val = jnp.where(r == n, 0.0, val * jnp.exp(-t * rate_ref[...]))
        o_ref[0, pl.ds(r0, chunk), :] = val
        return acc + jnp.sum(jnp.abs(val), axis=0, keepdims=True)

    total = lax.fori_loop(0, nchunk, emit, jnp.zeros((1, LANE), F32))
    scale = 1.0 / (total + EPS)

    def rescale(c, carry):
        r0 = pl.multiple_of(c * chunk, chunk)
        o_ref[0, pl.ds(r0, chunk), :] = o_ref[0, pl.ds(r0, chunk), :] * scale
        return carry

    lax.fori_loop(0, nchunk, rescale, 0)


def hyena_filters(n, p):
    order, width = p['hy_bias'].shape
    ns = width // LANE
    hid = p['hy_f_w2'].shape[0]
    feats = _filter_feats(n)
    padc = lambda a: jnp.pad(a.reshape(1, -1), ((0, 0), (0, LANE - a.shape[-1])))
    w1 = jnp.pad(p['hy_f_w1'], ((0, LANE - p['hy_f_w1'].shape[0]), (0, LANE - hid)))
    w2 = jnp.pad(p['hy_f_w2'], ((0, LANE - hid), (0, LANE - hid)))
    w3 = jnp.pad(p['hy_f_w3'], ((0, LANE - hid), (0, 0)))
    tr = _pick_tile(2 * n, 2048, 8)
    full = lambda i: (0, 0)
    hidden = pl.pallas_call(
        _filter_hidden_kernel,
        out_shape=jax.ShapeDtypeStruct((2 * n, LANE), F32),
        grid=(2 * n // tr,),
        in_specs=[pl.BlockSpec((tr, LANE), lambda i: (i, 0)),
                  pl.BlockSpec((LANE, LANE), full), pl.BlockSpec((1, LANE), full), pl.BlockSpec((1, LANE), full),
                  pl.BlockSpec((LANE, LANE), full), pl.BlockSpec((1, LANE), full), pl.BlockSpec((1, LANE), full)],
        out_specs=pl.BlockSpec((tr, LANE), lambda i: (i, 0)),
        compiler_params=_cparams("parallel"),
        name="hyena_filter_hidden",
    )(feats, w1, padc(p['hy_f_b1']), padc(p['hy_f_freq1']), w2, padc(p['hy_f_b2']), padc(p['hy_f_freq2']))
    rates = jnp.abs(jnp.linspace(math.log(HY_DECAY_TARGET) / HY_LONG_DECAY_PCT,
                                 math.log(HY_DECAY_TARGET) / HY_SHORT_DECAY_PCT, width, dtype=F32)).reshape(1, width)
    b3 = p['hy_f_b3'].reshape(1, -1)
    return pl.pallas_call(
        functools.partial(_filter_out_kernel, n=n, chunk=min(1024, n)),
        out_shape=jax.ShapeDtypeStruct((order, 2 * n, width), F32),
        grid=(order, ns),
        in_specs=[_one((2 * n, LANE), lambda o, j: (0, 0)),
                  pl.BlockSpec((LANE, LANE), lambda o, j: (0, 2 * o * ns + j)),
                  pl.BlockSpec((LANE, LANE), lambda o, j: (0, (2 * o + 1) * ns + j)),
                  pl.BlockSpec((1, LANE), lambda o, j: (0, 2 * o * ns + j)),
                  pl.BlockSpec((1, LANE), lambda o, j: (0, (2 * o + 1) * ns + j)),
                  pl.BlockSpec((1, LANE), lambda o, j: (0, j))],
        out_specs=pl.BlockSpec((1, 2 * n, LANE), lambda o, j: (o, 0, j)),
        compiler_params=_cparams("parallel", "parallel"),
        name="hyena_filter_out",
    )(hidden, w3, w3, b3, b3, rates)


def kernel(x, c, ctx, c_ctx, w_mod, b_mod, norm1_g, norm2_g, w_in, s5_a_re, s5_a_im, s5_log_step, s5_b_re, s5_b_im, s5_c_re, s5_c_im, s5_d, s5_glu_w, s5_glu_b, hy_conv_w, hy_conv_b, hy_f_w1, hy_f_b1, hy_f_freq1, hy_f_w2, hy_f_b2, hy_f_freq2, hy_f_w3, hy_f_b3, hy_bias, gla_wg, gla_bg, gla_norm_g, w_branch, w_out, ff_w_up, ff_conv_w, ff_conv_b, ff_w_down, final_norm_g):
    bsz, seq, d = x.shape
    nctx = ctx.shape[1]
    depth = w_mod.shape[0]
    q4 = d // 4
    rank2 = 2 * gla_wg.shape[2]
    gpad = max(LANE, q4 // 2)
    assert bsz + 1 <= 8

    c_gk = q4
    c_gv = 2 * q4
    c_gg = 4 * q4
    c_gq = c_gg + rank2
    c_gr = c_gq + q4
    c_hy = c_gr + 2 * q4
    c_mg = c_hy + 3 * q4

    rows = jnp.zeros((8, d), F32).at[:bsz].set(c).at[bsz].set(c_ctx)
    mod = modulation(rows, w_mod, b_mod)

    xs = x.reshape(bsz * seq, d)
    cs = ctx.reshape(bsz * nctx, d)
    for l in range(depth):
        ctx_out = l < depth - 1
        sh1, s1, g1, sh2, s2, g2 = [mod[l, :, i * d:(i + 1) * d][:, None, :] for i in range(6)]
        gs1 = norm1_g[l] * (1.0 + s1)
        gs2 = norm2_g[l] * (1.0 + s2)
        bx = slice(0, bsz)
        bc = slice(bsz, bsz + 1)

        wl = w_in[l]
        w_pack = jnp.concatenate([
            wl[:, 0:c_gk], wl[:, c_hy:c_mg], wl[:, c_gr:c_hy], wl[:, c_gk:c_gv], wl[:, c_gq:c_gr],
            wl[:, c_gv:c_gg], wl[:, c_gg:c_gq],
            jnp.zeros((d, gpad - rank2), F32)], axis=1).astype(BF16)
        w_gate = wl[:, c_mg:].reshape(d, 3, d).swapaxes(0, 1).astype(BF16)
        wb = w_branch[l].astype(BF16)
        w_o = w_out[l].astype(BF16)
        w_up = cast_col_blocks(ff_w_up[l], _pick_tile(ff_w_down.shape[1], FFN_TF))
        w_dn = ff_w_down[l].astype(BF16)

        p = dict(s5_a_re=s5_a_re[l], s5_a_im=s5_a_im[l], s5_log_step=s5_log_step[l],
                 s5_b_re=s5_b_re[l], s5_b_im=s5_b_im[l], s5_c_re=s5_c_re[l], s5_c_im=s5_c_im[l],
                 s5_d=s5_d[l], s5_glu_w=s5_glu_w[l], s5_glu_b=s5_glu_b[l],
                 hy_conv_w=hy_conv_w[l], hy_conv_b=hy_conv_b[l], hy_f_w1=hy_f_w1[l], hy_f_b1=hy_f_b1[l],
                 hy_f_freq1=hy_f_freq1[l], hy_f_w2=hy_f_w2[l], hy_f_b2=hy_f_b2[l],
                 hy_f_freq2=hy_f_freq2[l], hy_f_w3=hy_f_w3[l], hy_f_b3=hy_f_b3[l], hy_bias=hy_bias[l],
                 gla_wg=gla_wg[l], gla_bg=gla_bg[l], gla_norm_g=gla_norm_g[l])

        px = norm_matmul(xs, gs1[bx], sh1[bx], w_pack).reshape(bsz, seq, -1)
        pc = norm_matmul(cs, gs1[bc], sh1[bc], w_pack).reshape(bsz, nctx, -1)

        ya_c, ya_x = s5_mixer(pc, px, q4, p)
        oc_c, oc_x = gla_mixer(pc, px, 6 * q4, q4, gpad, p)
        yb_x = hyena_latent(px, q4, hyena_filters(seq, p), p)

        flat = lambda t: t.reshape(-1, t.shape[-1])
        mx = merge(xs, gs1[bx], sh1[bx], flat(ya_x), flat(yb_x), flat(oc_x), flat(px), 2, w_gate, wb)
        xs = matmul_residual(mx, w_o, xs, g1[bx])
        xs = ffn(xs, bsz, gs2[bx], sh2[bx], w_up, ff_conv_w[l], ff_conv_b[l], w_dn, g2[bx])

        if ctx_out:
            yb_c = hyena_context(pc, q4, hyena_filters(nctx, p), p)
            mc = merge(cs, gs1[bc], sh1[bc], flat(ya_c), flat(yb_c), flat(oc_c), flat(pc), 2, w_gate, wb)
            cs = matmul_residual(mc, w_o, cs, g1[bc])
            cs = ffn(cs, bsz, gs2[bc], sh2[bc], w_up, ff_conv_w[l], ff_conv_b[l], w_dn, g2[bc])

    return final_rmsnorm(xs, final_norm_g).reshape(bsz, seq, d)
```

```python
import functools
import math

import jax
import jax.numpy as jnp
from jax import lax
from jax.experimental import pallas as pl
from jax.experimental.pallas import tpu as pltpu

F32 = jnp.float32
BF16 = jnp.bfloat16
EPS = 1e-6
GRID_W = 64
S5_GROUP = 16
GLA_CHUNK = 64
GLA_GATE_TEMP = 16.0
HY_POS_FREQS = 16
HY_DECAY_TARGET = 1e-2
HY_SHORT_DECAY_PCT = 0.3
HY_LONG_DECAY_PCT = 1.5
LANE = 128
VMEM_LIMIT = 56 * 1024 * 1024
BIG_VMEM_LIMIT = 60 * 1024 * 1024
FFN_TF = 512


def _pick_tile(n, cap, mult=LANE):
    best = None
    for t in range(mult, min(n, cap) + 1, mult):
        if n % t == 0:
            best = t
    assert best is not None, (n, cap, mult)
    return best


def _col_blocks(w, tn):
    nj = w.shape[-1] // tn
    return jnp.moveaxis(w.reshape(*w.shape[:-1], nj, tn), -2, 0)


def _cast_blocks_kernel(w_ref, o_ref):
    o_ref[0] = w_ref[...].astype(o_ref.dtype)


def cast_col_blocks(w, tn):
    k, n = w.shape
    return pl.pallas_call(
        _cast_blocks_kernel,
        out_shape=jax.ShapeDtypeStruct((n // tn, k, tn), BF16),
        grid=(n // tn,),
        in_specs=[pl.BlockSpec((k, tn), lambda j: (0, j))],
        out_specs=pl.BlockSpec((1, k, tn), lambda j: (j, 0, 0)),
        compiler_params=pltpu.CompilerParams(dimension_semantics=("parallel",), vmem_limit_bytes=VMEM_LIMIT),
        name="cast_col_blocks",
    )(w)


def _cparams(*sem):
    return pltpu.CompilerParams(dimension_semantics=sem, vmem_limit_bytes=VMEM_LIMIT)


def _mod_kernel(r_ref, w_ref, b_ref, o_ref):
    r = r_ref[...]
    s = r * jax.nn.sigmoid(r)
    o_ref[0] = jnp.dot(s, w_ref[0], preferred_element_type=F32,
                       precision=lax.Precision.HIGHEST) + b_ref[0]


def modulation(rows, w_mod, b_mod):
    depth, d, n = w_mod.shape
    tn = _pick_tile(n, 1024)
    return pl.pallas_call(
        _mod_kernel,
        out_shape=jax.ShapeDtypeStruct((depth, 8, n), F32),
        grid=(depth, n // tn),
        in_specs=[pl.BlockSpec((8, d), lambda l, j: (0, 0)),
                  pl.BlockSpec((1, d, tn), lambda l, j: (l, 0, j)),
                  pl.BlockSpec((1, 1, tn), lambda l, j: (l, 0, j))],
        out_specs=pl.BlockSpec((1, 8, tn), lambda l, j: (l, 0, j)),
        compiler_params=_cparams("parallel", "parallel"),
        name="modulation",
    )(rows, w_mod, b_mod.reshape(depth, 1, n))


def _norm_mm_kernel(x_ref, gs_ref, sh_ref, w_ref, o_ref, h_ref):
    @pl.when(pl.program_id(1) == 0)
    def _():
        x = x_ref[...]
        ms = jnp.mean(x * x, axis=-1, keepdims=True)
        h_ref[...] = (x * lax.rsqrt(ms + EPS) * gs_ref[0] + sh_ref[0]).astype(BF16)

    o_ref[...] = jnp.dot(h_ref[...], w_ref[0], preferred_element_type=F32).astype(o_ref.dtype)


def norm_matmul(x, gs, sh, w, out_dtype=F32):
    m, d = x.shape
    n = w.shape[1]
    nb = gs.shape[0]
    tm = _pick_tile(m // nb, 1024, 8)
    tn = _pick_tile(n, 2048)
    tpb = (m // nb) // tm
    return pl.pallas_call(
        _norm_mm_kernel,
        out_shape=jax.ShapeDtypeStruct((m, n), out_dtype),
        grid=(m // tm, n // tn),
        in_specs=[pl.BlockSpec((tm, d), lambda i, j: (i, 0)),
                  pl.BlockSpec((1, 1, d), lambda i, j: (i // tpb, 0, 0)),
                  pl.BlockSpec((1, 1, d), lambda i, j: (i // tpb, 0, 0)),
                  pl.BlockSpec((1, d, tn), lambda i, j: (j, 0, 0))],
        out_specs=pl.BlockSpec((tm, tn), lambda i, j: (i, j)),
        scratch_shapes=[pltpu.VMEM((tm, d), BF16)],
        compiler_params=_cparams("parallel", "arbitrary"),
        name="norm_matmul",
    )(x, gs, sh, _col_blocks(w, tn))


def _mm_res_kernel(a_ref, w_ref, r_ref, g_ref, o_ref):
    acc = jnp.dot(a_ref[...], w_ref[0], preferred_element_type=F32)
    o_ref[...] = r_ref[...] + g_ref[0] * acc


def matmul_residual(a, w, res, g):
    m, k = a.shape
    n = w.shape[1]
    nb = g.shape[0]
    tm = _pick_tile(m // nb, 1024, 8)
    tn = _pick_tile(n, 2048)
    tpb = (m // nb) // tm
    resident = pl.Buffered(1) if n == tn else None
    return pl.pallas_call(
        _mm_res_kernel,
        out_shape=jax.ShapeDtypeStruct((m, n), F32),
        grid=(m // tm, n // tn),
        in_specs=[pl.BlockSpec((tm, k), lambda i, j: (i, 0)),
                  pl.BlockSpec((1, k, tn), lambda i, j: (j, 0, 0), pipeline_mode=resident),
                  pl.BlockSpec((tm, tn), lambda i, j: (i, j)),
                  pl.BlockSpec((1, 1, tn), lambda i, j: (i // tpb, 0, j))],
        out_specs=pl.BlockSpec((tm, tn), lambda i, j: (i, j)),
        compiler_params=_cparams("parallel", "arbitrary"),
        name="matmul_residual",
    )(a, _col_blocks(w, tn), res, g)


def _merge_kernel(x_ref, gs_ref, sh_ref, ya_ref, yb_ref, yc_ref, wg_ref, wb_ref, o_ref, h_ref):
    @pl.when(pl.program_id(1) == 0)
    def _():
        x = x_ref[...]
        ms = jnp.mean(x * x, axis=-1, keepdims=True)
        h_ref[...] = (x * lax.rsqrt(ms + EPS) * gs_ref[0] + sh_ref[0]).astype(BF16)

    h = h_ref[...]
    m = None
    lo = 0
    for i, y_ref in enumerate((ya_ref, yb_ref, yc_ref)):
        hi = lo + y_ref.shape[1]
        gate = jax.nn.sigmoid(jnp.dot(h, wg_ref[0, i], preferred_element_type=F32))
        br = jnp.dot(y_ref[...], wb_ref[0, lo:hi, :], preferred_element_type=F32)
        m = gate * br if m is None else m + gate * br
        lo = hi
    o_ref[...] = m.astype(o_ref.dtype)


def merge(x, gs, sh, ya, yb, yc, wg, wb):
    m, d = x.shape
    nb = gs.shape[0]
    tm = _pick_tile(m // nb, 1024, 8)
    tn = _pick_tile(d, 256)
    tpb = (m // nb) // tm
    row = lambda i, j: (i, 0)
    wy = wb.shape[0]
    return pl.pallas_call(
        _merge_kernel,
        out_shape=jax.ShapeDtypeStruct((m, d), BF16),
        grid=(m // tm, d // tn),
        in_specs=[pl.BlockSpec((tm, d), row),
                  pl.BlockSpec((1, 1, d), lambda i, j: (i // tpb, 0, 0)),
                  pl.BlockSpec((1, 1, d), lambda i, j: (i // tpb, 0, 0)),
                  pl.BlockSpec((tm, ya.shape[1]), row),
                  pl.BlockSpec((tm, yb.shape[1]), row),
                  pl.BlockSpec((tm, yc.shape[1]), row),
                  pl.BlockSpec((1, 3, d, tn), lambda i, j: (j, 0, 0, 0)),
                  pl.BlockSpec((1, wy, tn), lambda i, j: (j, 0, 0))],
        out_specs=pl.BlockSpec((tm, tn), lambda i, j: (i, j)),
        scratch_shapes=[pltpu.VMEM((tm, d), BF16)],
        compiler_params=_cparams("parallel", "arbitrary"),
        name="merge",
    )(x, gs, sh, ya, yb, yc, _col_blocks(wg, tn), _col_blocks(wb, tn))


def _dwconv3_tile(a, prev8, next8, first, last, w_ref, cb_ref):
    tr = a.shape[0]
    row = lax.broadcasted_iota(jnp.int32, a.shape, 0)
    before = jnp.where(first, 0.0, prev8[7:8, :])
    after = jnp.where(last, 0.0, next8[0:1, :])
    prev = jnp.where(row == 0, before, pltpu.roll(a, 1, 0))
    nxt = jnp.where(row == tr - 1, after, pltpu.roll(a, tr - 1, 0))
    return prev * w_ref[0:1, :] + a * w_ref[1:2, :] + nxt * w_ref[2:3, :] + cb_ref[...]


def _ffn_kernel(x_ref, xp_ref, xn_ref, gs_ref, sh_ref, wa_ref, wb_ref, cw_ref, cb_ref, wd_ref, g_ref,
                o_ref, h_ref, hh_ref, *, tpb):
    i, f = pl.program_id(0), pl.program_id(1)

    def normed(x):
        ms = jnp.mean(x * x, axis=-1, keepdims=True)
        return (x * lax.rsqrt(ms + EPS) * gs_ref[0] + sh_ref[0]).astype(BF16)

    @pl.when(f == 0)
    def _():
        h_ref[...] = normed(x_ref[...])
        hh_ref[0:8, :] = normed(xp_ref[...])
        hh_ref[8:16, :] = normed(xn_ref[...])
        o_ref[...] = jnp.zeros_like(o_ref)

    wa = wa_ref[0]
    a = jnp.dot(h_ref[...], wa, preferred_element_type=F32)
    ah = jnp.dot(hh_ref[...], wa, preferred_element_type=F32)
    b = jnp.dot(h_ref[...], wb_ref[0], preferred_element_type=F32)
    pos = i % tpb
    conv = _dwconv3_tile(a, ah[0:8], ah[8:16], pos == 0, pos == tpb - 1, cw_ref, cb_ref)
    act = (conv * jax.nn.sigmoid(conv) * b).astype(BF16)
    half = o_ref.shape[1] // 2
    for lo in (0, half):
        o_ref[:, lo:lo + half] += jnp.dot(act, wd_ref[:, lo:lo + half], preferred_element_type=F32)

    @pl.when(f == pl.num_programs(1) - 1)
    def _():
        o_ref[...] = x_ref[...] + g_ref[0] * o_ref[...]


def ffn(x, nseq, gs, sh, w_up, conv_w, conv_b, w_dn, g):
    m, d = x.shape
    fh = w_dn.shape[0]
    nb = gs.shape[0]
    tm = _pick_tile(m // nseq, 1024, 8)
    tf = w_up.shape[2]
    tpb = (m // nseq) // tm
    tpm = (m // nb) // tm
    nf = fh // tf
    r8 = tm // 8
    last8 = m // 8 - 1
    mod = lambda i, f: (i // tpm, 0, 0)
    return pl.pallas_call(
        functools.partial(_ffn_kernel, tpb=tpb),
        out_shape=jax.ShapeDtypeStruct((m, d), F32),
        grid=(m // tm, nf),
        in_specs=[pl.BlockSpec((tm, d), lambda i, f: (i, 0), pipeline_mode=pl.Buffered(1)),
                  pl.BlockSpec((8, d), lambda i, f: (jnp.maximum(i * r8 - 1, 0), 0)),
                  pl.BlockSpec((8, d), lambda i, f: (jnp.minimum((i + 1) * r8, last8), 0)),
                  pl.BlockSpec((1, 1, d), mod), pl.BlockSpec((1, 1, d), mod),
                  pl.BlockSpec((1, d, tf), lambda i, f: (f, 0, 0)),
                  pl.BlockSpec((1, d, tf), lambda i, f: (f + nf, 0, 0)),
                  pl.BlockSpec((3, tf), lambda i, f: (0, f)),
                  pl.BlockSpec((1, tf), lambda i, f: (0, f)),
                  pl.BlockSpec((tf, d), lambda i, f: (f, 0)),
                  pl.BlockSpec((1, 1, d), mod)],
        out_specs=pl.BlockSpec((tm, d), lambda i, f: (i, 0)),
        scratch_shapes=[pltpu.VMEM((tm, d), BF16), pltpu.VMEM((16, d), BF16)],
        compiler_params=pltpu.CompilerParams(dimension_semantics=("parallel", "arbitrary"),
                                             vmem_limit_bytes=BIG_VMEM_LIMIT),
        name="ffn",
    )(x, x, x, gs, sh, w_up, w_up, conv_w, conv_b.reshape(1, fh), w_dn, g)


def _rmsnorm_kernel(x_ref, g_ref, o_ref):
    x = x_ref[...]
    ms = jnp.mean(x * x, axis=-1, keepdims=True)
    o_ref[...] = x * lax.rsqrt(ms + EPS) * g_ref[...]


def final_rmsnorm(x, g):
    m, d = x.shape
    tm = _pick_tile(m, 1024, 8)
    return pl.pallas_call(
        _rmsnorm_kernel,
        out_shape=jax.ShapeDtypeStruct((m, d), F32),
        grid=(m // tm,),
        in_specs=[pl.BlockSpec((tm, d), lambda i: (i, 0)), pl.BlockSpec((1, d), lambda i: (0, 0))],
        out_specs=pl.BlockSpec((tm, d), lambda i: (i, 0)),
        compiler_params=_cparams("parallel"),
        name="final_rmsnorm",
    )(x, g.reshape(1, d))


S5_CHUNK = 256


def _const_spec(shape):
    zeros = (0,) * len(shape)
    return pl.BlockSpec(shape, lambda b, k: zeros, pipeline_mode=pl.Buffered(1))


def _s5_kernel(*refs, ncc, reverse, finish):
    if finish:
        (uc_ref, ux_ref, bblk_ref, cblk_ref, enr_ref, eni_ref, epr_ref, epi_ref, ac_ref,
         pc_ref, px_ref, d_ref, gw_ref, gb_ref, yc_ref, yx_ref, h_ref) = refs
    else:
        (uc_ref, ux_ref, bblk_ref, cblk_ref, enr_ref, eni_ref, epr_ref, epi_ref, ac_ref,
         yc_ref, yx_ref, h_ref) = refs
    k = pl.program_id(1)
    t = uc_ref.shape[1]
    gn = enr_ref.shape[1]

    @pl.when(k == 0)
    def _():
        h_ref[...] = jnp.zeros_like(h_ref)

    is_ctx = k < ncc
    u = jnp.where(is_ctx, uc_ref[0], ux_ref[0])
    row = lax.broadcasted_iota(jnp.int32, (t, t), 0)
    col = lax.broadcasted_iota(jnp.int32, (t, t), 1)
    tri = jnp.where((col >= row) if reverse else (col <= row), 1.0, 0.0).astype(BF16)
    last = 0 if reverse else t - 1
    nblk, wb, sb2 = bblk_ref.shape
    sb = sb2 // 2
    ys = []
    for j in range(nblk):
        sc = slice(j * sb, (j + 1) * sb)
        bu = jnp.dot(u[:, j * wb:(j + 1) * wb].astype(BF16), bblk_ref[j], preferred_element_type=F32)
        br, bi = bu[:, :sb], bu[:, sb:]
        enr, eni = enr_ref[:, sc], eni_ref[:, sc]
        z = jnp.concatenate([br * enr - bi * eni, br * eni + bi * enr], axis=1).astype(BF16)
        cs = jnp.dot(tri, z, preferred_element_type=F32)
        hr, hi = h_ref[0:1, sc], h_ref[1:2, sc]
        acr, aci = ac_ref[0:1, sc], ac_ref[1:2, sc]
        sr = cs[:, :sb] + (hr * acr - hi * aci)
        si = cs[:, sb:] + (hr * aci + hi * acr)
        epr, epi = epr_ref[:, sc], epi_ref[:, sc]
        xr = sr * epr - si * epi
        xi = sr * epi + si * epr
        h_ref[0:1, sc] = xr[last:last + 1, :]
        h_ref[1:2, sc] = xi[last:last + 1, :]
        xs = jnp.concatenate([xr, xi], axis=1).astype(BF16)
        ys.append(jnp.dot(xs, cblk_ref[j], preferred_element_type=F32))
    y = ys[0] if nblk == 1 else jnp.concatenate(ys, axis=1)
    if finish:
        y = y + jnp.where(is_ctx, pc_ref[0], px_ref[0]) + d_ref[...] * u
        y = jax.nn.gelu(y)
        gate = jnp.dot(y.astype(BF16), gw_ref[...], preferred_element_type=F32) + gb_ref[...]
        y = y * jax.nn.sigmoid(gate)

    @pl.when(is_ctx)
    def _():
        yc_ref[0] = y.astype(yc_ref.dtype)

    @pl.when(jnp.logical_not(is_ctx))
    def _():
        yx_ref[0] = y.astype(yx_ref.dtype)


def _s5_tables(a_re, a_im, log_step, b_re, b_im, c_re, c_im, t, reverse):
    g, n, h = b_re.shape
    gpb = min(g, max(1, LANE // h))
    nblk = g // gpb
    eye = jnp.eye(gpb, dtype=F32)
    dt = jnp.exp(log_step)[:, None]
    ldr, ldi = a_re * dt, a_im * dt
    mag = jnp.exp(ldr)
    abr, abi = mag * jnp.cos(ldi), mag * jnp.sin(ldi)
    den = a_re * a_re + a_im * a_im
    nr, ni = abr - 1.0, abi
    fr = (nr * a_re + ni * a_im) / den
    fi = (ni * a_re - nr * a_im) / den
    bbr = fr[..., None] * b_re - fi[..., None] * b_im
    bbi = fr[..., None] * b_im + fi[..., None] * b_re
    blk_b = lambda m: jnp.einsum('jgnh,gk->jghkn', m.reshape(nblk, gpb, n, h), eye).reshape(nblk, gpb * h, gpb * n)
    bblk = jnp.concatenate([blk_b(bbr), blk_b(bbi)], axis=2).astype(BF16)
    blk_c = lambda m: jnp.einsum('jghn,gk->jgnkh', m.reshape(nblk, gpb, h, n), eye).reshape(nblk, gpb * n, gpb * h)
    cblk = jnp.concatenate([blk_c(c_re), -blk_c(c_im)], axis=1).astype(BF16)
    centre = float(t // 2)
    pos = jnp.arange(t, dtype=F32)[:, None]
    steps = ((t - pos) if reverse else (pos + 1.0)) - centre
    lr, li = ldr.reshape(1, g * n), ldi.reshape(1, g * n)
    er, ei = steps * lr, steps * li
    epr, epi = jnp.exp(er) * jnp.cos(ei), jnp.exp(er) * jnp.sin(ei)
    enr, eni = jnp.exp(-er) * jnp.cos(ei), -jnp.exp(-er) * jnp.sin(ei)
    ac = jnp.concatenate([jnp.exp(centre * lr) * jnp.cos(centre * li),
                          jnp.exp(centre * lr) * jnp.sin(centre * li)], axis=0)
    return bblk, cblk, (enr, eni, epr, epi, ac)


def _s5_pass(pc, px, col, width, tabs, reverse, fin=None, out_dtype=F32):
    bsz, nctx = pc.shape[:2]
    seq = px.shape[1]
    t = S5_CHUNK
    ncc, ncx = nctx // t, seq // t
    bblk, cblk, (enr, eni, epr, epi, ac) = tabs
    gn = enr.shape[1]
    if reverse:
        cidx = lambda k: jnp.maximum(ncc - 1 - k, 0)
        xidx = lambda k: jnp.minimum(ncx - 1 - (k - ncc), ncx - 1)
    else:
        cidx = lambda k: jnp.minimum(k, ncc - 1)
        xidx = lambda k: jnp.maximum(k - ncc, 0)
    in_specs = [pl.BlockSpec((1, t, width), lambda b, k: (b, cidx(k), col)),
                pl.BlockSpec((1, t, width), lambda b, k: (b, xidx(k), col)),
                _const_spec(bblk.shape), _const_spec(cblk.shape),
                _const_spec(enr.shape), _const_spec(eni.shape), _const_spec(epr.shape), _const_spec(epi.shape),
                _const_spec(ac.shape)]
    args = [pc, px, bblk, cblk, enr, eni, epr, epi, ac]
    if fin is not None:
        prev_c, prev_x, dvec, gw, gb = fin
        in_specs += [pl.BlockSpec((1, t, width), lambda b, k: (b, cidx(k), 0)),
                     pl.BlockSpec((1, t, width), lambda b, k: (b, xidx(k), 0)),
                     _const_spec((1, width)), _const_spec(gw.shape), _const_spec((1, width))]
        args += [prev_c, prev_x, dvec.reshape(1, width), gw, gb.reshape(1, width)]
    return pl.pallas_call(
        functools.partial(_s5_kernel, ncc=ncc, reverse=reverse, finish=fin is not None),
        out_shape=(jax.ShapeDtypeStruct((bsz, nctx, width), out_dtype),
                   jax.ShapeDtypeStruct((bsz, seq, width), out_dtype)),
        grid=(bsz, ncc + ncx),
        in_specs=in_specs,
        out_specs=(pl.BlockSpec((1, t, width), lambda b, k: (b, cidx(k), 0)),
                   pl.BlockSpec((1, t, width), lambda b, k: (b, xidx(k), 0))),
        scratch_shapes=[pltpu.VMEM((2, gn), F32)],
        compiler_params=_cparams("parallel", "arbitrary"),
        name="s5_rev" if reverse else "s5_fwd",
    )(*args)


def s5_mixer(pc, px, width, p):
    tabs = [_s5_tables(p['s5_a_re'][d], p['s5_a_im'][d], p['s5_log_step'][d], p['s5_b_re'][d], p['s5_b_im'][d],
                       p['s5_c_re'][d], p['s5_c_im'][d], S5_CHUNK, d == 1) for d in range(2)]
    bc, bx = _s5_pass(pc, px, 0, width, tabs[1], reverse=True)
    return _s5_pass(pc, px, 0, width, tabs[0], reverse=False,
                    fin=(bc, bx, p['s5_d'], p['s5_glu_w'].astype(BF16), p['s5_glu_b']), out_dtype=BF16)


GLA_STEP = 256


def _gla_kernel(*refs, nh, dk, dv, ncs, reverse, finish, scale):
    if finish:
        (kc_ref, qc_ref, vc_ref, lc_ref, kx_ref, qx_ref, vx_ref, lx_ref, wg_ref, bg_ref, pc_ref, px_ref, ng_ref,
         rc_ref, rx_ref, oc_ref, ox_ref, s_ref, o_scr) = refs
    else:
        (kc_ref, qc_ref, vc_ref, lc_ref, kx_ref, qx_ref, vx_ref, lx_ref, wg_ref, bg_ref,
         oc_ref, ox_ref, s_ref, o_scr) = refs
    ts = kc_ref.shape[1]
    nchunk = ts // GLA_CHUNK
    is_ctx = pl.program_id(1) < ncs
    pick = lambda c_ref, x_ref: jnp.where(is_ctx, c_ref[0], x_ref[0])

    @pl.when(pl.program_id(1) == 0)
    def _():
        s_ref[...] = jnp.zeros_like(s_ref)

    hp = lax.Precision.HIGHEST
    rank2 = wg_ref.shape[0]
    pre = jnp.dot(pick(lc_ref, lx_ref)[:, :rank2], wg_ref[...], preferred_element_type=F32,
                  precision=hp) + bg_ref[...]
    g = jax.nn.log_sigmoid(pre) * (1.0 / GLA_GATE_TEMP)
    ci = lax.broadcasted_iota(jnp.int32, (GLA_CHUNK, GLA_CHUNK), 0)
    cj = lax.broadcasted_iota(jnp.int32, (GLA_CHUNK, GLA_CHUNK), 1)
    keep = (cj >= ci) if reverse else (cj <= ci)
    cum = jnp.where(keep, 1.0, 0.0)
    b = jnp.concatenate([jnp.dot(cum, g[c * GLA_CHUNK:(c + 1) * GLA_CHUNK], preferred_element_type=F32, precision=hp)
                         for c in range(nchunk)], axis=0)
    eb = jnp.exp(b)
    enb = jnp.exp(-b)
    q_in = pick(qc_ref, qx_ref) * scale * eb
    k_out = pick(kc_ref, kx_ref) * enb
    v = pick(vc_ref, vx_ref)
    if finish:
        prev = pick(pc_ref, px_ref)
        r = pick(rc_ref, rx_ref)
        out_gate = r * jax.nn.sigmoid(r)
    nt = (((1,), (1,)), ((), ()))
    tn = (((0,), (0,)), ((), ()))
    for c in (range(nchunk - 1, -1, -1) if reverse else range(nchunk)):
        r0 = c * GLA_CHUNK
        rows = slice(r0, r0 + GLA_CHUNK)
        end = r0 if reverse else r0 + GLA_CHUNK - 1
        etot = jnp.exp(b[end:end + 1, :])
        k_kv = k_out[rows] * etot
        for h in range(nh):
            kc = slice(h * dk, (h + 1) * dk)
            vc = slice(h * dv, (h + 1) * dv)
            qh = q_in[rows, kc].astype(BF16)
            kh = k_out[rows, kc].astype(BF16)
            kkv = k_kv[:, kc].astype(BF16)
            vh = v[rows, vc].astype(BF16)
            st = s_ref[h]
            sc = lax.dot_general(qh, kh, nt, preferred_element_type=F32)
            sc = jnp.where(keep, sc, 0.0).astype(BF16)
            o = (jnp.dot(sc, vh, preferred_element_type=F32)
                 + lax.dot_general(qh, st.astype(BF16), nt, preferred_element_type=F32))
            kvt = lax.dot_general(vh, kkv, tn, preferred_element_type=F32)
            s_ref[h] = st * etot[:, kc] + kvt
            if finish:
                o = o + prev[rows, vc]
                ms = jnp.mean(o * o, axis=-1, keepdims=True)
                o = o * lax.rsqrt(ms + EPS) * ng_ref[...] * out_gate[rows, vc]
            o_scr[rows, vc] = o

    @pl.when(is_ctx)
    def _():
        oc_ref[0] = o_scr[...].astype(oc_ref.dtype)

    @pl.when(jnp.logical_not(is_ctx))
    def _():
        ox_ref[0] = o_scr[...].astype(ox_ref.dtype)


def _gla_pass(pc, gx, c0, xbase, wgp, bg, nh, dk, dv, gpad, reverse, fin=None):
    bsz, nctx = pc.shape[:2]
    seq = gx.shape[1]
    ts = GLA_STEP
    ncs, nxs = nctx // ts, seq // ts
    key, val = nh * dk, nh * dv
    if reverse:
        cidx = lambda k: jnp.maximum(ncs - 1 - k, 0)
        xidx = lambda k: jnp.minimum(nxs - 1 - (k - ncs), nxs - 1)
    else:
        cidx = lambda k: jnp.minimum(k, ncs - 1)
        xidx = lambda k: jnp.maximum(k - ncs, 0)

    def group(idx, base):
        return [pl.BlockSpec((1, ts, key), lambda b, k: (b, idx(k), base // key)),
                pl.BlockSpec((1, ts, key), lambda b, k: (b, idx(k), base // key + 1)),
                pl.BlockSpec((1, ts, val), lambda b, k: (b, idx(k), (base + 2 * key) // val)),
                pl.BlockSpec((1, ts, gpad), lambda b, k: (b, idx(k), (base + 2 * key + val) // gpad))]

    out_dtype = F32 if fin is None else BF16
    in_specs = group(cidx, c0) + group(xidx, xbase) + [_const_spec(wgp.shape), _const_spec((1, key))]
    args = [pc] * 4 + [gx] * 4 + [wgp, bg.reshape(1, key)]
    if fin is not None:
        prev_c, prev_x, ng = fin
        in_specs += [pl.BlockSpec((1, ts, val), lambda b, k: (b, cidx(k), 0)),
                     pl.BlockSpec((1, ts, val), lambda b, k: (b, xidx(k), 0)), _const_spec((1, dv)),
                     pl.BlockSpec((1, ts, val), lambda b, k: (b, cidx(k), (c0 - val) // val)),
                     pl.BlockSpec((1, ts, val), lambda b, k: (b, xidx(k), (xbase - val) // val))]
        args += [prev_c, prev_x, ng.reshape(1, dv), pc, gx]
    return pl.pallas_call(
        functools.partial(_gla_kernel, nh=nh, dk=dk, dv=dv, ncs=ncs, reverse=reverse, finish=fin is not None,
                          scale=dk ** -0.5),
        out_shape=(jax.ShapeDtypeStruct((bsz, nctx, val), out_dtype),
                   jax.ShapeDtypeStruct((bsz, seq, val), out_dtype)),
        grid=(bsz, ncs + nxs),
        in_specs=in_specs,
        out_specs=(pl.BlockSpec((1, ts, val), lambda b, k: (b, cidx(k), 0)),
                   pl.BlockSpec((1, ts, val), lambda b, k: (b, xidx(k), 0))),
        scratch_shapes=[pltpu.VMEM((nh, dv, dk), F32), pltpu.VMEM((ts, val), F32)],
        compiler_params=_cparams("parallel", "arbitrary"),
        name="gla_rev" if reverse else "gla_fwd",
    )(*args)


def gla_mixer(pc, px, c0, q4, gpad, p):
    dv = p['gla_norm_g'].shape[0]
    nh = (2 * q4) // dv
    dk = q4 // nh
    rank = p['gla_wg'].shape[1]
    xbase = 2 * q4
    gx = _to_col_major(px[..., c0 - xbase:])
    wgp = [jnp.zeros((2 * rank, q4), F32).at[d * rank:(d + 1) * rank].set(p['gla_wg'][d]) for d in range(2)]
    kw = dict(nh=nh, dk=dk, dv=dv, gpad=gpad)
    rc, rx = _gla_pass(pc, gx, c0, xbase, wgp[1], p['gla_bg'][1], reverse=True, **kw)
    oc, ox = _gla_pass(pc, gx, c0, xbase, wgp[0], p['gla_bg'][0], reverse=False,
                       fin=(rc, rx, p['gla_norm_g']), **kw)
    return oc, _from_col_major(ox)


HY_N2 = 128
HY_PAD = 8


def _hy_dims(n):
    n1 = 2 * n // HY_N2
    k1 = n1 // 2 + 1
    k1p = -(-k1 // 8) * 8
    return n1, k1, k1p


def _hy_tables(n):
    n1, k1, k1p = _hy_dims(n)
    big = 2 * n
    kk = jnp.arange(k1p, dtype=jnp.int32)[:, None]
    mm = jnp.arange(n1, dtype=jnp.int32)[None, :]
    ang = (2.0 * math.pi / n1) * ((kk * mm) % n1).astype(F32)
    valid = (kk < k1).astype(F32)
    f1 = jnp.concatenate([jnp.cos(ang) * valid, -jnp.sin(ang) * valid], axis=0)
    wk = jnp.where((kk == 0) | (kk == n1 // 2), 1.0, 2.0) * valid / big
    f1inv = jnp.concatenate([(jnp.cos(ang) * wk).T, (-jnp.sin(ang) * wk).T], axis=1)
    k1i = jnp.arange(k1, dtype=jnp.int32)[:, None, None]
    k2i = jnp.arange(HY_N2, dtype=jnp.int32)[None, :, None]
    n2i = jnp.arange(HY_N2, dtype=jnp.int32)[None, None, :]
    ph = (2.0 * math.pi / big) * ((n2i * (k1i + n1 * k2i)) % big).astype(F32)
    gr, gi = jnp.cos(ph), -jnp.sin(ph)
    gs = jnp.concatenate([gr, gi], axis=1)
    gts = jnp.concatenate([gr.swapaxes(1, 2), gi.swapaxes(1, 2)], axis=1)
    return f1.astype(BF16), f1inv.astype(BF16), gs.astype(BF16), gts.astype(BF16)


def _hy_stage1(src_ref, f1, a_r, a_i, nslab, k1p):
    pitch = HY_N2 + HY_PAD

    def body(i, carry):
        n2 = 2 * i
        rows = jnp.concatenate([src_ref[pl.ds(n2, nslab, stride=pitch), :],
                                src_ref[pl.ds(n2 + 1, nslab, stride=pitch), :]], axis=1)
        out = jnp.dot(f1, rows.astype(BF16), preferred_element_type=F32)
        base = pl.multiple_of(n2 * k1p, 8)
        a_r[pl.ds(base, k1p), :] = out[:k1p, :LANE]
        a_i[pl.ds(base, k1p), :] = out[k1p:, :LANE]
        base1 = pl.multiple_of(base + k1p, 8)
        a_r[pl.ds(base1, k1p), :] = out[:k1p, LANE:]
        a_i[pl.ds(base1, k1p), :] = out[k1p:, LANE:]
        return carry

    lax.fori_loop(0, HY_N2 // 2, body, 0, unroll=8)


def _hy_stage2(a_r, a_i, gs_ref, k, k1p):
    ar = a_r[pl.ds(k, HY_N2, stride=k1p), :]
    ai = a_i[pl.ds(k, HY_N2, stride=k1p), :]
    rhs = jnp.concatenate([ar, ai], axis=1).astype(BF16)
    out = jnp.dot(gs_ref[k], rhs, preferred_element_type=F32)
    h = HY_N2
    return out[:h, :LANE] - out[h:, LANE:], out[:h, LANE:] + out[h:, :LANE]


def _hyena_conv_kernel(y_ref, g_ref, wy_ref, by_ref, wg_ref, bg_ref, bias_ref, hr_ref, hi_ref,
                       f1_ref, f1inv_ref, gs_ref, gts_ref, o_ref, ypad, zbuf, a_r, a_i, *, conv_y):
    n = y_ref.shape[1]
    nslab = n // HY_N2
    pitch = HY_N2 + HY_PAD
    k1 = gs_ref.shape[0]
    k1p = f1_ref.shape[0] // 2
    h = HY_N2

    def slab_conv(ref, i, w_ref, b_ref):
        r0 = pl.multiple_of(i * HY_N2, HY_N2)
        prev8 = ref[0, pl.ds(pl.multiple_of(jnp.maximum(r0 - 8, 0), 8), 8), :]
        next8 = ref[0, pl.ds(pl.multiple_of(jnp.minimum(r0 + HY_N2, n - 8), 8), 8), :]
        return _dwconv3_tile(ref[0, pl.ds(r0, HY_N2), :], prev8, next8, i == 0, i == nslab - 1, w_ref, b_ref)

    def y_slab(i):
        if conv_y:
            return slab_conv(y_ref, i, wy_ref, by_ref)
        return y_ref[0, pl.ds(pl.multiple_of(i * HY_N2, HY_N2), HY_N2), :]

    def fill(i, carry):
        ypad[pl.ds(pl.multiple_of(i * pitch, 8), HY_N2), :] = y_slab(i)
        return carry

    lax.fori_loop(0, nslab, fill, 0, unroll=2)
    _hy_stage1(ypad, f1_ref[...], a_r, a_i, nslab, k1p)

    def freq_fwd(k, carry):
        xr, xi = _hy_stage2(a_r, a_i, gs_ref, k, k1p)
        hr, hi = hr_ref[0, k], hi_ref[0, k]
        zbuf[k] = jnp.concatenate([xr * hr - xi * hi, xr * hi + xi * hr], axis=1).astype(BF16)
        return carry

    lax.fori_loop(0, k1, freq_fwd, 0, unroll=4)

    def freq_inv(k, carry):
        out = jnp.dot(gts_ref[k], zbuf[k], preferred_element_type=F32)
        a_r[pl.ds(k, HY_N2, stride=k1p), :] = out[:h, :LANE] + out[h:, LANE:]
        a_i[pl.ds(k, HY_N2, stride=k1p), :] = out[:h, LANE:] - out[h:, :LANE]
        return carry

    lax.fori_loop(0, k1, freq_inv, 0, unroll=4)
    f1inv = f1inv_ref[...][:nslab]

    def inv1(i, carry):
        n2 = 2 * i
        b0 = pl.multiple_of(n2 * k1p, 8)
        b1 = pl.multiple_of(b0 + k1p, 8)
        rhs = jnp.concatenate(
            [jnp.concatenate([a_r[pl.ds(b0, k1p), :], a_i[pl.ds(b0, k1p), :]], axis=0),
             jnp.concatenate([a_r[pl.ds(b1, k1p), :], a_i[pl.ds(b1, k1p), :]], axis=0)], axis=1).astype(BF16)
        out = jnp.dot(f1inv, rhs, preferred_element_type=F32)
        ypad[pl.ds(n2, nslab, stride=pitch), :] = out[:, :LANE]
        ypad[pl.ds(n2 + 1, nslab, stride=pitch), :] = out[:, LANE:]
        return carry

    lax.fori_loop(0, HY_N2 // 2, inv1, 0, unroll=8)

    def finish(i, carry):
        p0 = pl.multiple_of(i * pitch, 8)
        gate = slab_conv(g_ref, i, wg_ref, bg_ref)
        res = gate * (ypad[pl.ds(p0, HY_N2), :] + y_slab(i) * bias_ref[0])
        o_ref[0, pl.ds(pl.multiple_of(i * HY_N2, HY_N2), HY_N2), :] = res.astype(o_ref.dtype)
        return carry

    lax.fori_loop(0, nslab, finish, 0, unroll=2)


def _hyena_spectrum_kernel(f_ref, f1_ref, gs_ref, hr_ref, hi_ref, a_r, a_i, fpad):
    big = f_ref.shape[1]
    nslab = big // HY_N2
    pitch = HY_N2 + HY_PAD
    k1 = gs_ref.shape[0]
    k1p = f1_ref.shape[0] // 2

    def fill(i, carry):
        fpad[pl.ds(pl.multiple_of(i * pitch, 8), HY_N2), :] = f_ref[0, pl.ds(pl.multiple_of(i * HY_N2, HY_N2), HY_N2), :]
        return carry

    lax.fori_loop(0, nslab, fill, 0, unroll=2)
    _hy_stage1(fpad, f1_ref[...], a_r, a_i, nslab, k1p)

    def freq(k, carry):
        xr, xi = _hy_stage2(a_r, a_i, gs_ref, k, k1p)
        hr_ref[0, k] = xr
        hi_ref[0, k] = xi
        return carry

    lax.fori_loop(0, k1, freq, 0, unroll=4)


def _one(shape, index_map):
    return pl.BlockSpec(shape, index_map, pipeline_mode=pl.Buffered(1))


def hyena_spectrum(filt, tabs):
    f1, _, gs, _ = tabs
    r, big, w = filt.shape
    ns = w // LANE
    n1, k1, k1p = _hy_dims(big // 2)
    shp = jax.ShapeDtypeStruct((r * ns, k1, HY_N2, LANE), F32)
    spec_o = pl.BlockSpec((1, k1, HY_N2, LANE), lambda i, j: (i * ns + j, 0, 0, 0))
    return pl.pallas_call(
        _hyena_spectrum_kernel,
        out_shape=(shp, shp),
        grid=(r, ns),
        in_specs=[_one((1, big, LANE), lambda i, j: (i, 0, j)),
                  _one(f1.shape, lambda i, j: (0, 0)),
                  _one(gs.shape, lambda i, j: (0, 0, 0))],
        out_specs=(spec_o, spec_o),
        scratch_shapes=[pltpu.VMEM((HY_N2 * k1p, LANE), F32), pltpu.VMEM((HY_N2 * k1p, LANE), F32),
                        pltpu.VMEM((n1 * (HY_N2 + HY_PAD), LANE), F32)],
        compiler_params=_cparams("parallel", "parallel"),
        name="hyena_spectrum",
    )(filt, f1, gs)


def hyena_order(y, ycol, g, gcol, conv_w, conv_b, cy, cg, bias, hr, hi, order, tabs, conv_y, out_dtype):
    f1, f1inv, gs, gts = tabs
    bsz, n = y.shape[:2]
    w = bias.shape[-1]
    ns = w // LANE
    n1, k1, k1p = _hy_dims(n)
    f1d = f1[:, :n1 // 2]
    pitch = HY_N2 + HY_PAD
    nslab = n // HY_N2
    cw = lambda c: _one((3, LANE), lambda j, b: (0, c + j))
    cb = lambda c: _one((1, LANE), lambda j, b: (0, c + j))
    hspec = _one((1, k1, HY_N2, LANE), lambda j, b: (order * ns + j, 0, 0, 0))
    return pl.pallas_call(
        functools.partial(_hyena_conv_kernel, conv_y=conv_y),
        out_shape=jax.ShapeDtypeStruct((bsz, n, w), out_dtype),
        grid=(ns, bsz),
        in_specs=[_one((1, n, LANE), lambda j, b: (b, 0, ycol + j)),
                  _one((1, n, LANE), lambda j, b: (b, 0, gcol + j)),
                  cw(cy), cb(cy), cw(cg), cb(cg),
                  _one((1, 1, LANE), lambda j, b: (order, 0, j)),
                  hspec, hspec,
                  _one(f1d.shape, lambda j, b: (0, 0)), _one(f1inv.shape, lambda j, b: (0, 0)),
                  _one(gs.shape, lambda j, b: (0, 0, 0)), _one(gts.shape, lambda j, b: (0, 0, 0))],
        out_specs=pl.BlockSpec((1, n, LANE), lambda j, b: (b, 0, j)),
        scratch_shapes=[pltpu.VMEM((nslab * pitch, LANE), F32), pltpu.VMEM((k1, HY_N2, 2 * LANE), BF16),
                        pltpu.VMEM((HY_N2 * k1p, LANE), F32), pltpu.VMEM((HY_N2 * k1p, LANE), F32)],
        compiler_params=pltpu.CompilerParams(dimension_semantics=("parallel", "parallel"),
                                             vmem_limit_bytes=BIG_VMEM_LIMIT),
        name="hyena_order",
    )(y, g, conv_w, conv_b, conv_w, conv_b, bias.reshape(bias.shape[0], 1, w), hr, hi, f1d, f1inv, gs, gts)


def hyena_latent(px, c0, filt, p):
    w = p['hy_bias'].shape[-1]
    ns = w // LANE
    n = px.shape[1]
    tabs = _hy_tables(n)
    hr, hi = hyena_spectrum(filt, tabs)
    cw, cb = p['hy_conv_w'], p['hy_conv_b'].reshape(1, -1)
    b0 = c0 // LANE
    y1 = hyena_order(px, b0, px, b0 + ns, cw, cb, 0, ns, p['hy_bias'], hr, hi, 0, tabs, True, F32)
    return hyena_order(y1, 0, px, b0 + 2 * ns, cw, cb, 0, 2 * ns, p['hy_bias'], hr, hi, 1, tabs, False, BF16)


def _hyena_ctx_kernel(zv_ref, z1_ref, z2_ref, wv_ref, bv_ref, w1_ref, b1_ref, w2_ref, b2_ref, bias_ref,
                      filt_ref, ff_ref, finv_ref, o_ref):
    n = zv_ref.shape[1]
    hp = lax.Precision.HIGHEST
    zero8 = jnp.zeros((8, LANE), F32)
    conv = lambda ref, w, b: _dwconv3_tile(ref[0], zero8, zero8, True, True, w, b)
    ff = ff_ref[...]
    kp = ff.shape[0] // 2
    y = conv(zv_ref, wv_ref, bv_ref)
    for o, (g_ref, w, b) in enumerate(((z1_ref, w1_ref, b1_ref), (z2_ref, w2_ref, b2_ref))):
        hsp = jnp.dot(ff, filt_ref[o], preferred_element_type=F32, precision=hp)
        ysp = jnp.dot(ff[:, :n], y, preferred_element_type=F32, precision=hp)
        hr, hi, yr, yi = hsp[:kp], hsp[kp:], ysp[:kp], ysp[kp:]
        z = jnp.concatenate([yr * hr - yi * hi, yr * hi + yi * hr], axis=0)
        cv = jnp.dot(finv_ref[...], z, preferred_element_type=F32, precision=hp)
        y = conv(g_ref, w, b) * (cv + y * bias_ref[o])
    o_ref[0] = y.astype(o_ref.dtype)


def hyena_context(pc, c0, filt, p):
    bsz, n = pc.shape[:2]
    w = p['hy_bias'].shape[-1]
    ns = w // LANE
    kp = -(-(n + 1) // 8) * 8
    kk = jnp.arange(kp, dtype=jnp.int32)[:, None]
    mm = jnp.arange(2 * n, dtype=jnp.int32)[None, :]
    ang = (math.pi / n) * ((kk * mm) % (2 * n)).astype(F32)
    valid = (kk <= n).astype(F32)
    ff = jnp.concatenate([jnp.cos(ang) * valid, -jnp.sin(ang) * valid], axis=0)
    wk = jnp.where((kk == 0) | (kk == n), 1.0, 2.0) * valid / (2 * n)
    finv = jnp.concatenate([(jnp.cos(ang) * wk).T[:n], (-jnp.sin(ang) * wk).T[:n]], axis=1)
    b0 = c0 // LANE
    zs = lambda c: pl.BlockSpec((1, n, LANE), lambda j, b: (b, 0, b0 + c + j))
    cw = lambda c: pl.BlockSpec((3, LANE), lambda j, b: (0, c + j))
    cb = lambda c: pl.BlockSpec((1, LANE), lambda j, b: (0, c + j))
    conv_w, conv_b = p['hy_conv_w'], p['hy_conv_b'].reshape(1, -1)
    return pl.pallas_call(
        _hyena_ctx_kernel,
        out_shape=jax.ShapeDtypeStruct((bsz, n, w), BF16),
        grid=(ns, bsz),
        in_specs=[zs(0), zs(ns), zs(2 * ns), cw(0), cb(0), cw(ns), cb(ns), cw(2 * ns), cb(2 * ns),
                  pl.BlockSpec((2, 1, LANE), lambda j, b: (0, 0, j)),
                  pl.BlockSpec((2, 2 * n, LANE), lambda j, b: (0, 0, j)),
                  pl.BlockSpec(ff.shape, lambda j, b: (0, 0)), pl.BlockSpec(finv.shape, lambda j, b: (0, 0))],
        out_specs=pl.BlockSpec((1, n, LANE), lambda j, b: (b, 0, j)),
        compiler_params=_cparams("parallel", "parallel"),
        name="hyena_context",
    )(pc, pc, pc, conv_w, conv_b, conv_w, conv_b, conv_w, conv_b, p['hy_bias'].reshape(2, 1, w), filt, ff, finv)


def _to_col_major(t):
    bsz, n = t.shape[:2]
    rows = n // GRID_W
    return t.reshape(bsz, rows, GRID_W, *t.shape[2:]).swapaxes(1, 2).reshape(bsz, n, *t.shape[2:])


def _from_col_major(t):
    bsz, n = t.shape[:2]
    rows = n // GRID_W
    return t.reshape(bsz, GRID_W, rows, *t.shape[2:]).swapaxes(1, 2).reshape(bsz, n, *t.shape[2:])


def _filter_positions(n):
    r = jnp.arange(2 * n, dtype=jnp.int32)
    return jnp.where(r < n, r, 2 * n - r).astype(F32)


def _filter_feats(n):
    pos = _filter_positions(n)
    t = pos / max(n - 1, 1)
    freqs = jnp.linspace(1e-4, HY_POS_FREQS - 1, HY_POS_FREQS, dtype=F32)
    ang = (2.0 * math.pi / n) * pos[:, None] * freqs[None]
    feats = jnp.concatenate([t[:, None], jnp.cos(ang), -jnp.sin(ang)], axis=-1)
    return jnp.pad(feats, ((0, 0), (0, LANE - feats.shape[1])))


def _filter_hidden_kernel(f_ref, w1_ref, b1_ref, s1_ref, w2_ref, b2_ref, s2_ref, o_ref):
    hp = lax.Precision.HIGHEST
    h = jnp.sin(s1_ref[...] * (jnp.dot(f_ref[...], w1_ref[...], preferred_element_type=F32, precision=hp)
                               + b1_ref[...]))
    o_ref[...] = jnp.sin(s2_ref[...] * (jnp.dot(h, w2_ref[...], preferred_element_type=F32, precision=hp)
                                        + b2_ref[...]))


def _filter_out_kernel(h_ref, wf_ref, wb_ref, bf_ref, bb_ref, rate_ref, o_ref, *, n, chunk):
    nchunk = 2 * n // chunk
    inv_span = 1.0 / max(n - 1, 1)

    def emit(c, acc):
        r0 = pl.multiple_of(c * chunk, chunk)
        past = c < nchunk // 2
        w = jnp.where(past, wf_ref[...], wb_ref[...])
        bias = jnp.where(past, bf_ref[...], bb_ref[...])
        hr0 = pl.multiple_of((c % (nchunk // 2)) * chunk, chunk)
        val = jnp.dot(h_ref[pl.ds(hr0, chunk), :], w, preferred_element_type=F32,
                      precision=lax.Precision.HIGHEST) + bias
        r = r0 + lax.broadcasted_iota(jnp.int32, (chunk, LANE), 0)
        t = jnp.where(r < n, r, 2 * n - r).astype(F32) * inv_span
        val = jnp.where(r == n, 0.0, val * jnp.exp(-t * rate_ref[...]))
        o_ref[0, pl.ds(r0, chunk), :] = val
        return acc + jnp.sum(jnp.abs(val), axis=0, keepdims=True)

    total = lax.fori_loop(0, nchunk, emit, jnp.zeros((1, LANE), F32))
    scale = 1.0 / (total + EPS)

    def rescale(c, carry):
        r0 = pl.multiple_of(c * chunk, chunk)
        o_ref[0, pl.ds(r0, chunk), :] = o_ref[0, pl.ds(r0, chunk), :] * scale
        return carry

    lax.fori_loop(0, nchunk, rescale, 0)


def hyena_filters(n, p):
    order, width = p['hy_bias'].shape
    ns = width // LANE
    hid = p['hy_f_w2'].shape[0]
    half = LANE // 2
    assert hid <= half
    feats = _filter_feats(n)
    feats = jnp.concatenate([feats[:n], feats[n:]], axis=1)
    zpad = lambda a, rows, cols: jnp.pad(a, ((0, rows - a.shape[0]), (0, cols - a.shape[1])))
    diag2 = lambda a: jnp.concatenate([jnp.pad(a, ((0, 0), (0, a.shape[1]))),
                                       jnp.pad(a, ((0, 0), (a.shape[1], 0)))], axis=0)
    twice = lambda a: jnp.tile(zpad(a.reshape(1, -1), 1, half), (1, 2))
    w1 = diag2(zpad(p['hy_f_w1'], LANE, half))
    w2 = diag2(zpad(p['hy_f_w2'], half, half))
    w3 = zpad(p['hy_f_w3'], half, p['hy_f_w3'].shape[1])
    w3_past = jnp.pad(w3, ((0, half), (0, 0)))
    w3_future = jnp.pad(w3, ((half, 0), (0, 0)))
    tr = _pick_tile(n, 2048, 8)
    full = lambda i: (0, 0)
    hidden = pl.pallas_call(
        _filter_hidden_kernel,
        out_shape=jax.ShapeDtypeStruct((n, LANE), F32),
        grid=(n // tr,),
        in_specs=[pl.BlockSpec((tr, 2 * LANE), lambda i: (i, 0)),
                  pl.BlockSpec((2 * LANE, LANE), full), pl.BlockSpec((1, LANE), full), pl.BlockSpec((1, LANE), full),
                  pl.BlockSpec((LANE, LANE), full), pl.BlockSpec((1, LANE), full), pl.BlockSpec((1, LANE), full)],
        out_specs=pl.BlockSpec((tr, LANE), lambda i: (i, 0)),
        compiler_params=_cparams("parallel"),
        name="hyena_filter_hidden",
    )(feats, w1, twice(p['hy_f_b1']), twice(p['hy_f_freq1']), w2, twice(p['hy_f_b2']), twice(p['hy_f_freq2']))
    rates = jnp.abs(jnp.linspace(math.log(HY_DECAY_TARGET) / HY_LONG_DECAY_PCT,
                                 math.log(HY_DECAY_TARGET) / HY_SHORT_DECAY_PCT, width, dtype=F32)).reshape(1, width)
    b3 = p['hy_f_b3'].reshape(1, -1)
    return pl.pallas_call(
        functools.partial(_filter_out_kernel, n=n, chunk=min(1024, n)),
        out_shape=jax.ShapeDtypeStruct((order, 2 * n, width), F32),
        grid=(order, ns),
        in_specs=[_one((n, LANE), lambda o, j: (0, 0)),
                  pl.BlockSpec((LANE, LANE), lambda o, j: (0, 2 * o * ns + j)),
                  pl.BlockSpec((LANE, LANE), lambda o, j: (0, (2 * o + 1) * ns + j)),
                  pl.BlockSpec((1, LANE), lambda o, j: (0, 2 * o * ns + j)),
                  pl.BlockSpec((1, LANE), lambda o, j: (0, (2 * o + 1) * ns + j)),
                  pl.BlockSpec((1, LANE), lambda o, j: (0, j))],
        out_specs=pl.BlockSpec((1, 2 * n, LANE), lambda o, j: (o, 0, j)),
        compiler_params=_cparams("parallel", "parallel"),
        name="hyena_filter_out",
    )(hidden, w3_past, w3_future, b3, b3, rates)


def kernel(x, c, ctx, c_ctx, w_mod, b_mod, norm1_g, norm2_g, w_in, s5_a_re, s5_a_im, s5_log_step, s5_b_re, s5_b_im, s5_c_re, s5_c_im, s5_d, s5_glu_w, s5_glu_b, hy_conv_w, hy_conv_b, hy_f_w1, hy_f_b1, hy_f_freq1, hy_f_w2, hy_f_b2, hy_f_freq2, hy_f_w3, hy_f_b3, hy_bias, gla_wg, gla_bg, gla_norm_g, w_branch, w_out, ff_w_up, ff_conv_w, ff_conv_b, ff_w_down, final_norm_g):
    bsz, seq, d = x.shape
    nctx = ctx.shape[1]
    depth = w_mod.shape[0]
    q4 = d // 4
    rank2 = 2 * gla_wg.shape[2]
    gpad = max(LANE, q4 // 2)
    assert bsz + 1 <= 8

    c_gk = q4
    c_gv = 2 * q4
    c_gg = 4 * q4
    c_gq = c_gg + rank2
    c_gr = c_gq + q4
    c_hy = c_gr + 2 * q4
    c_mg = c_hy + 3 * q4

    rows = jnp.zeros((8, d), F32).at[:bsz].set(c).at[bsz].set(c_ctx)
    mod = modulation(rows, w_mod, b_mod)

    xs = x.reshape(bsz * seq, d)
    cs = ctx.reshape(bsz * nctx, d)
    for l in range(depth):
        ctx_out = l < depth - 1
        sh1, s1, g1, sh2, s2, g2 = [mod[l, :, i * d:(i + 1) * d][:, None, :] for i in range(6)]
        gs1 = norm1_g[l] * (1.0 + s1)
        gs2 = norm2_g[l] * (1.0 + s2)
        bx = slice(0, bsz)
        bc = slice(bsz, bsz + 1)

        wl = w_in[l]
        w_pack = jnp.concatenate([
            wl[:, 0:c_gk], wl[:, c_hy:c_mg], wl[:, c_gr:c_hy], wl[:, c_gk:c_gv], wl[:, c_gq:c_gr],
            wl[:, c_gv:c_gg], wl[:, c_gg:c_gq],
            jnp.zeros((d, gpad - rank2), F32)], axis=1).astype(BF16)
        w_gate = wl[:, c_mg:].reshape(d, 3, d).swapaxes(0, 1).astype(BF16)
        wb = w_branch[l].astype(BF16)
        w_o = w_out[l].astype(BF16)
        w_up = cast_col_blocks(ff_w_up[l], _pick_tile(ff_w_down.shape[1], FFN_TF))
        w_dn = ff_w_down[l].astype(BF16)

        p = dict(s5_a_re=s5_a_re[l], s5_a_im=s5_a_im[l], s5_log_step=s5_log_step[l],
                 s5_b_re=s5_b_re[l], s5_b_im=s5_b_im[l], s5_c_re=s5_c_re[l], s5_c_im=s5_c_im[l],
                 s5_d=s5_d[l], s5_glu_w=s5_glu_w[l], s5_glu_b=s5_glu_b[l],
                 hy_conv_w=hy_conv_w[l], hy_conv_b=hy_conv_b[l], hy_f_w1=hy_f_w1[l], hy_f_b1=hy_f_b1[l],
                 hy_f_freq1=hy_f_freq1[l], hy_f_w2=hy_f_w2[l], hy_f_b2=hy_f_b2[l],
                 hy_f_freq2=hy_f_freq2[l], hy_f_w3=hy_f_w3[l], hy_f_b3=hy_f_b3[l], hy_bias=hy_bias[l],
                 gla_wg=gla_wg[l], gla_bg=gla_bg[l], gla_norm_g=gla_norm_g[l])

        px = norm_matmul(xs, gs1[bx], sh1[bx], w_pack).reshape(bsz, seq, -1)
        pc = norm_matmul(cs, gs1[bc], sh1[bc], w_pack).reshape(bsz, nctx, -1)

        ya_c, ya_x = s5_mixer(pc, px, q4, p)
        oc_c, oc_x = gla_mixer(pc, px, 6 * q4, q4, gpad, p)
        yb_x = hyena_latent(px, q4, hyena_filters(seq, p), p)

        flat = lambda t: t.reshape(-1, t.shape[-1])
        mx = merge(xs, gs1[bx], sh1[bx], flat(ya_x), flat(yb_x), flat(oc_x), w_gate, wb)
        xs = matmul_residual(mx, w_o, xs, g1[bx])
        xs = ffn(xs, bsz, gs2[bx], sh2[bx], w_up, ff_conv_w[l], ff_conv_b[l], w_dn, g2[bx])

        if ctx_out:
            yb_c = hyena_context(pc, q4, hyena_filters(nctx, p), p)
            mc = merge(cs, gs1[bc], sh1[bc], flat(ya_c), flat(yb_c), flat(oc_c), w_gate, wb)
            cs = matmul_residual(mc, w_o, cs, g1[bc])
            cs = ffn(cs, bsz, gs2[bc], sh2[bc], w_up, ff_conv_w[l], ff_conv_b[l], w_dn, g2[bc])

    return final_rmsnorm(xs, final_norm_g).reshape(bsz, seq, d)
```

```python
import functools
import math

import jax
import jax.numpy as jnp
from jax import lax
from jax.experimental import pallas as pl
from jax.experimental.pallas import tpu as pltpu

F32 = jnp.float32
BF16 = jnp.bfloat16
EPS = 1e-6
GRID_W = 64
S5_GROUP = 16
GLA_CHUNK = 64
GLA_GATE_TEMP = 16.0
HY_POS_FREQS = 16
HY_DECAY_TARGET = 1e-2
HY_SHORT_DECAY_PCT = 0.3
HY_LONG_DECAY_PCT = 1.5
LANE = 128
VMEM_LIMIT = 56 * 1024 * 1024
BIG_VMEM_LIMIT = 60 * 1024 * 1024
FFN_TF = 512
ROW_TILE = 1024
WIDE_COLS = 2048
MERGE_COLS = 256


def _pick_tile(n, cap, mult=LANE):
    best = None
    for t in range(mult, min(n, cap) + 1, mult):
        if n % t == 0:
            best = t
    assert best is not None, (n, cap, mult)
    return best


def _col_blocks(w, tn):
    nj = w.shape[-1] // tn
    return jnp.moveaxis(w.reshape(*w.shape[:-1], nj, tn), -2, 0)


def _cast_blocks_kernel(w_ref, o_ref):
    o_ref[0] = w_ref[...].astype(o_ref.dtype)


def cast_col_blocks(w, tn):
    k, n = w.shape
    return pl.pallas_call(
        _cast_blocks_kernel,
        out_shape=jax.ShapeDtypeStruct((n // tn, k, tn), BF16),
        grid=(n // tn,),
        in_specs=[pl.BlockSpec((k, tn), lambda j: (0, j))],
        out_specs=pl.BlockSpec((1, k, tn), lambda j: (j, 0, 0)),
        compiler_params=pltpu.CompilerParams(dimension_semantics=("parallel",), vmem_limit_bytes=VMEM_LIMIT),
        name="cast_col_blocks",
    )(w)


def _cparams(*sem):
    return pltpu.CompilerParams(dimension_semantics=sem, vmem_limit_bytes=VMEM_LIMIT)


def _mod_kernel(r_ref, w_ref, b_ref, o_ref):
    r = r_ref[...]
    s = r * jax.nn.sigmoid(r)
    o_ref[0] = jnp.dot(s, w_ref[0], preferred_element_type=F32,
                       precision=lax.Precision.HIGHEST) + b_ref[0]


def modulation(rows, w_mod, b_mod):
    depth, d, n = w_mod.shape
    tn = _pick_tile(n, 1024)
    return pl.pallas_call(
        _mod_kernel,
        out_shape=jax.ShapeDtypeStruct((depth, 8, n), F32),
        grid=(depth, n // tn),
        in_specs=[pl.BlockSpec((8, d), lambda l, j: (0, 0)),
                  pl.BlockSpec((1, d, tn), lambda l, j: (l, 0, j)),
                  pl.BlockSpec((1, 1, tn), lambda l, j: (l, 0, j))],
        out_specs=pl.BlockSpec((1, 8, tn), lambda l, j: (l, 0, j)),
        compiler_params=_cparams("parallel", "parallel"),
        name="modulation",
    )(rows, w_mod, b_mod.reshape(depth, 1, n))


def _norm_mm_kernel(x_ref, gs_ref, sh_ref, w_ref, o_ref, h_ref):
    @pl.when(pl.program_id(1) == 0)
    def _():
        x = x_ref[...]
        ms = jnp.mean(x * x, axis=-1, keepdims=True)
        h_ref[...] = (x * lax.rsqrt(ms + EPS) * gs_ref[0] + sh_ref[0]).astype(BF16)

    o_ref[...] = jnp.dot(h_ref[...], w_ref[0], preferred_element_type=F32).astype(o_ref.dtype)


def norm_matmul(x, gs, sh, w, out_dtype=F32):
    m, d = x.shape
    n = w.shape[1]
    nb = gs.shape[0]
    tm = _pick_tile(m // nb, ROW_TILE, 8)
    tn = _pick_tile(n, WIDE_COLS)
    tpb = (m // nb) // tm
    return pl.pallas_call(
        _norm_mm_kernel,
        out_shape=jax.ShapeDtypeStruct((m, n), out_dtype),
        grid=(m // tm, n // tn),
        in_specs=[pl.BlockSpec((tm, d), lambda i, j: (i, 0)),
                  pl.BlockSpec((1, 1, d), lambda i, j: (i // tpb, 0, 0)),
                  pl.BlockSpec((1, 1, d), lambda i, j: (i // tpb, 0, 0)),
                  pl.BlockSpec((1, d, tn), lambda i, j: (j, 0, 0))],
        out_specs=pl.BlockSpec((tm, tn), lambda i, j: (i, j)),
        scratch_shapes=[pltpu.VMEM((tm, d), BF16)],
        compiler_params=_cparams("parallel", "arbitrary"),
        name="norm_matmul",
    )(x, gs, sh, _col_blocks(w, tn))


def _mm_res_kernel(a_ref, w_ref, r_ref, g_ref, o_ref):
    acc = jnp.dot(a_ref[...], w_ref[0], preferred_element_type=F32)
    o_ref[...] = r_ref[...] + g_ref[0] * acc


def matmul_residual(a, w, res, g):
    m, k = a.shape
    n = w.shape[1]
    nb = g.shape[0]
    tm = _pick_tile(m // nb, ROW_TILE, 8)
    tn = _pick_tile(n, WIDE_COLS)
    tpb = (m // nb) // tm
    resident = pl.Buffered(1) if n == tn else None
    return pl.pallas_call(
        _mm_res_kernel,
        out_shape=jax.ShapeDtypeStruct((m, n), F32),
        grid=(m // tm, n // tn),
        in_specs=[pl.BlockSpec((tm, k), lambda i, j: (i, 0)),
                  pl.BlockSpec((1, k, tn), lambda i, j: (j, 0, 0), pipeline_mode=resident),
                  pl.BlockSpec((tm, tn), lambda i, j: (i, j)),
                  pl.BlockSpec((1, 1, tn), lambda i, j: (i // tpb, 0, j))],
        out_specs=pl.BlockSpec((tm, tn), lambda i, j: (i, j)),
        compiler_params=_cparams("parallel", "arbitrary"),
        name="matmul_residual",
    )(a, _col_blocks(w, tn), res, g)


def _merge_kernel(x_ref, gs_ref, sh_ref, ya_ref, yb_ref, yc_ref, wg_ref, wb_ref, o_ref, h_ref):
    @pl.when(pl.program_id(1) == 0)
    def _():
        x = x_ref[...]
        ms = jnp.mean(x * x, axis=-1, keepdims=True)
        h_ref[...] = (x * lax.rsqrt(ms + EPS) * gs_ref[0] + sh_ref[0]).astype(BF16)

    h = h_ref[...]
    m = None
    lo = 0
    for i, y_ref in enumerate((ya_ref, yb_ref, yc_ref)):
        hi = lo + y_ref.shape[1]
        gate = jax.nn.sigmoid(jnp.dot(h, wg_ref[0, i], preferred_element_type=F32))
        br = jnp.dot(y_ref[...], wb_ref[0, lo:hi, :], preferred_element_type=F32)
        m = gate * br if m is None else m + gate * br
        lo = hi
    o_ref[...] = m.astype(o_ref.dtype)


def merge(x, gs, sh, ya, yb, yc, wg, wb):
    m, d = x.shape
    nb = gs.shape[0]
    tm = _pick_tile(m // nb, ROW_TILE, 8)
    tn = _pick_tile(d, MERGE_COLS)
    tpb = (m // nb) // tm
    row = lambda i, j: (i, 0)
    wy = wb.shape[0]
    return pl.pallas_call(
        _merge_kernel,
        out_shape=jax.ShapeDtypeStruct((m, d), BF16),
        grid=(m // tm, d // tn),
        in_specs=[pl.BlockSpec((tm, d), row),
                  pl.BlockSpec((1, 1, d), lambda i, j: (i // tpb, 0, 0)),
                  pl.BlockSpec((1, 1, d), lambda i, j: (i // tpb, 0, 0)),
                  pl.BlockSpec((tm, ya.shape[1]), row),
                  pl.BlockSpec((tm, yb.shape[1]), row),
                  pl.BlockSpec((tm, yc.shape[1]), row),
                  pl.BlockSpec((1, 3, d, tn), lambda i, j: (j, 0, 0, 0)),
                  pl.BlockSpec((1, wy, tn), lambda i, j: (j, 0, 0))],
        out_specs=pl.BlockSpec((tm, tn), lambda i, j: (i, j)),
        scratch_shapes=[pltpu.VMEM((tm, d), BF16)],
        compiler_params=_cparams("parallel", "arbitrary"),
        name="merge",
    )(x, gs, sh, ya, yb, yc, _col_blocks(wg, tn), _col_blocks(wb, tn))


def _dwconv3_tile(a, prev8, next8, first, last, w_ref, cb_ref):
    tr = a.shape[0]
    row = lax.broadcasted_iota(jnp.int32, a.shape, 0)
    before = jnp.where(first, 0.0, prev8[7:8, :])
    after = jnp.where(last, 0.0, next8[0:1, :])
    prev = jnp.where(row == 0, before, pltpu.roll(a, 1, 0))
    nxt = jnp.where(row == tr - 1, after, pltpu.roll(a, tr - 1, 0))
    return prev * w_ref[0:1, :] + a * w_ref[1:2, :] + nxt * w_ref[2:3, :] + cb_ref[...]


def _ffn_kernel(x_ref, xp_ref, xn_ref, gs_ref, sh_ref, wa_ref, wb_ref, cw_ref, cb_ref, wd_ref, g_ref,
                o_ref, h_ref, hh_ref, *, tpb):
    i, f = pl.program_id(0), pl.program_id(1)

    def normed(x):
        ms = jnp.mean(x * x, axis=-1, keepdims=True)
        return (x * lax.rsqrt(ms + EPS) * gs_ref[0] + sh_ref[0]).astype(BF16)

    @pl.when(f == 0)
    def _():
        h_ref[...] = normed(x_ref[...])
        hh_ref[0:8, :] = normed(xp_ref[...])
        hh_ref[8:16, :] = normed(xn_ref[...])

    wa = wa_ref[0]
    a = jnp.dot(h_ref[...], wa, preferred_element_type=F32)
    ah = jnp.dot(hh_ref[...], wa, preferred_element_type=F32)
    b = jnp.dot(h_ref[...], wb_ref[0], preferred_element_type=F32)
    pos = i % tpb
    conv = _dwconv3_tile(a, ah[0:8], ah[8:16], pos == 0, pos == tpb - 1, cw_ref, cb_ref)
    act = (conv * jax.nn.sigmoid(conv) * b).astype(BF16)
    half = o_ref.shape[1] // 2
    for lo in (0, half):
        part = jnp.dot(act, wd_ref[:, lo:lo + half], preferred_element_type=F32)

        @pl.when(f == 0)
        def _(part=part, lo=lo):
            o_ref[:, lo:lo + half] = part

        @pl.when(f > 0)
        def _(part=part, lo=lo):
            o_ref[:, lo:lo + half] += part

    @pl.when(f == pl.num_programs(1) - 1)
    def _():
        o_ref[...] = x_ref[...] + g_ref[0] * o_ref[...]


def ffn(x, nseq, gs, sh, w_up, conv_w, conv_b, w_dn, g):
    m, d = x.shape
    fh = w_dn.shape[0]
    nb = gs.shape[0]
    tm = _pick_tile(m // nseq, ROW_TILE, 8)
    tf = w_up.shape[2]
    tpb = (m // nseq) // tm
    tpm = (m // nb) // tm
    nf = fh // tf
    r8 = tm // 8
    last8 = m // 8 - 1
    mod = lambda i, f: (i // tpm, 0, 0)
    return pl.pallas_call(
        functools.partial(_ffn_kernel, tpb=tpb),
        out_shape=jax.ShapeDtypeStruct((m, d), F32),
        grid=(m // tm, nf),
        in_specs=[pl.BlockSpec((tm, d), lambda i, f: (i, 0), pipeline_mode=pl.Buffered(1)),
                  pl.BlockSpec((8, d), lambda i, f: (jnp.maximum(i * r8 - 1, 0), 0)),
                  pl.BlockSpec((8, d), lambda i, f: (jnp.minimum((i + 1) * r8, last8), 0)),
                  pl.BlockSpec((1, 1, d), mod), pl.BlockSpec((1, 1, d), mod),
                  pl.BlockSpec((1, d, tf), lambda i, f: (f, 0, 0)),
                  pl.BlockSpec((1, d, tf), lambda i, f: (f + nf, 0, 0)),
                  pl.BlockSpec((3, tf), lambda i, f: (0, f)),
                  pl.BlockSpec((1, tf), lambda i, f: (0, f)),
                  pl.BlockSpec((tf, d), lambda i, f: (f, 0)),
                  pl.BlockSpec((1, 1, d), mod)],
        out_specs=pl.BlockSpec((tm, d), lambda i, f: (i, 0)),
        scratch_shapes=[pltpu.VMEM((tm, d), BF16), pltpu.VMEM((16, d), BF16)],
        compiler_params=pltpu.CompilerParams(dimension_semantics=("parallel", "arbitrary"),
                                             vmem_limit_bytes=BIG_VMEM_LIMIT),
        name="ffn",
    )(x, x, x, gs, sh, w_up, w_up, conv_w, conv_b.reshape(1, fh), w_dn, g)


def _rmsnorm_kernel(x_ref, g_ref, o_ref):
    x = x_ref[...]
    ms = jnp.mean(x * x, axis=-1, keepdims=True)
    o_ref[...] = x * lax.rsqrt(ms + EPS) * g_ref[...]


def final_rmsnorm(x, g):
    m, d = x.shape
    tm = _pick_tile(m, ROW_TILE, 8)
    return pl.pallas_call(
        _rmsnorm_kernel,
        out_shape=jax.ShapeDtypeStruct((m, d), F32),
        grid=(m // tm,),
        in_specs=[pl.BlockSpec((tm, d), lambda i: (i, 0)), pl.BlockSpec((1, d), lambda i: (0, 0))],
        out_specs=pl.BlockSpec((tm, d), lambda i: (i, 0)),
        compiler_params=_cparams("parallel"),
        name="final_rmsnorm",
    )(x, g.reshape(1, d))


S5_CHUNK = 256


def _const_spec(shape):
    zeros = (0,) * len(shape)
    return pl.BlockSpec(shape, lambda b, k: zeros, pipeline_mode=pl.Buffered(1))


def _s5_kernel(*refs, ncc, reverse, finish):
    if finish:
        (uc_ref, ux_ref, bblk_ref, cblk_ref, enr_ref, eni_ref, epr_ref, epi_ref, ac_ref,
         pc_ref, px_ref, d_ref, gw_ref, gb_ref, yc_ref, yx_ref, h_ref) = refs
    else:
        (uc_ref, ux_ref, bblk_ref, cblk_ref, enr_ref, eni_ref, epr_ref, epi_ref, ac_ref,
         yc_ref, yx_ref, h_ref) = refs
    k = pl.program_id(1)
    t = uc_ref.shape[1]
    gn = enr_ref.shape[1]

    @pl.when(k == 0)
    def _():
        h_ref[...] = jnp.zeros_like(h_ref)

    is_ctx = k < ncc
    u = jnp.where(is_ctx, uc_ref[0], ux_ref[0])
    row = lax.broadcasted_iota(jnp.int32, (t, t), 0)
    col = lax.broadcasted_iota(jnp.int32, (t, t), 1)
    tri = jnp.where((col >= row) if reverse else (col <= row), 1.0, 0.0).astype(BF16)
    last = 0 if reverse else t - 1
    nblk, wb, sb2 = bblk_ref.shape
    sb = sb2 // 2
    ys = []
    for j in range(nblk):
        sc = slice(j * sb, (j + 1) * sb)
        bu = jnp.dot(u[:, j * wb:(j + 1) * wb].astype(BF16), bblk_ref[j], preferred_element_type=F32)
        br, bi = bu[:, :sb], bu[:, sb:]
        enr, eni = enr_ref[:, sc], eni_ref[:, sc]
        z = jnp.concatenate([br * enr - bi * eni, br * eni + bi * enr], axis=1).astype(BF16)
        cs = jnp.dot(tri, z, preferred_element_type=F32)
        hr, hi = h_ref[0:1, sc], h_ref[1:2, sc]
        acr, aci = ac_ref[0:1, sc], ac_ref[1:2, sc]
        sr = cs[:, :sb] + (hr * acr - hi * aci)
        si = cs[:, sb:] + (hr * aci + hi * acr)
        epr, epi = epr_ref[:, sc], epi_ref[:, sc]
        xr = sr * epr - si * epi
        xi = sr * epi + si * epr
        h_ref[0:1, sc] = xr[last:last + 1, :]
        h_ref[1:2, sc] = xi[last:last + 1, :]
        xs = jnp.concatenate([xr, xi], axis=1).astype(BF16)
        ys.append(jnp.dot(xs, cblk_ref[j], preferred_element_type=F32))
    y = ys[0] if nblk == 1 else jnp.concatenate(ys, axis=1)
    if finish:
        y = y + jnp.where(is_ctx, pc_ref[0], px_ref[0]) + d_ref[...] * u
        y = jax.nn.gelu(y)
        gate = jnp.dot(y.astype(BF16), gw_ref[...], preferred_element_type=F32) + gb_ref[...]
        y = y * jax.nn.sigmoid(gate)

    @pl.when(is_ctx)
    def _():
        yc_ref[0] = y.astype(yc_ref.dtype)

    @pl.when(jnp.logical_not(is_ctx))
    def _():
        yx_ref[0] = y.astype(yx_ref.dtype)


def _s5_tables(a_re, a_im, log_step, b_re, b_im, c_re, c_im, t, reverse):
    g, n, h = b_re.shape
    gpb = min(g, max(1, LANE // h))
    nblk = g // gpb
    eye = jnp.eye(gpb, dtype=F32)
    dt = jnp.exp(log_step)[:, None]
    ldr, ldi = a_re * dt, a_im * dt
    mag = jnp.exp(ldr)
    abr, abi = mag * jnp.cos(ldi), mag * jnp.sin(ldi)
    den = a_re * a_re + a_im * a_im
    nr, ni = abr - 1.0, abi
    fr = (nr * a_re + ni * a_im) / den
    fi = (ni * a_re - nr * a_im) / den
    bbr = fr[..., None] * b_re - fi[..., None] * b_im
    bbi = fr[..., None] * b_im + fi[..., None] * b_re
    blk_b = lambda m: jnp.einsum('jgnh,gk->jghkn', m.reshape(nblk, gpb, n, h), eye).reshape(nblk, gpb * h, gpb * n)
    bblk = jnp.concatenate([blk_b(bbr), blk_b(bbi)], axis=2).astype(BF16)
    blk_c = lambda m: jnp.einsum('jghn,gk->jgnkh', m.reshape(nblk, gpb, h, n), eye).reshape(nblk, gpb * n, gpb * h)
    cblk = jnp.concatenate([blk_c(c_re), -blk_c(c_im)], axis=1).astype(BF16)
    centre = float(t // 2)
    pos = jnp.arange(t, dtype=F32)[:, None]
    steps = ((t - pos) if reverse else (pos + 1.0)) - centre
    lr, li = ldr.reshape(1, g * n), ldi.reshape(1, g * n)
    er, ei = steps * lr, steps * li
    epr, epi = jnp.exp(er) * jnp.cos(ei), jnp.exp(er) * jnp.sin(ei)
    enr, eni = jnp.exp(-er) * jnp.cos(ei), -jnp.exp(-er) * jnp.sin(ei)
    ac = jnp.concatenate([jnp.exp(centre * lr) * jnp.cos(centre * li),
                          jnp.exp(centre * lr) * jnp.sin(centre * li)], axis=0)
    return bblk, cblk, (enr, eni, epr, epi, ac)


def _s5_pass(pc, px, col, width, tabs, reverse, fin=None, out_dtype=F32):
    bsz, nctx = pc.shape[:2]
    seq = px.shape[1]
    t = S5_CHUNK
    ncc, ncx = nctx // t, seq // t
    bblk, cblk, (enr, eni, epr, epi, ac) = tabs
    gn = enr.shape[1]
    if reverse:
        cidx = lambda k: jnp.maximum(ncc - 1 - k, 0)
        xidx = lambda k: jnp.minimum(ncx - 1 - (k - ncc), ncx - 1)
    else:
        cidx = lambda k: jnp.minimum(k, ncc - 1)
        xidx = lambda k: jnp.maximum(k - ncc, 0)
    in_specs = [pl.BlockSpec((1, t, width), lambda b, k: (b, cidx(k), col)),
                pl.BlockSpec((1, t, width), lambda b, k: (b, xidx(k), col)),
                _const_spec(bblk.shape), _const_spec(cblk.shape),
                _const_spec(enr.shape), _const_spec(eni.shape), _const_spec(epr.shape), _const_spec(epi.shape),
                _const_spec(ac.shape)]
    args = [pc, px, bblk, cblk, enr, eni, epr, epi, ac]
    if fin is not None:
        prev_c, prev_x, dvec, gw, gb = fin
        in_specs += [pl.BlockSpec((1, t, width), lambda b, k: (b, cidx(k), 0)),
                     pl.BlockSpec((1, t, width), lambda b, k: (b, xidx(k), 0)),
                     _const_spec((1, width)), _const_spec(gw.shape), _const_spec((1, width))]
        args += [prev_c, prev_x, dvec.reshape(1, width), gw, gb.reshape(1, width)]
    return pl.pallas_call(
        functools.partial(_s5_kernel, ncc=ncc, reverse=reverse, finish=fin is not None),
        out_shape=(jax.ShapeDtypeStruct((bsz, nctx, width), out_dtype),
                   jax.ShapeDtypeStruct((bsz, seq, width), out_dtype)),
        grid=(bsz, ncc + ncx),
        in_specs=in_specs,
        out_specs=(pl.BlockSpec((1, t, width), lambda b, k: (b, cidx(k), 0)),
                   pl.BlockSpec((1, t, width), lambda b, k: (b, xidx(k), 0))),
        scratch_shapes=[pltpu.VMEM((2, gn), F32)],
        compiler_params=_cparams("parallel", "arbitrary"),
        name="s5_rev" if reverse else "s5_fwd",
    )(*args)


def s5_mixer(pc, px, width, p):
    tabs = [_s5_tables(p['s5_a_re'][d], p['s5_a_im'][d], p['s5_log_step'][d], p['s5_b_re'][d], p['s5_b_im'][d],
                       p['s5_c_re'][d], p['s5_c_im'][d], S5_CHUNK, d == 1) for d in range(2)]
    bc, bx = _s5_pass(pc, px, 0, width, tabs[1], reverse=True)
    return _s5_pass(pc, px, 0, width, tabs[0], reverse=False,
                    fin=(bc, bx, p['s5_d'], p['s5_glu_w'].astype(BF16), p['s5_glu_b']), out_dtype=BF16)


GLA_STEP = 256


def _gla_kernel(*refs, nh, dk, dv, ncs, reverse, finish, scale):
    if finish:
        (kc_ref, qc_ref, vc_ref, lc_ref, kx_ref, qx_ref, vx_ref, lx_ref, wg_ref, bg_ref, pc_ref, px_ref, ng_ref,
         rc_ref, rx_ref, oc_ref, ox_ref, s_ref, o_scr) = refs
    else:
        (kc_ref, qc_ref, vc_ref, lc_ref, kx_ref, qx_ref, vx_ref, lx_ref, wg_ref, bg_ref,
         oc_ref, ox_ref, s_ref, o_scr) = refs
    ts = kc_ref.shape[1]
    nchunk = ts // GLA_CHUNK
    is_ctx = pl.program_id(1) < ncs
    pick = lambda c_ref, x_ref: jnp.where(is_ctx, c_ref[0], x_ref[0])

    @pl.when(pl.program_id(1) == 0)
    def _():
        s_ref[...] = jnp.zeros_like(s_ref)

    hp = lax.Precision.HIGHEST
    rank2 = wg_ref.shape[0]
    pre = jnp.dot(pick(lc_ref, lx_ref)[:, :rank2], wg_ref[...], preferred_element_type=F32,
                  precision=hp) + bg_ref[...]
    g = jax.nn.log_sigmoid(pre) * (1.0 / GLA_GATE_TEMP)
    ci = lax.broadcasted_iota(jnp.int32, (GLA_CHUNK, GLA_CHUNK), 0)
    cj = lax.broadcasted_iota(jnp.int32, (GLA_CHUNK, GLA_CHUNK), 1)
    keep = (cj >= ci) if reverse else (cj <= ci)
    cum = jnp.where(keep, 1.0, 0.0)
    b = jnp.concatenate([jnp.dot(cum, g[c * GLA_CHUNK:(c + 1) * GLA_CHUNK], preferred_element_type=F32, precision=hp)
                         for c in range(nchunk)], axis=0)
    eb = jnp.exp(b)
    enb = jnp.exp(-b)
    q_in = pick(qc_ref, qx_ref) * scale * eb
    k_out = pick(kc_ref, kx_ref) * enb
    v = pick(vc_ref, vx_ref)
    if finish:
        prev = pick(pc_ref, px_ref)
        r = pick(rc_ref, rx_ref)
        out_gate = r * jax.nn.sigmoid(r)
    nt = (((1,), (1,)), ((), ()))
    tn = (((0,), (0,)), ((), ()))
    for c in (range(nchunk - 1, -1, -1) if reverse else range(nchunk)):
        r0 = c * GLA_CHUNK
        rows = slice(r0, r0 + GLA_CHUNK)
        end = r0 if reverse else r0 + GLA_CHUNK - 1
        etot = jnp.exp(b[end:end + 1, :])
        k_kv = k_out[rows] * etot
        for h in range(nh):
            kc = slice(h * dk, (h + 1) * dk)
            vc = slice(h * dv, (h + 1) * dv)
            qh = q_in[rows, kc].astype(BF16)
            kh = k_out[rows, kc].astype(BF16)
            kkv = k_kv[:, kc].astype(BF16)
            vh = v[rows, vc].astype(BF16)
            st = s_ref[h]
            sc = lax.dot_general(qh, kh, nt, preferred_element_type=F32)
            sc = jnp.where(keep, sc, 0.0).astype(BF16)
            o = (jnp.dot(sc, vh, preferred_element_type=F32)
                 + lax.dot_general(qh, st.astype(BF16), nt, preferred_element_type=F32))
            kvt = lax.dot_general(vh, kkv, tn, preferred_element_type=F32)
            s_ref[h] = st * etot[:, kc] + kvt
            if finish:
                o = o + prev[rows, vc]
                ms = jnp.mean(o * o, axis=-1, keepdims=True)
                o = o * lax.rsqrt(ms + EPS) * ng_ref[...] * out_gate[rows, vc]
            o_scr[rows, vc] = o

    @pl.when(is_ctx)
    def _():
        oc_ref[0] = o_scr[...].astype(oc_ref.dtype)

    @pl.when(jnp.logical_not(is_ctx))
    def _():
        ox_ref[0] = o_scr[...].astype(ox_ref.dtype)


def _gla_pass(pc, gx, c0, xbase, wgp, bg, nh, dk, dv, gpad, reverse, fin=None):
    bsz, nctx = pc.shape[:2]
    seq = gx.shape[1]
    ts = GLA_STEP
    ncs, nxs = nctx // ts, seq // ts
    key, val = nh * dk, nh * dv
    if reverse:
        cidx = lambda k: jnp.maximum(ncs - 1 - k, 0)
        xidx = lambda k: jnp.minimum(nxs - 1 - (k - ncs), nxs - 1)
    else:
        cidx = lambda k: jnp.minimum(k, ncs - 1)
        xidx = lambda k: jnp.maximum(k - ncs, 0)

    def group(idx, base):
        return [pl.BlockSpec((1, ts, key), lambda b, k: (b, idx(k), base // key)),
                pl.BlockSpec((1, ts, key), lambda b, k: (b, idx(k), base // key + 1)),
                pl.BlockSpec((1, ts, val), lambda b, k: (b, idx(k), (base + 2 * key) // val)),
                pl.BlockSpec((1, ts, gpad), lambda b, k: (b, idx(k), (base + 2 * key + val) // gpad))]

    out_dtype = F32 if fin is None else BF16
    in_specs = group(cidx, c0) + group(xidx, xbase) + [_const_spec(wgp.shape), _const_spec((1, key))]
    args = [pc] * 4 + [gx] * 4 + [wgp, bg.reshape(1, key)]
    if fin is not None:
        prev_c, prev_x, ng = fin
        in_specs += [pl.BlockSpec((1, ts, val), lambda b, k: (b, cidx(k), 0)),
                     pl.BlockSpec((1, ts, val), lambda b, k: (b, xidx(k), 0)), _const_spec((1, dv)),
                     pl.BlockSpec((1, ts, val), lambda b, k: (b, cidx(k), (c0 - val) // val)),
                     pl.BlockSpec((1, ts, val), lambda b, k: (b, xidx(k), (xbase - val) // val))]
        args += [prev_c, prev_x, ng.reshape(1, dv), pc, gx]
    return pl.pallas_call(
        functools.partial(_gla_kernel, nh=nh, dk=dk, dv=dv, ncs=ncs, reverse=reverse, finish=fin is not None,
                          scale=dk ** -0.5),
        out_shape=(jax.ShapeDtypeStruct((bsz, nctx, val), out_dtype),
                   jax.ShapeDtypeStruct((bsz, seq, val), out_dtype)),
        grid=(bsz, ncs + nxs),
        in_specs=in_specs,
        out_specs=(pl.BlockSpec((1, ts, val), lambda b, k: (b, cidx(k), 0)),
                   pl.BlockSpec((1, ts, val), lambda b, k: (b, xidx(k), 0))),
        scratch_shapes=[pltpu.VMEM((nh, dv, dk), F32), pltpu.VMEM((ts, val), F32)],
        compiler_params=_cparams("parallel", "arbitrary"),
        name="gla_rev" if reverse else "gla_fwd",
    )(*args)


def gla_mixer(pc, px, c0, q4, gpad, p):
    dv = p['gla_norm_g'].shape[0]
    nh = (2 * q4) // dv
    dk = q4 // nh
    rank = p['gla_wg'].shape[1]
    xbase = 2 * q4
    gx = _to_col_major(px[..., c0 - xbase:])
    wgp = [jnp.zeros((2 * rank, q4), F32).at[d * rank:(d + 1) * rank].set(p['gla_wg'][d]) for d in range(2)]
    kw = dict(nh=nh, dk=dk, dv=dv, gpad=gpad)
    rc, rx = _gla_pass(pc, gx, c0, xbase, wgp[1], p['gla_bg'][1], reverse=True, **kw)
    oc, ox = _gla_pass(pc, gx, c0, xbase, wgp[0], p['gla_bg'][0], reverse=False,
                       fin=(rc, rx, p['gla_norm_g']), **kw)
    return oc, _from_col_major(ox)


HY_N2 = 128
HY_PAD = 8


def _hy_dims(n):
    n1 = 2 * n // HY_N2
    k1 = n1 // 2 + 1
    k1p = -(-k1 // 8) * 8
    return n1, k1, k1p


def _hy_tables(n):
    n1, k1, k1p = _hy_dims(n)
    big = 2 * n
    kk = jnp.arange(k1p, dtype=jnp.int32)[:, None]
    mm = jnp.arange(n1, dtype=jnp.int32)[None, :]
    ang = (2.0 * math.pi / n1) * ((kk * mm) % n1).astype(F32)
    valid = (kk < k1).astype(F32)
    f1 = jnp.concatenate([jnp.cos(ang) * valid, -jnp.sin(ang) * valid], axis=0)
    wk = jnp.where((kk == 0) | (kk == n1 // 2), 1.0, 2.0) * valid / big
    f1inv = jnp.concatenate([(jnp.cos(ang) * wk).T, (-jnp.sin(ang) * wk).T], axis=1)
    k1i = jnp.arange(k1, dtype=jnp.int32)[:, None, None]
    k2i = jnp.arange(HY_N2, dtype=jnp.int32)[None, :, None]
    n2i = jnp.arange(HY_N2, dtype=jnp.int32)[None, None, :]
    ph = (2.0 * math.pi / big) * ((n2i * (k1i + n1 * k2i)) % big).astype(F32)
    gr, gi = jnp.cos(ph), -jnp.sin(ph)
    gs = jnp.concatenate([gr, gi], axis=1)
    gts = jnp.concatenate([gr.swapaxes(1, 2), gi.swapaxes(1, 2)], axis=1)
    return f1.astype(BF16), f1inv.astype(BF16), gs.astype(BF16), gts.astype(BF16)


def _hy_stage1(src_ref, f1, a_r, a_i, nslab, k1p):
    pitch = HY_N2 + HY_PAD

    def body(i, carry):
        n2 = 2 * i
        rows = jnp.concatenate([src_ref[pl.ds(n2, nslab, stride=pitch), :],
                                src_ref[pl.ds(n2 + 1, nslab, stride=pitch), :]], axis=1)
        out = jnp.dot(f1, rows.astype(BF16), preferred_element_type=F32)
        base = pl.multiple_of(n2 * k1p, 8)
        a_r[pl.ds(base, k1p), :] = out[:k1p, :LANE]
        a_i[pl.ds(base, k1p), :] = out[k1p:, :LANE]
        base1 = pl.multiple_of(base + k1p, 8)
        a_r[pl.ds(base1, k1p), :] = out[:k1p, LANE:]
        a_i[pl.ds(base1, k1p), :] = out[k1p:, LANE:]
        return carry

    lax.fori_loop(0, HY_N2 // 2, body, 0, unroll=8)


def _hy_stage2(a_r, a_i, gs_ref, k, k1p):
    ar = a_r[pl.ds(k, HY_N2, stride=k1p), :]
    ai = a_i[pl.ds(k, HY_N2, stride=k1p), :]
    rhs = jnp.concatenate([ar, ai], axis=1).astype(BF16)
    out = jnp.dot(gs_ref[k], rhs, preferred_element_type=F32)
    h = HY_N2
    return out[:h, :LANE] - out[h:, LANE:], out[:h, LANE:] + out[h:, :LANE]


def _hyena_conv_kernel(y_ref, g_ref, wy_ref, by_ref, wg_ref, bg_ref, bias_ref, hr_ref, hi_ref,
                       f1_ref, f1inv_ref, gs_ref, gts_ref, o_ref, ypad, zbuf, a_r, a_i, *, conv_y):
    n = y_ref.shape[1]
    nslab = n // HY_N2
    pitch = HY_N2 + HY_PAD
    k1 = gs_ref.shape[0]
    k1p = f1_ref.shape[0] // 2
    h = HY_N2

    def slab_conv(ref, i, w_ref, b_ref):
        r0 = pl.multiple_of(i * HY_N2, HY_N2)
        prev8 = ref[0, pl.ds(pl.multiple_of(jnp.maximum(r0 - 8, 0), 8), 8), :]
        next8 = ref[0, pl.ds(pl.multiple_of(jnp.minimum(r0 + HY_N2, n - 8), 8), 8), :]
        return _dwconv3_tile(ref[0, pl.ds(r0, HY_N2), :], prev8, next8, i == 0, i == nslab - 1, w_ref, b_ref)

    def y_slab(i):
        if conv_y:
            return slab_conv(y_ref, i, wy_ref, by_ref)
        return y_ref[0, pl.ds(pl.multiple_of(i * HY_N2, HY_N2), HY_N2), :]

    def fill(i, carry):
        ypad[pl.ds(pl.multiple_of(i * pitch, 8), HY_N2), :] = y_slab(i)
        return carry

    lax.fori_loop(0, nslab, fill, 0, unroll=2)
    _hy_stage1(ypad, f1_ref[...], a_r, a_i, nslab, k1p)

    def freq_fwd(k, carry):
        xr, xi = _hy_stage2(a_r, a_i, gs_ref, k, k1p)
        hr, hi = hr_ref[0, k], hi_ref[0, k]
        zbuf[k] = jnp.concatenate([xr * hr - xi * hi, xr * hi + xi * hr], axis=1).astype(BF16)
        return carry

    lax.fori_loop(0, k1, freq_fwd, 0, unroll=4)

    def freq_inv(k, carry):
        out = jnp.dot(gts_ref[k], zbuf[k], preferred_element_type=F32)
        a_r[pl.ds(k, HY_N2, stride=k1p), :] = out[:h, :LANE] + out[h:, LANE:]
        a_i[pl.ds(k, HY_N2, stride=k1p), :] = out[:h, LANE:] - out[h:, :LANE]
        return carry

    lax.fori_loop(0, k1, freq_inv, 0, unroll=4)
    f1inv = f1inv_ref[...][:nslab]

    def inv1(i, carry):
        n2 = 2 * i
        b0 = pl.multiple_of(n2 * k1p, 8)
        b1 = pl.multiple_of(b0 + k1p, 8)
        rhs = jnp.concatenate(
            [jnp.concatenate([a_r[pl.ds(b0, k1p), :], a_i[pl.ds(b0, k1p), :]], axis=0),
             jnp.concatenate([a_r[pl.ds(b1, k1p), :], a_i[pl.ds(b1, k1p), :]], axis=0)], axis=1).astype(BF16)
        out = jnp.dot(f1inv, rhs, preferred_element_type=F32)
        ypad[pl.ds(n2, nslab, stride=pitch), :] = out[:, :LANE]
        ypad[pl.ds(n2 + 1, nslab, stride=pitch), :] = out[:, LANE:]
        return carry

    lax.fori_loop(0, HY_N2 // 2, inv1, 0, unroll=8)

    def finish(i, carry):
        p0 = pl.multiple_of(i * pitch, 8)
        gate = slab_conv(g_ref, i, wg_ref, bg_ref)
        res = gate * (ypad[pl.ds(p0, HY_N2), :] + y_slab(i) * bias_ref[0])
        o_ref[0, pl.ds(pl.multiple_of(i * HY_N2, HY_N2), HY_N2), :] = res.astype(o_ref.dtype)
        return carry

    lax.fori_loop(0, nslab, finish, 0, unroll=2)


def _hyena_spectrum_kernel(f_ref, f1_ref, gs_ref, hr_ref, hi_ref, a_r, a_i, fpad):
    big = f_ref.shape[1]
    nslab = big // HY_N2
    pitch = HY_N2 + HY_PAD
    k1 = gs_ref.shape[0]
    k1p = f1_ref.shape[0] // 2

    def fill(i, carry):
        fpad[pl.ds(pl.multiple_of(i * pitch, 8), HY_N2), :] = f_ref[0, pl.ds(pl.multiple_of(i * HY_N2, HY_N2), HY_N2), :]
        return carry

    lax.fori_loop(0, nslab, fill, 0, unroll=2)
    _hy_stage1(fpad, f1_ref[...], a_r, a_i, nslab, k1p)

    def freq(k, carry):
        xr, xi = _hy_stage2(a_r, a_i, gs_ref, k, k1p)
        hr_ref[0, k] = xr
        hi_ref[0, k] = xi
        return carry

    lax.fori_loop(0, k1, freq, 0, unroll=4)


def _one(shape, index_map):
    return pl.BlockSpec(shape, index_map, pipeline_mode=pl.Buffered(1))


def hyena_spectrum(filt, tabs):
    f1, _, gs, _ = tabs
    r, big, w = filt.shape
    ns = w // LANE
    n1, k1, k1p = _hy_dims(big // 2)
    shp = jax.ShapeDtypeStruct((r * ns, k1, HY_N2, LANE), F32)
    spec_o = pl.BlockSpec((1, k1, HY_N2, LANE), lambda i, j: (i * ns + j, 0, 0, 0))
    return pl.pallas_call(
        _hyena_spectrum_kernel,
        out_shape=(shp, shp),
        grid=(r, ns),
        in_specs=[_one((1, big, LANE), lambda i, j: (i, 0, j)),
                  _one(f1.shape, lambda i, j: (0, 0)),
                  _one(gs.shape, lambda i, j: (0, 0, 0))],
        out_specs=(spec_o, spec_o),
        scratch_shapes=[pltpu.VMEM((HY_N2 * k1p, LANE), F32), pltpu.VMEM((HY_N2 * k1p, LANE), F32),
                        pltpu.VMEM((n1 * (HY_N2 + HY_PAD), LANE), F32)],
        compiler_params=_cparams("parallel", "parallel"),
        name="hyena_spectrum",
    )(filt, f1, gs)


def hyena_order(y, ycol, g, gcol, conv_w, conv_b, cy, cg, bias, hr, hi, order, tabs, conv_y, out_dtype):
    f1, f1inv, gs, gts = tabs
    bsz, n = y.shape[:2]
    w = bias.shape[-1]
    ns = w // LANE
    n1, k1, k1p = _hy_dims(n)
    f1d = f1[:, :n1 // 2]
    pitch = HY_N2 + HY_PAD
    nslab = n // HY_N2
    cw = lambda c: _one((3, LANE), lambda j, b: (0, c + j))
    cb = lambda c: _one((1, LANE), lambda j, b: (0, c + j))
    hspec = _one((1, k1, HY_N2, LANE), lambda j, b: (order * ns + j, 0, 0, 0))
    return pl.pallas_call(
        functools.partial(_hyena_conv_kernel, conv_y=conv_y),
        out_shape=jax.ShapeDtypeStruct((bsz, n, w), out_dtype),
        grid=(ns, bsz),
        in_specs=[_one((1, n, LANE), lambda j, b: (b, 0, ycol + j)),
                  _one((1, n, LANE), lambda j, b: (b, 0, gcol + j)),
                  cw(cy), cb(cy), cw(cg), cb(cg),
                  _one((1, 1, LANE), lambda j, b: (order, 0, j)),
                  hspec, hspec,
                  _one(f1d.shape, lambda j, b: (0, 0)), _one(f1inv.shape, lambda j, b: (0, 0)),
                  _one(gs.shape, lambda j, b: (0, 0, 0)), _one(gts.shape, lambda j, b: (0, 0, 0))],
        out_specs=pl.BlockSpec((1, n, LANE), lambda j, b: (b, 0, j)),
        scratch_shapes=[pltpu.VMEM((nslab * pitch, LANE), F32), pltpu.VMEM((k1, HY_N2, 2 * LANE), BF16),
                        pltpu.VMEM((HY_N2 * k1p, LANE), F32), pltpu.VMEM((HY_N2 * k1p, LANE), F32)],
        compiler_params=pltpu.CompilerParams(dimension_semantics=("parallel", "parallel"),
                                             vmem_limit_bytes=BIG_VMEM_LIMIT),
        name="hyena_order",
    )(y, g, conv_w, conv_b, conv_w, conv_b, bias.reshape(bias.shape[0], 1, w), hr, hi, f1d, f1inv, gs, gts)


def hyena_latent(px, c0, filt, p):
    w = p['hy_bias'].shape[-1]
    ns = w // LANE
    n = px.shape[1]
    tabs = _hy_tables(n)
    hr, hi = hyena_spectrum(filt, tabs)
    cw, cb = p['hy_conv_w'], p['hy_conv_b'].reshape(1, -1)
    b0 = c0 // LANE
    y1 = hyena_order(px, b0, px, b0 + ns, cw, cb, 0, ns, p['hy_bias'], hr, hi, 0, tabs, True, F32)
    return hyena_order(y1, 0, px, b0 + 2 * ns, cw, cb, 0, 2 * ns, p['hy_bias'], hr, hi, 1, tabs, False, BF16)


def _hyena_ctx_kernel(zv_ref, z1_ref, z2_ref, wv_ref, bv_ref, w1_ref, b1_ref, w2_ref, b2_ref, bias_ref,
                      filt_ref, ff_ref, finv_ref, o_ref):
    n = zv_ref.shape[1]
    hp = lax.Precision.HIGHEST
    zero8 = jnp.zeros((8, LANE), F32)
    conv = lambda ref, w, b: _dwconv3_tile(ref[0], zero8, zero8, True, True, w, b)
    ff = ff_ref[...]
    kp = ff.shape[0] // 2
    y = conv(zv_ref, wv_ref, bv_ref)
    for o, (g_ref, w, b) in enumerate(((z1_ref, w1_ref, b1_ref), (z2_ref, w2_ref, b2_ref))):
        hsp = jnp.dot(ff, filt_ref[o], preferred_element_type=F32, precision=hp)
        ysp = jnp.dot(ff[:, :n], y, preferred_element_type=F32, precision=hp)
        hr, hi, yr, yi = hsp[:kp], hsp[kp:], ysp[:kp], ysp[kp:]
        z = jnp.concatenate([yr * hr - yi * hi, yr * hi + yi * hr], axis=0)
        cv = jnp.dot(finv_ref[...], z, preferred_element_type=F32, precision=hp)
        y = conv(g_ref, w, b) * (cv + y * bias_ref[o])
    o_ref[0] = y.astype(o_ref.dtype)


def hyena_context(pc, c0, filt, p):
    bsz, n = pc.shape[:2]
    w = p['hy_bias'].shape[-1]
    ns = w // LANE
    kp = -(-(n + 1) // 8) * 8
    kk = jnp.arange(kp, dtype=jnp.int32)[:, None]
    mm = jnp.arange(2 * n, dtype=jnp.int32)[None, :]
    ang = (math.pi / n) * ((kk * mm) % (2 * n)).astype(F32)
    valid = (kk <= n).astype(F32)
    ff = jnp.concatenate([jnp.cos(ang) * valid, -jnp.sin(ang) * valid], axis=0)
    wk = jnp.where((kk == 0) | (kk == n), 1.0, 2.0) * valid / (2 * n)
    finv = jnp.concatenate([(jnp.cos(ang) * wk).T[:n], (-jnp.sin(ang) * wk).T[:n]], axis=1)
    b0 = c0 // LANE
    zs = lambda c: pl.BlockSpec((1, n, LANE), lambda j, b: (b, 0, b0 + c + j))
    cw = lambda c: pl.BlockSpec((3, LANE), lambda j, b: (0, c + j))
    cb = lambda c: pl.BlockSpec((1, LANE), lambda j, b: (0, c + j))
    conv_w, conv_b = p['hy_conv_w'], p['hy_conv_b'].reshape(1, -1)
    return pl.pallas_call(
        _hyena_ctx_kernel,
        out_shape=jax.ShapeDtypeStruct((bsz, n, w), BF16),
        grid=(ns, bsz),
        in_specs=[zs(0), zs(ns), zs(2 * ns), cw(0), cb(0), cw(ns), cb(ns), cw(2 * ns), cb(2 * ns),
                  pl.BlockSpec((2, 1, LANE), lambda j, b: (0, 0, j)),
                  pl.BlockSpec((2, 2 * n, LANE), lambda j, b: (0, 0, j)),
                  pl.BlockSpec(ff.shape, lambda j, b: (0, 0)), pl.BlockSpec(finv.shape, lambda j, b: (0, 0))],
        out_specs=pl.BlockSpec((1, n, LANE), lambda j, b: (b, 0, j)),
        compiler_params=_cparams("parallel", "parallel"),
        name="hyena_context",
    )(pc, pc, pc, conv_w, conv_b, conv_w, conv_b, conv_w, conv_b, p['hy_bias'].reshape(2, 1, w), filt, ff, finv)


def _to_col_major(t):
    bsz, n = t.shape[:2]
    rows = n // GRID_W
    return t.reshape(bsz, rows, GRID_W, *t.shape[2:]).swapaxes(1, 2).reshape(bsz, n, *t.shape[2:])


def _from_col_major(t):
    bsz, n = t.shape[:2]
    rows = n // GRID_W
    return t.reshape(bsz, GRID_W, rows, *t.shape[2:]).swapaxes(1, 2).reshape(bsz, n, *t.shape[2:])


def _filter_positions(n):
    r = jnp.arange(2 * n, dtype=jnp.int32)
    return jnp.where(r < n, r, 2 * n - r).astype(F32)


def _filter_feats(n):
    pos = _filter_positions(n)
    t = pos / max(n - 1, 1)
    freqs = jnp.linspace(1e-4, HY_POS_FREQS - 1, HY_POS_FREQS, dtype=F32)
    ang = (2.0 * math.pi / n) * pos[:, None] * freqs[None]
    feats = jnp.concatenate([t[:, None], jnp.cos(ang), -jnp.sin(ang)], axis=-1)
    return jnp.pad(feats, ((0, 0), (0, LANE - feats.shape[1])))


def _filter_hidden_kernel(f_ref, w1_ref, b1_ref, s1_ref, w2_ref, b2_ref, s2_ref, o_ref):
    hp = lax.Precision.HIGHEST
    h = jnp.sin(s1_ref[...] * (jnp.dot(f_ref[...], w1_ref[...], preferred_element_type=F32, precision=hp)
                               + b1_ref[...]))
    o_ref[...] = jnp.sin(s2_ref[...] * (jnp.dot(h, w2_ref[...], preferred_element_type=F32, precision=hp)
                                        + b2_ref[...]))


def _filter_out_kernel(h_ref, wf_ref, wb_ref, bf_ref, bb_ref, rate_ref, o_ref, *, n, chunk):
    nchunk = 2 * n // chunk
    inv_span = 1.0 / max(n - 1, 1)

    def emit(c, acc):
        r0 = pl.multiple_of(c * chunk, chunk)
        past = c < nchunk // 2
        w = jnp.where(past, wf_ref[...], wb_ref[...])
        bias = jnp.where(past, bf_ref[...], bb_ref[...])
        hr0 = pl.multiple_of((c % (nchunk // 2)) * chunk, chunk)
        val = jnp.dot(h_ref[pl.ds(hr0, chunk), :], w, preferred_element_type=F32,
                      precision=lax.Precision.HIGHEST) + bias
        r = r0 + lax.broadcasted_iota(jnp.int32, (chunk, LANE), 0)
        t = jnp.where(r < n, r, 2 * n - r).astype(F32) * inv_span
        val = jnp.where(r == n, 0.0, val * jnp.exp(-t * rate_ref[...]))
        o_ref[0, pl.ds(r0, chunk), :] = val
        return acc + jnp.sum(jnp.abs(val), axis=0, keepdims=True)

    total = lax.fori_loop(0, nchunk, emit, jnp.zeros((1, LANE), F32))
    scale = 1.0 / (total + EPS)

    def rescale(c, carry):
        r0 = pl.multiple_of(c * chunk, chunk)
        o_ref[0, pl.ds(r0, chunk), :] = o_ref[0, pl.ds(r0, chunk), :] * scale
        return carry

    lax.fori_loop(0, nchunk, rescale, 0)


def hyena_filters(n, p):
    order, width = p['hy_bias'].shape
    ns = width // LANE
    hid = p['hy_f_w2'].shape[0]
    half = LANE // 2
    assert hid <= half
    feats = _filter_feats(n)
    feats = jnp.concatenate([feats[:n], feats[n:]], axis=1)
    zpad = lambda a, rows, cols: jnp.pad(a, ((0, rows - a.shape[0]), (0, cols - a.shape[1])))
    diag2 = lambda a: jnp.concatenate([jnp.pad(a, ((0, 0), (0, a.shape[1]))),
                                       jnp.pad(a, ((0, 0), (a.shape[1], 0)))], axis=0)
    twice = lambda a: jnp.tile(zpad(a.reshape(1, -1), 1, half), (1, 2))
    w1 = diag2(zpad(p['hy_f_w1'], LANE, half))
    w2 = diag2(zpad(p['hy_f_w2'], half, half))
    w3 = zpad(p['hy_f_w3'], half, p['hy_f_w3'].shape[1])
    w3_past = jnp.pad(w3, ((0, half), (0, 0)))
    w3_future = jnp.pad(w3, ((half, 0), (0, 0)))
    tr = _pick_tile(n, 2048, 8)
    full = lambda i: (0, 0)
    hidden = pl.pallas_call(
        _filter_hidden_kernel,
        out_shape=jax.ShapeDtypeStruct((n, LANE), F32),
        grid=(n // tr,),
        in_specs=[pl.BlockSpec((tr, 2 * LANE), lambda i: (i, 0)),
                  pl.BlockSpec((2 * LANE, LANE), full), pl.BlockSpec((1, LANE), full), pl.BlockSpec((1, LANE), full),
                  pl.BlockSpec((LANE, LANE), full), pl.BlockSpec((1, LANE), full), pl.BlockSpec((1, LANE), full)],
        out_specs=pl.BlockSpec((tr, LANE), lambda i: (i, 0)),
        compiler_params=_cparams("parallel"),
        name="hyena_filter_hidden",
    )(feats, w1, twice(p['hy_f_b1']), twice(p['hy_f_freq1']), w2, twice(p['hy_f_b2']), twice(p['hy_f_freq2']))
    rates = jnp.abs(jnp.linspace(math.log(HY_DECAY_TARGET) / HY_LONG_DECAY_PCT,
                                 math.log(HY_DECAY_TARGET) / HY_SHORT_DECAY_PCT, width, dtype=F32)).reshape(1, width)
    b3 = p['hy_f_b3'].reshape(1, -1)
    return pl.pallas_call(
        functools.partial(_filter_out_kernel, n=n, chunk=min(1024, n)),
        out_shape=jax.ShapeDtypeStruct((order, 2 * n, width), F32),
        grid=(order, ns),
        in_specs=[_one((n, LANE), lambda o, j: (0, 0)),
                  pl.BlockSpec((LANE, LANE), lambda o, j: (0, 2 * o * ns + j)),
                  pl.BlockSpec((LANE, LANE), lambda o, j: (0, (2 * o + 1) * ns + j)),
                  pl.BlockSpec((1, LANE), lambda o, j: (0, 2 * o * ns + j)),
                  pl.BlockSpec((1, LANE), lambda o, j: (0, (2 * o + 1) * ns + j)),
                  pl.BlockSpec((1, LANE), lambda o, j: (0, j))],
        out_specs=pl.BlockSpec((1, 2 * n, LANE), lambda o, j: (o, 0, j)),
        compiler_params=_cparams("parallel", "parallel"),
        name="hyena_filter_out",
    )(hidden, w3_past, w3_future, b3, b3, rates)


def kernel(x, c, ctx, c_ctx, w_mod, b_mod, norm1_g, norm2_g, w_in, s5_a_re, s5_a_im, s5_log_step, s5_b_re, s5_b_im, s5_c_re, s5_c_im, s5_d, s5_glu_w, s5_glu_b, hy_conv_w, hy_conv_b, hy_f_w1, hy_f_b1, hy_f_freq1, hy_f_w2, hy_f_b2, hy_f_freq2, hy_f_w3, hy_f_b3, hy_bias, gla_wg, gla_bg, gla_norm_g, w_branch, w_out, ff_w_up, ff_conv_w, ff_conv_b, ff_w_down, final_norm_g):
    bsz, seq, d = x.shape
    nctx = ctx.shape[1]
    depth = w_mod.shape[0]
    q4 = d // 4
    rank2 = 2 * gla_wg.shape[2]
    gpad = max(LANE, q4 // 2)
    assert bsz + 1 <= 8

    c_gk = q4
    c_gv = 2 * q4
    c_gg = 4 * q4
    c_gq = c_gg + rank2
    c_gr = c_gq + q4
    c_hy = c_gr + 2 * q4
    c_mg = c_hy + 3 * q4

    rows = jnp.zeros((8, d), F32).at[:bsz].set(c).at[bsz].set(c_ctx)
    mod = modulation(rows, w_mod, b_mod)

    xs = x.reshape(bsz * seq, d)
    cs = ctx.reshape(bsz * nctx, d)
    for l in range(depth):
        ctx_out = l < depth - 1
        sh1, s1, g1, sh2, s2, g2 = [mod[l, :, i * d:(i + 1) * d][:, None, :] for i in range(6)]
        gs1 = norm1_g[l] * (1.0 + s1)
        gs2 = norm2_g[l] * (1.0 + s2)
        bx = slice(0, bsz)
        bc = slice(bsz, bsz + 1)

        wl = w_in[l]
        w_pack = jnp.concatenate([
            wl[:, 0:c_gk], wl[:, c_hy:c_mg], wl[:, c_gr:c_hy], wl[:, c_gk:c_gv], wl[:, c_gq:c_gr],
            wl[:, c_gv:c_gg], wl[:, c_gg:c_gq],
            jnp.zeros((d, gpad - rank2), F32)], axis=1).astype(BF16)
        w_gate = wl[:, c_mg:].reshape(d, 3, d).swapaxes(0, 1).astype(BF16)
        wb = w_branch[l].astype(BF16)
        w_o = w_out[l].astype(BF16)
        w_up = cast_col_blocks(ff_w_up[l], _pick_tile(ff_w_down.shape[1], FFN_TF))
        w_dn = ff_w_down[l].astype(BF16)

        p = dict(s5_a_re=s5_a_re[l], s5_a_im=s5_a_im[l], s5_log_step=s5_log_step[l],
                 s5_b_re=s5_b_re[l], s5_b_im=s5_b_im[l], s5_c_re=s5_c_re[l], s5_c_im=s5_c_im[l],
                 s5_d=s5_d[l], s5_glu_w=s5_glu_w[l], s5_glu_b=s5_glu_b[l],
                 hy_conv_w=hy_conv_w[l], hy_conv_b=hy_conv_b[l], hy_f_w1=hy_f_w1[l], hy_f_b1=hy_f_b1[l],
                 hy_f_freq1=hy_f_freq1[l], hy_f_w2=hy_f_w2[l], hy_f_b2=hy_f_b2[l],
                 hy_f_freq2=hy_f_freq2[l], hy_f_w3=hy_f_w3[l], hy_f_b3=hy_f_b3[l], hy_bias=hy_bias[l],
                 gla_wg=gla_wg[l], gla_bg=gla_bg[l], gla_norm_g=gla_norm_g[l])

        px = norm_matmul(xs, gs1[bx], sh1[bx], w_pack).reshape(bsz, seq, -1)
        pc = norm_matmul(cs, gs1[bc], sh1[bc], w_pack).reshape(bsz, nctx, -1)

        ya_c, ya_x = s5_mixer(pc, px, q4, p)
        oc_c, oc_x = gla_mixer(pc, px, 6 * q4, q4, gpad, p)
        yb_x = hyena_latent(px, q4, hyena_filters(seq, p), p)

        flat = lambda t: t.reshape(-1, t.shape[-1])
        mx = merge(xs, gs1[bx], sh1[bx], flat(ya_x), flat(yb_x), flat(oc_x), w_gate, wb)
        xs = matmul_residual(mx, w_o, xs, g1[bx])
        xs = ffn(xs, bsz, gs2[bx], sh2[bx], w_up, ff_conv_w[l], ff_conv_b[l], w_dn, g2[bx])

        if ctx_out:
            yb_c = hyena_context(pc, q4, hyena_filters(nctx, p), p)
            mc = merge(cs, gs1[bc], sh1[bc], flat(ya_c), flat(yb_c), flat(oc_c), w_gate, wb)
            cs = matmul_residual(mc, w_o, cs, g1[bc])
            cs = ffn(cs, bsz, gs2[bc], sh2[bc], w_up, ff_conv_w[l], ff_conv_b[l], w_dn, g2[bc])

    return final_rmsnorm(xs, final_norm_g).reshape(bsz, seq, d)
```

```python
import functools
import math

import jax
import jax.numpy as jnp
from jax import lax
from jax.experimental import pallas as pl
from jax.experimental.pallas import tpu as pltpu

F32 = jnp.float32
BF16 = jnp.bfloat16
EPS = 1e-6
GRID_W = 64
S5_GROUP = 16
GLA_CHUNK = 64
GLA_GATE_TEMP = 16.0
HY_POS_FREQS = 16
HY_DECAY_TARGET = 1e-2
HY_SHORT_DECAY_PCT = 0.3
HY_LONG_DECAY_PCT = 1.5
LANE = 128
VMEM_LIMIT = 56 * 1024 * 1024
BIG_VMEM_LIMIT = 60 * 1024 * 1024
FFN_TF = 512


def _pick_tile(n, cap, mult=LANE):
    best = None
    for t in range(mult, min(n, cap) + 1, mult):
        if n % t == 0:
            best = t
    assert best is not None, (n, cap, mult)
    return best


def _col_blocks(w, tn):
    nj = w.shape[-1] // tn
    return jnp.moveaxis(w.reshape(*w.shape[:-1], nj, tn), -2, 0)


def _cast_blocks_kernel(w_ref, o_ref):
    o_ref[0] = w_ref[...].astype(o_ref.dtype)


def cast_col_blocks(w, tn):
    k, n = w.shape
    return pl.pallas_call(
        _cast_blocks_kernel,
        out_shape=jax.ShapeDtypeStruct((n // tn, k, tn), BF16),
        grid=(n // tn,),
        in_specs=[pl.BlockSpec((k, tn), lambda j: (0, j))],
        out_specs=pl.BlockSpec((1, k, tn), lambda j: (j, 0, 0)),
        compiler_params=pltpu.CompilerParams(dimension_semantics=("parallel",), vmem_limit_bytes=VMEM_LIMIT),
        name="cast_col_blocks",
    )(w)


def _cparams(*sem):
    return pltpu.CompilerParams(dimension_semantics=sem, vmem_limit_bytes=VMEM_LIMIT)


def _mod_kernel(r_ref, w_ref, b_ref, o_ref):
    r = r_ref[...]
    s = r * jax.nn.sigmoid(r)
    o_ref[0] = jnp.dot(s, w_ref[0], preferred_element_type=F32,
                       precision=lax.Precision.HIGHEST) + b_ref[0]


def modulation(rows, w_mod, b_mod):
    depth, d, n = w_mod.shape
    tn = _pick_tile(n, 1024)
    return pl.pallas_call(
        _mod_kernel,
        out_shape=jax.ShapeDtypeStruct((depth, 8, n), F32),
        grid=(depth, n // tn),
        in_specs=[pl.BlockSpec((8, d), lambda l, j: (0, 0)),
                  pl.BlockSpec((1, d, tn), lambda l, j: (l, 0, j)),
                  pl.BlockSpec((1, 1, tn), lambda l, j: (l, 0, j))],
        out_specs=pl.BlockSpec((1, 8, tn), lambda l, j: (l, 0, j)),
        compiler_params=_cparams("parallel", "parallel"),
        name="modulation",
    )(rows, w_mod, b_mod.reshape(depth, 1, n))


def _norm_mm_kernel(x_ref, gs_ref, sh_ref, w_ref, o_ref, h_ref):
    @pl.when(pl.program_id(1) == 0)
    def _():
        x = x_ref[...]
        ms = jnp.mean(x * x, axis=-1, keepdims=True)
        h_ref[...] = (x * lax.rsqrt(ms + EPS) * gs_ref[0] + sh_ref[0]).astype(BF16)

    o_ref[...] = jnp.dot(h_ref[...], w_ref[0], preferred_element_type=F32).astype(o_ref.dtype)


def norm_matmul(x, gs, sh, w, out_dtype=F32):
    m, d = x.shape
    n = w.shape[1]
    nb = gs.shape[0]
    tm = _pick_tile(m // nb, 1024, 8)
    tn = _pick_tile(n, 2048)
    tpb = (m // nb) // tm
    return pl.pallas_call(
        _norm_mm_kernel,
        out_shape=jax.ShapeDtypeStruct((m, n), out_dtype),
        grid=(m // tm, n // tn),
        in_specs=[pl.BlockSpec((tm, d), lambda i, j: (i, 0)),
                  pl.BlockSpec((1, 1, d), lambda i, j: (i // tpb, 0, 0)),
                  pl.BlockSpec((1, 1, d), lambda i, j: (i // tpb, 0, 0)),
                  pl.BlockSpec((1, d, tn), lambda i, j: (j, 0, 0))],
        out_specs=pl.BlockSpec((tm, tn), lambda i, j: (i, j)),
        scratch_shapes=[pltpu.VMEM((tm, d), BF16)],
        compiler_params=_cparams("parallel", "arbitrary"),
        name="norm_matmul",
    )(x, gs, sh, _col_blocks(w, tn))


def _mm_res_kernel(a_ref, w_ref, r_ref, g_ref, o_ref):
    acc = jnp.dot(a_ref[...], w_ref[0], preferred_element_type=F32)
    o_ref[...] = r_ref[...] + g_ref[0] * acc


def matmul_residual(a, w, res, g):
    m, k = a.shape
    n = w.shape[1]
    nb = g.shape[0]
    tm = _pick_tile(m // nb, 1024, 8)
    tn = _pick_tile(n, 2048)
    tpb = (m // nb) // tm
    resident = pl.Buffered(1) if n == tn else None
    return pl.pallas_call(
        _mm_res_kernel,
        out_shape=jax.ShapeDtypeStruct((m, n), F32),
        grid=(m // tm, n // tn),
        in_specs=[pl.BlockSpec((tm, k), lambda i, j: (i, 0)),
                  pl.BlockSpec((1, k, tn), lambda i, j: (j, 0, 0), pipeline_mode=resident),
                  pl.BlockSpec((tm, tn), lambda i, j: (i, j)),
                  pl.BlockSpec((1, 1, tn), lambda i, j: (i // tpb, 0, j))],
        out_specs=pl.BlockSpec((tm, tn), lambda i, j: (i, j)),
        compiler_params=_cparams("parallel", "arbitrary"),
        name="matmul_residual",
    )(a, _col_blocks(w, tn), res, g)


def _merge_kernel(x_ref, gs_ref, sh_ref, ya_ref, yb_ref, yc_ref, wg_ref, wb_ref, o_ref, h_ref):
    @pl.when(pl.program_id(1) == 0)
    def _():
        x = x_ref[...]
        ms = jnp.mean(x * x, axis=-1, keepdims=True)
        h_ref[...] = (x * lax.rsqrt(ms + EPS) * gs_ref[0] + sh_ref[0]).astype(BF16)

    h = h_ref[...]
    m = None
    lo = 0
    for i, y_ref in enumerate((ya_ref, yb_ref, yc_ref)):
        hi = lo + y_ref.shape[1]
        gate = jax.nn.sigmoid(jnp.dot(h, wg_ref[0, i], preferred_element_type=F32))
        br = jnp.dot(y_ref[...], wb_ref[0, lo:hi, :], preferred_element_type=F32)
        m = gate * br if m is None else m + gate * br
        lo = hi
    o_ref[...] = m.astype(o_ref.dtype)


def merge(x, gs, sh, ya, yb, yc, wg, wb):
    m, d = x.shape
    nb = gs.shape[0]
    tm = _pick_tile(m // nb, 1024, 8)
    tn = _pick_tile(d, 256)
    tpb = (m // nb) // tm
    row = lambda i, j: (i, 0)
    wy = wb.shape[0]
    return pl.pallas_call(
        _merge_kernel,
        out_shape=jax.ShapeDtypeStruct((m, d), BF16),
        grid=(m // tm, d // tn),
        in_specs=[pl.BlockSpec((tm, d), row),
                  pl.BlockSpec((1, 1, d), lambda i, j: (i // tpb, 0, 0)),
                  pl.BlockSpec((1, 1, d), lambda i, j: (i // tpb, 0, 0)),
                  pl.BlockSpec((tm, ya.shape[1]), row),
                  pl.BlockSpec((tm, yb.shape[1]), row),
                  pl.BlockSpec((tm, yc.shape[1]), row),
                  pl.BlockSpec((1, 3, d, tn), lambda i, j: (j, 0, 0, 0)),
                  pl.BlockSpec((1, wy, tn), lambda i, j: (j, 0, 0))],
        out_specs=pl.BlockSpec((tm, tn), lambda i, j: (i, j)),
        scratch_shapes=[pltpu.VMEM((tm, d), BF16)],
        compiler_params=_cparams("parallel", "arbitrary"),
        name="merge",
    )(x, gs, sh, ya, yb, yc, _col_blocks(wg, tn), _col_blocks(wb, tn))


def _dwconv3_tile(a, prev8, next8, first, last, w_ref, cb_ref):
    tr = a.shape[0]
    row = lax.broadcasted_iota(jnp.int32, a.shape, 0)
    before = jnp.where(first, 0.0, prev8[7:8, :])
    after = jnp.where(last, 0.0, next8[0:1, :])
    prev = jnp.where(row == 0, before, pltpu.roll(a, 1, 0))
    nxt = jnp.where(row == tr - 1, after, pltpu.roll(a, tr - 1, 0))
    return prev * w_ref[0:1, :] + a * w_ref[1:2, :] + nxt * w_ref[2:3, :] + cb_ref[...]


def _ffn_kernel(x_ref, xp_ref, xn_ref, gs_ref, sh_ref, wa_ref, wb_ref, cw_ref, cb_ref, wd_ref, g_ref,
                o_ref, h_ref, hh_ref, *, tpb):
    i, f = pl.program_id(0), pl.program_id(1)

    def normed(x):
        ms = jnp.mean(x * x, axis=-1, keepdims=True)
        return (x * lax.rsqrt(ms + EPS) * gs_ref[0] + sh_ref[0]).astype(BF16)

    @pl.when(f == 0)
    def _():
        h_ref[...] = normed(x_ref[...])
        hh_ref[0:8, :] = normed(xp_ref[...])
        hh_ref[8:16, :] = normed(xn_ref[...])
        o_ref[...] = jnp.zeros_like(o_ref)

    wa = wa_ref[0]
    a = jnp.dot(h_ref[...], wa, preferred_element_type=F32)
    ah = jnp.dot(hh_ref[...], wa, preferred_element_type=F32)
    b = jnp.dot(h_ref[...], wb_ref[0], preferred_element_type=F32)
    pos = i % tpb
    conv = _dwconv3_tile(a, ah[0:8], ah[8:16], pos == 0, pos == tpb - 1, cw_ref, cb_ref)
    act = (conv * jax.nn.sigmoid(conv) * b).astype(BF16)
    half = o_ref.shape[1] // 2
    for lo in (0, half):
        o_ref[:, lo:lo + half] += jnp.dot(act, wd_ref[:, lo:lo + half], preferred_element_type=F32)

    @pl.when(f == pl.num_programs(1) - 1)
    def _():
        o_ref[...] = x_ref[...] + g_ref[0] * o_ref[...]


def ffn(x, nseq, gs, sh, w_up, conv_w, conv_b, w_dn, g):
    m, d = x.shape
    fh = w_dn.shape[0]
    nb = gs.shape[0]
    tm = _pick_tile(m // nseq, 1024, 8)
    tf = w_up.shape[2]
    tpb = (m // nseq) // tm
    tpm = (m // nb) // tm
    nf = fh // tf
    r8 = tm // 8
    last8 = m // 8 - 1
    mod = lambda i, f: (i // tpm, 0, 0)
    return pl.pallas_call(
        functools.partial(_ffn_kernel, tpb=tpb),
        out_shape=jax.ShapeDtypeStruct((m, d), F32),
        grid=(m // tm, nf),
        in_specs=[pl.BlockSpec((tm, d), lambda i, f: (i, 0), pipeline_mode=pl.Buffered(1)),
                  pl.BlockSpec((8, d), lambda i, f: (jnp.maximum(i * r8 - 1, 0), 0)),
                  pl.BlockSpec((8, d), lambda i, f: (jnp.minimum((i + 1) * r8, last8), 0)),
                  pl.BlockSpec((1, 1, d), mod), pl.BlockSpec((1, 1, d), mod),
                  pl.BlockSpec((1, d, tf), lambda i, f: (f, 0, 0)),
                  pl.BlockSpec((1, d, tf), lambda i, f: (f + nf, 0, 0)),
                  pl.BlockSpec((3, tf), lambda i, f: (0, f)),
                  pl.BlockSpec((1, tf), lambda i, f: (0, f)),
                  pl.BlockSpec((tf, d), lambda i, f: (f, 0)),
                  pl.BlockSpec((1, 1, d), mod)],
        out_specs=pl.BlockSpec((tm, d), lambda i, f: (i, 0)),
        scratch_shapes=[pltpu.VMEM((tm, d), BF16), pltpu.VMEM((16, d), BF16)],
        compiler_params=pltpu.CompilerParams(dimension_semantics=("parallel", "arbitrary"),
                                             vmem_limit_bytes=BIG_VMEM_LIMIT),
        name="ffn",
    )(x, x, x, gs, sh, w_up, w_up, conv_w, conv_b.reshape(1, fh), w_dn, g)


def _rmsnorm_kernel(x_ref, g_ref, o_ref):
    x = x_ref[...]
    ms = jnp.mean(x * x, axis=-1, keepdims=True)
    o_ref[...] = x * lax.rsqrt(ms + EPS) * g_ref[...]


def final_rmsnorm(x, g):
    m, d = x.shape
    tm = _pick_tile(m, 1024, 8)
    return pl.pallas_call(
        _rmsnorm_kernel,
        out_shape=jax.ShapeDtypeStruct((m, d), F32),
        grid=(m // tm,),
        in_specs=[pl.BlockSpec((tm, d), lambda i: (i, 0)), pl.BlockSpec((1, d), lambda i: (0, 0))],
        out_specs=pl.BlockSpec((tm, d), lambda i: (i, 0)),
        compiler_params=_cparams("parallel"),
        name="final_rmsnorm",
    )(x, g.reshape(1, d))


S5_CHUNK = 256


def _const_spec(shape):
    zeros = (0,) * len(shape)
    return pl.BlockSpec(shape, lambda b, k: zeros, pipeline_mode=pl.Buffered(1))


def _s5_kernel(*refs, ncc, reverse, finish):
    if finish:
        (uc_ref, ux_ref, bblk_ref, cblk_ref, enr_ref, eni_ref, epr_ref, epi_ref, ac_ref,
         pc_ref, px_ref, d_ref, gw_ref, gb_ref, yc_ref, yx_ref, h_ref) = refs
    else:
        (uc_ref, ux_ref, bblk_ref, cblk_ref, enr_ref, eni_ref, epr_ref, epi_ref, ac_ref,
         yc_ref, yx_ref, h_ref) = refs
    k = pl.program_id(1)
    t = uc_ref.shape[1]
    gn = enr_ref.shape[1]

    @pl.when(k == 0)
    def _():
        h_ref[...] = jnp.zeros_like(h_ref)

    is_ctx = k < ncc
    u = jnp.where(is_ctx, uc_ref[0], ux_ref[0])
    row = lax.broadcasted_iota(jnp.int32, (t, t), 0)
    col = lax.broadcasted_iota(jnp.int32, (t, t), 1)
    tri = jnp.where((col >= row) if reverse else (col <= row), 1.0, 0.0).astype(BF16)
    last = 0 if reverse else t - 1
    nblk, wb, sb2 = bblk_ref.shape
    sb = sb2 // 2
    ys = []
    for j in range(nblk):
        sc = slice(j * sb, (j + 1) * sb)
        bu = jnp.dot(u[:, j * wb:(j + 1) * wb].astype(BF16), bblk_ref[j], preferred_element_type=F32)
        br, bi = bu[:, :sb], bu[:, sb:]
        enr, eni = enr_ref[:, sc], eni_ref[:, sc]
        z = jnp.concatenate([br * enr - bi * eni, br * eni + bi * enr], axis=1).astype(BF16)
        cs = jnp.dot(tri, z, preferred_element_type=F32)
        hr, hi = h_ref[0:1, sc], h_ref[1:2, sc]
        acr, aci = ac_ref[0:1, sc], ac_ref[1:2, sc]
        sr = cs[:, :sb] + (hr * acr - hi * aci)
        si = cs[:, sb:] + (hr * aci + hi * acr)
        epr, epi = epr_ref[:, sc], epi_ref[:, sc]
        xr = sr * epr - si * epi
        xi = sr * epi + si * epr
        h_ref[0:1, sc] = xr[last:last + 1, :]
        h_ref[1:2, sc] = xi[last:last + 1, :]
        xs = jnp.concatenate([xr, xi], axis=1).astype(BF16)
        ys.append(jnp.dot(xs, cblk_ref[j], preferred_element_type=F32))
    y = ys[0] if nblk == 1 else jnp.concatenate(ys, axis=1)
    if finish:
        y = y + jnp.where(is_ctx, pc_ref[0], px_ref[0]) + d_ref[...] * u
        y = jax.nn.gelu(y)
        gate = jnp.dot(y.astype(BF16), gw_ref[...], preferred_element_type=F32) + gb_ref[...]
        y = y * jax.nn.sigmoid(gate)

    @pl.when(is_ctx)
    def _():
        yc_ref[0] = y.astype(yc_ref.dtype)

    @pl.when(jnp.logical_not(is_ctx))
    def _():
        yx_ref[0] = y.astype(yx_ref.dtype)


def _s5_tables(a_re, a_im, log_step, b_re, b_im, c_re, c_im, t, reverse):
    g, n, h = b_re.shape
    gpb = min(g, max(1, LANE // h))
    nblk = g // gpb
    eye = jnp.eye(gpb, dtype=F32)
    dt = jnp.exp(log_step)[:, None]
    ldr, ldi = a_re * dt, a_im * dt
    mag = jnp.exp(ldr)
    abr, abi = mag * jnp.cos(ldi), mag * jnp.sin(ldi)
    den = a_re * a_re + a_im * a_im
    nr, ni = abr - 1.0, abi
    fr = (nr * a_re + ni * a_im) / den
    fi = (ni * a_re - nr * a_im) / den
    bbr = fr[..., None] * b_re - fi[..., None] * b_im
    bbi = fr[..., None] * b_im + fi[..., None] * b_re
    blk_b = lambda m: jnp.einsum('jgnh,gk->jghkn', m.reshape(nblk, gpb, n, h), eye).reshape(nblk, gpb * h, gpb * n)
    bblk = jnp.concatenate([blk_b(bbr), blk_b(bbi)], axis=2).astype(BF16)
    blk_c = lambda m: jnp.einsum('jghn,gk->jgnkh', m.reshape(nblk, gpb, h, n), eye).reshape(nblk, gpb * n, gpb * h)
    cblk = jnp.concatenate([blk_c(c_re), -blk_c(c_im)], axis=1).astype(BF16)
    centre = float(t // 2)
    pos = jnp.arange(t, dtype=F32)[:, None]
    steps = ((t - pos) if reverse else (pos + 1.0)) - centre
    lr, li = ldr.reshape(1, g * n), ldi.reshape(1, g * n)
    er, ei = steps * lr, steps * li
    epr, epi = jnp.exp(er) * jnp.cos(ei), jnp.exp(er) * jnp.sin(ei)
    enr, eni = jnp.exp(-er) * jnp.cos(ei), -jnp.exp(-er) * jnp.sin(ei)
    ac = jnp.concatenate([jnp.exp(centre * lr) * jnp.cos(centre * li),
                          jnp.exp(centre * lr) * jnp.sin(centre * li)], axis=0)
    return bblk, cblk, (enr, eni, epr, epi, ac)


def _s5_pass(pc, px, col, width, tabs, reverse, fin=None, out_dtype=F32):
    bsz, nctx = pc.shape[:2]
    seq = px.shape[1]
    t = S5_CHUNK
    ncc, ncx = nctx // t, seq // t
    bblk, cblk, (enr, eni, epr, epi, ac) = tabs
    gn = enr.shape[1]
    if reverse:
        cidx = lambda k: jnp.maximum(ncc - 1 - k, 0)
        xidx = lambda k: jnp.minimum(ncx - 1 - (k - ncc), ncx - 1)
    else:
        cidx = lambda k: jnp.minimum(k, ncc - 1)
        xidx = lambda k: jnp.maximum(k - ncc, 0)
    in_specs = [pl.BlockSpec((1, t, width), lambda b, k: (b, cidx(k), col)),
                pl.BlockSpec((1, t, width), lambda b, k: (b, xidx(k), col)),
                _const_spec(bblk.shape), _const_spec(cblk.shape),
                _const_spec(enr.shape), _const_spec(eni.shape), _const_spec(epr.shape), _const_spec(epi.shape),
                _const_spec(ac.shape)]
    args = [pc, px, bblk, cblk, enr, eni, epr, epi, ac]
    if fin is not None:
        prev_c, prev_x, dvec, gw, gb = fin
        in_specs += [pl.BlockSpec((1, t, width), lambda b, k: (b, cidx(k), 0)),
                     pl.BlockSpec((1, t, width), lambda b, k: (b, xidx(k), 0)),
                     _const_spec((1, width)), _const_spec(gw.shape), _const_spec((1, width))]
        args += [prev_c, prev_x, dvec.reshape(1, width), gw, gb.reshape(1, width)]
    return pl.pallas_call(
        functools.partial(_s5_kernel, ncc=ncc, reverse=reverse, finish=fin is not None),
        out_shape=(jax.ShapeDtypeStruct((bsz, nctx, width), out_dtype),
                   jax.ShapeDtypeStruct((bsz, seq, width), out_dtype)),
        grid=(bsz, ncc + ncx),
        in_specs=in_specs,
        out_specs=(pl.BlockSpec((1, t, width), lambda b, k: (b, cidx(k), 0)),
                   pl.BlockSpec((1, t, width), lambda b, k: (b, xidx(k), 0))),
        scratch_shapes=[pltpu.VMEM((2, gn), F32)],
        compiler_params=_cparams("parallel", "arbitrary"),
        name="s5_rev" if reverse else "s5_fwd",
    )(*args)


def s5_mixer(pc, px, width, p):
    tabs = [_s5_tables(p['s5_a_re'][d], p['s5_a_im'][d], p['s5_log_step'][d], p['s5_b_re'][d], p['s5_b_im'][d],
                       p['s5_c_re'][d], p['s5_c_im'][d], S5_CHUNK, d == 1) for d in range(2)]
    bc, bx = _s5_pass(pc, px, 0, width, tabs[1], reverse=True)
    return _s5_pass(pc, px, 0, width, tabs[0], reverse=False,
                    fin=(bc, bx, p['s5_d'], p['s5_glu_w'].astype(BF16), p['s5_glu_b']), out_dtype=BF16)


GLA_STEP = 256
GLA_WBLK = 8


def _gla_kernel(*refs, nh, dk, dv, ncs, nxs, cps, reverse, finish, scale):
    if finish:
        (kc_ref, qc_ref, vc_ref, lc_ref, kx_ref, qx_ref, vx_ref, lx_ref, wg_ref, bg_ref, pc_ref, px_ref, ng_ref,
         rc_ref, rx_ref, oc_ref, ox_ref, s_ref, o_scr) = refs
    else:
        (kc_ref, qc_ref, vc_ref, lc_ref, kx_ref, qx_ref, vx_ref, lx_ref, wg_ref, bg_ref,
         oc_ref, ox_ref, s_ref, o_scr) = refs
    ts = kc_ref.shape[1]
    nchunk = ts // GLA_CHUNK
    step = pl.program_id(1)
    is_ctx = step < ncs
    xs = jnp.clip(nxs - 1 - (step - ncs), 0, nxs - 1) if reverse else jnp.maximum(step - ncs, 0)
    col0 = cps * (xs % (GLA_WBLK // cps))

    def latent(x_ref):
        return jnp.concatenate([x_ref[0, :, col0 + j, :] for j in range(cps)], axis=0)

    pick = lambda c_ref, x_ref: jnp.where(is_ctx, c_ref[0], latent(x_ref))

    @pl.when(pl.program_id(1) == 0)
    def _():
        s_ref[...] = jnp.zeros_like(s_ref)

    hp = lax.Precision.HIGHEST
    rank2 = wg_ref.shape[0]
    pre = jnp.dot(pick(lc_ref, lx_ref)[:, :rank2], wg_ref[...], preferred_element_type=F32,
                  precision=hp) + bg_ref[...]
    g = jax.nn.log_sigmoid(pre) * (1.0 / GLA_GATE_TEMP)
    ci = lax.broadcasted_iota(jnp.int32, (GLA_CHUNK, GLA_CHUNK), 0)
    cj = lax.broadcasted_iota(jnp.int32, (GLA_CHUNK, GLA_CHUNK), 1)
    keep = (cj >= ci) if reverse else (cj <= ci)
    cum = jnp.where(keep, 1.0, 0.0)
    b = jnp.concatenate([jnp.dot(cum, g[c * GLA_CHUNK:(c + 1) * GLA_CHUNK], preferred_element_type=F32, precision=hp)
                         for c in range(nchunk)], axis=0)
    eb = jnp.exp(b)
    enb = jnp.exp(-b)
    q_in = pick(qc_ref, qx_ref) * scale * eb
    k_out = pick(kc_ref, kx_ref) * enb
    v = pick(vc_ref, vx_ref)
    if finish:
        prev = jnp.where(is_ctx, pc_ref[0], px_ref[0])
        r = pick(rc_ref, rx_ref)
        out_gate = r * jax.nn.sigmoid(r)
    nt = (((1,), (1,)), ((), ()))
    tn = (((0,), (0,)), ((), ()))
    for c in (range(nchunk - 1, -1, -1) if reverse else range(nchunk)):
        r0 = c * GLA_CHUNK
        rows = slice(r0, r0 + GLA_CHUNK)
        end = r0 if reverse else r0 + GLA_CHUNK - 1
        etot = jnp.exp(b[end:end + 1, :])
        k_kv = k_out[rows] * etot
        for h in range(nh):
            kc = slice(h * dk, (h + 1) * dk)
            vc = slice(h * dv, (h + 1) * dv)
            qh = q_in[rows, kc].astype(BF16)
            kh = k_out[rows, kc].astype(BF16)
            kkv = k_kv[:, kc].astype(BF16)
            vh = v[rows, vc].astype(BF16)
            st = s_ref[h]
            sc = lax.dot_general(qh, kh, nt, preferred_element_type=F32)
            sc = jnp.where(keep, sc, 0.0).astype(BF16)
            o = (jnp.dot(sc, vh, preferred_element_type=F32)
                 + lax.dot_general(qh, st.astype(BF16), nt, preferred_element_type=F32))
            kvt = lax.dot_general(vh, kkv, tn, preferred_element_type=F32)
            s_ref[h] = st * etot[:, kc] + kvt
            if finish:
                o = o + prev[rows, vc]
                ms = jnp.mean(o * o, axis=-1, keepdims=True)
                o = o * lax.rsqrt(ms + EPS) * ng_ref[...] * out_gate[rows, vc]
            o_scr[rows, vc] = o

    @pl.when(is_ctx)
    def _():
        oc_ref[0] = o_scr[...].astype(oc_ref.dtype)

    @pl.when(jnp.logical_not(is_ctx))
    def _():
        ox_ref[0] = o_scr[...].astype(ox_ref.dtype)


def _gla_pass(pc, gx, c0, wgp, bg, nh, dk, dv, gpad, reverse, fin=None):
    bsz, nctx = pc.shape[:2]
    rows, wcols = gx.shape[1:3]
    seq = rows * wcols
    ts = GLA_STEP
    assert ts % rows == 0 and GLA_WBLK % (ts // rows) == 0 and wcols % GLA_WBLK == 0
    cps = ts // rows
    spb = GLA_WBLK // cps
    ncs, nxs = nctx // ts, seq // ts
    key, val = nh * dk, nh * dv
    if reverse:
        cidx = lambda k: jnp.maximum(ncs - 1 - k, 0)
        xidx = lambda k: jnp.minimum(nxs - 1 - (k - ncs), nxs - 1)
    else:
        cidx = lambda k: jnp.minimum(k, ncs - 1)
        xidx = lambda k: jnp.maximum(k - ncs, 0)

    def group(idx, base):
        return [pl.BlockSpec((1, ts, key), lambda b, k: (b, idx(k), base // key)),
                pl.BlockSpec((1, ts, key), lambda b, k: (b, idx(k), base // key + 1)),
                pl.BlockSpec((1, ts, val), lambda b, k: (b, idx(k), (base + 2 * key) // val)),
                pl.BlockSpec((1, ts, gpad), lambda b, k: (b, idx(k), (base + 2 * key + val) // gpad))]

    def lat(width, col):
        return pl.BlockSpec((1, rows, GLA_WBLK, width), lambda b, k: (b, 0, xidx(k) // spb, col))

    lat_group = [lat(key, c0 // key), lat(key, c0 // key + 1), lat(val, (c0 + 2 * key) // val),
                 lat(gpad, (c0 + 2 * key + val) // gpad)]
    out_dtype = F32 if fin is None else BF16
    in_specs = group(cidx, c0) + lat_group + [_const_spec(wgp.shape), _const_spec((1, key))]
    args = [pc] * 4 + [gx] * 4 + [wgp, bg.reshape(1, key)]
    if fin is not None:
        prev_c, prev_x, ng = fin
        in_specs += [pl.BlockSpec((1, ts, val), lambda b, k: (b, cidx(k), 0)),
                     pl.BlockSpec((1, ts, val), lambda b, k: (b, xidx(k), 0)), _const_spec((1, dv)),
                     pl.BlockSpec((1, ts, val), lambda b, k: (b, cidx(k), (c0 - val) // val)),
                     lat(val, (c0 - val) // val)]
        args += [prev_c, prev_x, ng.reshape(1, dv), pc, gx]
    return pl.pallas_call(
        functools.partial(_gla_kernel, nh=nh, dk=dk, dv=dv, ncs=ncs, nxs=nxs, cps=cps, reverse=reverse,
                          finish=fin is not None, scale=dk ** -0.5),
        out_shape=(jax.ShapeDtypeStruct((bsz, nctx, val), out_dtype),
                   jax.ShapeDtypeStruct((bsz, seq, val), out_dtype)),
        grid=(bsz, ncs + nxs),
        in_specs=in_specs,
        out_specs=(pl.BlockSpec((1, ts, val), lambda b, k: (b, cidx(k), 0)),
                   pl.BlockSpec((1, ts, val), lambda b, k: (b, xidx(k), 0))),
        scratch_shapes=[pltpu.VMEM((nh, dv, dk), F32), pltpu.VMEM((ts, val), F32)],
        compiler_params=_cparams("parallel", "arbitrary"),
        name="gla_rev" if reverse else "gla_fwd",
    )(*args)


def gla_mixer(pc, px, c0, q4, gpad, p):
    dv = p['gla_norm_g'].shape[0]
    nh = (2 * q4) // dv
    dk = q4 // nh
    rank = p['gla_wg'].shape[1]
    bsz, seq, width = px.shape
    gx = px.reshape(bsz, seq // GRID_W, GRID_W, width)
    wgp = [jnp.zeros((2 * rank, q4), F32).at[d * rank:(d + 1) * rank].set(p['gla_wg'][d]) for d in range(2)]
    kw = dict(nh=nh, dk=dk, dv=dv, gpad=gpad)
    rc, rx = _gla_pass(pc, gx, c0, wgp[1], p['gla_bg'][1], reverse=True, **kw)
    oc, ox = _gla_pass(pc, gx, c0, wgp[0], p['gla_bg'][0], reverse=False, fin=(rc, rx, p['gla_norm_g']), **kw)
    return oc, _from_col_major(ox)


HY_N2 = 128
HY_PAD = 8


def _hy_dims(n):
    n1 = 2 * n // HY_N2
    k1 = n1 // 2 + 1
    k1p = -(-k1 // 8) * 8
    return n1, k1, k1p


def _hy_tables(n):
    n1, k1, k1p = _hy_dims(n)
    big = 2 * n
    kk = jnp.arange(k1p, dtype=jnp.int32)[:, None]
    mm = jnp.arange(n1, dtype=jnp.int32)[None, :]
    ang = (2.0 * math.pi / n1) * ((kk * mm) % n1).astype(F32)
    valid = (kk < k1).astype(F32)
    f1 = jnp.concatenate([jnp.cos(ang) * valid, -jnp.sin(ang) * valid], axis=0)
    wk = jnp.where((kk == 0) | (kk == n1 // 2), 1.0, 2.0) * valid / big
    f1inv = jnp.concatenate([(jnp.cos(ang) * wk).T, (-jnp.sin(ang) * wk).T], axis=1)
    k1i = jnp.arange(k1, dtype=jnp.int32)[:, None, None]
    k2i = jnp.arange(HY_N2, dtype=jnp.int32)[None, :, None]
    n2i = jnp.arange(HY_N2, dtype=jnp.int32)[None, None, :]
    ph = (2.0 * math.pi / big) * ((n2i * (k1i + n1 * k2i)) % big).astype(F32)
    gr, gi = jnp.cos(ph), -jnp.sin(ph)
    gs = jnp.concatenate([gr, gi], axis=1)
    gts = jnp.concatenate([gr.swapaxes(1, 2), gi.swapaxes(1, 2)], axis=1)
    return f1.astype(BF16), f1inv.astype(BF16), gs.astype(BF16), gts.astype(BF16)


def _hy_stage1(src_ref, f1, a_r, a_i, nslab, k1p):
    pitch = HY_N2 + HY_PAD

    def body(i, carry):
        n2 = 2 * i
        rows = jnp.concatenate([src_ref[pl.ds(n2, nslab, stride=pitch), :],
                                src_ref[pl.ds(n2 + 1, nslab, stride=pitch), :]], axis=1)
        out = jnp.dot(f1, rows.astype(BF16), preferred_element_type=F32)
        base = pl.multiple_of(n2 * k1p, 8)
        a_r[pl.ds(base, k1p), :] = out[:k1p, :LANE]
        a_i[pl.ds(base, k1p), :] = out[k1p:, :LANE]
        base1 = pl.multiple_of(base + k1p, 8)
        a_r[pl.ds(base1, k1p), :] = out[:k1p, LANE:]
        a_i[pl.ds(base1, k1p), :] = out[k1p:, LANE:]
        return carry

    lax.fori_loop(0, HY_N2 // 2, body, 0, unroll=8)


def _hy_stage2(a_r, a_i, gs_ref, k, k1p):
    ar = a_r[pl.ds(k, HY_N2, stride=k1p), :]
    ai = a_i[pl.ds(k, HY_N2, stride=k1p), :]
    rhs = jnp.concatenate([ar, ai], axis=1).astype(BF16)
    out = jnp.dot(gs_ref[k], rhs, preferred_element_type=F32)
    h = HY_N2
    return out[:h, :LANE] - out[h:, LANE:], out[:h, LANE:] + out[h:, :LANE]


def _hyena_conv_kernel(y_ref, g_ref, wy_ref, by_ref, wg_ref, bg_ref, bias_ref, hr_ref, hi_ref,
                       f1_ref, f1inv_ref, gs_ref, gts_ref, o_ref, ypad, zbuf, a_r, a_i, *, conv_y):
    n = y_ref.shape[1]
    nslab = n // HY_N2
    pitch = HY_N2 + HY_PAD
    k1 = gs_ref.shape[0]
    k1p = f1_ref.shape[0] // 2
    h = HY_N2

    def slab_conv(ref, i, w_ref, b_ref):
        r0 = pl.multiple_of(i * HY_N2, HY_N2)
        prev8 = ref[0, pl.ds(pl.multiple_of(jnp.maximum(r0 - 8, 0), 8), 8), :]
        next8 = ref[0, pl.ds(pl.multiple_of(jnp.minimum(r0 + HY_N2, n - 8), 8), 8), :]
        return _dwconv3_tile(ref[0, pl.ds(r0, HY_N2), :], prev8, next8, i == 0, i == nslab - 1, w_ref, b_ref)

    def y_slab(i):
        if conv_y:
            return slab_conv(y_ref, i, wy_ref, by_ref)
        return y_ref[0, pl.ds(pl.multiple_of(i * HY_N2, HY_N2), HY_N2), :]

    def fill(i, carry):
        ypad[pl.ds(pl.multiple_of(i * pitch, 8), HY_N2), :] = y_slab(i)
        return carry

    lax.fori_loop(0, nslab, fill, 0, unroll=2)
    _hy_stage1(ypad, f1_ref[...], a_r, a_i, nslab, k1p)

    def freq_fwd(k, carry):
        xr, xi = _hy_stage2(a_r, a_i, gs_ref, k, k1p)
        hr, hi = hr_ref[0, k], hi_ref[0, k]
        zbuf[k] = jnp.concatenate([xr * hr - xi * hi, xr * hi + xi * hr], axis=1).astype(BF16)
        return carry

    lax.fori_loop(0, k1, freq_fwd, 0, unroll=4)

    def freq_inv(k, carry):
        out = jnp.dot(gts_ref[k], zbuf[k], preferred_element_type=F32)
        a_r[pl.ds(k, HY_N2, stride=k1p), :] = out[:h, :LANE] + out[h:, LANE:]
        a_i[pl.ds(k, HY_N2, stride=k1p), :] = out[:h, LANE:] - out[h:, :LANE]
        return carry

    lax.fori_loop(0, k1, freq_inv, 0, unroll=4)
    f1inv = f1inv_ref[...][:nslab]

    def inv1(i, carry):
        n2 = 2 * i
        b0 = pl.multiple_of(n2 * k1p, 8)
        b1 = pl.multiple_of(b0 + k1p, 8)
        rhs = jnp.concatenate(
            [jnp.concatenate([a_r[pl.ds(b0, k1p), :], a_i[pl.ds(b0, k1p), :]], axis=0),
             jnp.concatenate([a_r[pl.ds(b1, k1p), :], a_i[pl.ds(b1, k1p), :]], axis=0)], axis=1).astype(BF16)
        out = jnp.dot(f1inv, rhs, preferred_element_type=F32)
        ypad[pl.ds(n2, nslab, stride=pitch), :] = out[:, :LANE]
        ypad[pl.ds(n2 + 1, nslab, stride=pitch), :] = out[:, LANE:]
        return carry

    lax.fori_loop(0, HY_N2 // 2, inv1, 0, unroll=8)

    def finish(i, carry):
        p0 = pl.multiple_of(i * pitch, 8)
        gate = slab_conv(g_ref, i, wg_ref, bg_ref)
        res = gate * (ypad[pl.ds(p0, HY_N2), :] + y_slab(i) * bias_ref[0])
        o_ref[0, pl.ds(pl.multiple_of(i * HY_N2, HY_N2), HY_N2), :] = res.astype(o_ref.dtype)
        return carry

    lax.fori_loop(0, nslab, finish, 0, unroll=2)


def _hyena_spectrum_kernel(f_ref, f1_ref, gs_ref, hr_ref, hi_ref, a_r, a_i, fpad):
    big = f_ref.shape[1]
    nslab = big // HY_N2
    pitch = HY_N2 + HY_PAD
    k1 = gs_ref.shape[0]
    k1p = f1_ref.shape[0] // 2

    def fill(i, carry):
        fpad[pl.ds(pl.multiple_of(i * pitch, 8), HY_N2), :] = f_ref[0, pl.ds(pl.multiple_of(i * HY_N2, HY_N2), HY_N2), :]
        return carry

    lax.fori_loop(0, nslab, fill, 0, unroll=2)
    _hy_stage1(fpad, f1_ref[...], a_r, a_i, nslab, k1p)

    def freq(k, carry):
        xr, xi = _hy_stage2(a_r, a_i, gs_ref, k, k1p)
        hr_ref[0, k] = xr
        hi_ref[0, k] = xi
        return carry

    lax.fori_loop(0, k1, freq, 0, unroll=4)


def _one(shape, index_map):
    return pl.BlockSpec(shape, index_map, pipeline_mode=pl.Buffered(1))


def hyena_spectrum(filt, tabs):
    f1, _, gs, _ = tabs
    r, big, w = filt.shape
    ns = w // LANE
    n1, k1, k1p = _hy_dims(big // 2)
    shp = jax.ShapeDtypeStruct((r * ns, k1, HY_N2, LANE), F32)
    spec_o = pl.BlockSpec((1, k1, HY_N2, LANE), lambda i, j: (i * ns + j, 0, 0, 0))
    return pl.pallas_call(
        _hyena_spectrum_kernel,
        out_shape=(shp, shp),
        grid=(r, ns),
        in_specs=[_one((1, big, LANE), lambda i, j: (i, 0, j)),
                  _one(f1.shape, lambda i, j: (0, 0)),
                  _one(gs.shape, lambda i, j: (0, 0, 0))],
        out_specs=(spec_o, spec_o),
        scratch_shapes=[pltpu.VMEM((HY_N2 * k1p, LANE), F32), pltpu.VMEM((HY_N2 * k1p, LANE), F32),
                        pltpu.VMEM((n1 * (HY_N2 + HY_PAD), LANE), F32)],
        compiler_params=_cparams("parallel", "parallel"),
        name="hyena_spectrum",
    )(filt, f1, gs)


def hyena_order(y, ycol, g, gcol, conv_w, conv_b, cy, cg, bias, hr, hi, order, tabs, conv_y, out_dtype):
    f1, f1inv, gs, gts = tabs
    bsz, n = y.shape[:2]
    w = bias.shape[-1]
    ns = w // LANE
    n1, k1, k1p = _hy_dims(n)
    f1d = f1[:, :n1 // 2]
    pitch = HY_N2 + HY_PAD
    nslab = n // HY_N2
    cw = lambda c: _one((3, LANE), lambda j, b: (0, c + j))
    cb = lambda c: _one((1, LANE), lambda j, b: (0, c + j))
    hspec = _one((1, k1, HY_N2, LANE), lambda j, b: (order * ns + j, 0, 0, 0))
    return pl.pallas_call(
        functools.partial(_hyena_conv_kernel, conv_y=conv_y),
        out_shape=jax.ShapeDtypeStruct((bsz, n, w), out_dtype),
        grid=(ns, bsz),
        in_specs=[_one((1, n, LANE), lambda j, b: (b, 0, ycol + j)),
                  _one((1, n, LANE), lambda j, b: (b, 0, gcol + j)),
                  cw(cy), cb(cy), cw(cg), cb(cg),
                  _one((1, 1, LANE), lambda j, b: (order, 0, j)),
                  hspec, hspec,
                  _one(f1d.shape, lambda j, b: (0, 0)), _one(f1inv.shape, lambda j, b: (0, 0)),
                  _one(gs.shape, lambda j, b: (0, 0, 0)), _one(gts.shape, lambda j, b: (0, 0, 0))],
        out_specs=pl.BlockSpec((1, n, LANE), lambda j, b: (b, 0, j)),
        scratch_shapes=[pltpu.VMEM((nslab * pitch, LANE), F32), pltpu.VMEM((k1, HY_N2, 2 * LANE), BF16),
                        pltpu.VMEM((HY_N2 * k1p, LANE), F32), pltpu.VMEM((HY_N2 * k1p, LANE), F32)],
        compiler_params=pltpu.CompilerParams(dimension_semantics=("parallel", "parallel"),
                                             vmem_limit_bytes=BIG_VMEM_LIMIT),
        name="hyena_order",
    )(y, g, conv_w, conv_b, conv_w, conv_b, bias.reshape(bias.shape[0], 1, w), hr, hi, f1d, f1inv, gs, gts)


def hyena_latent(px, c0, filt, p):
    w = p['hy_bias'].shape[-1]
    ns = w // LANE
    n = px.shape[1]
    tabs = _hy_tables(n)
    hr, hi = hyena_spectrum(filt, tabs)
    cw, cb = p['hy_conv_w'], p['hy_conv_b'].reshape(1, -1)
    b0 = c0 // LANE
    y1 = hyena_order(px, b0, px, b0 + ns, cw, cb, 0, ns, p['hy_bias'], hr, hi, 0, tabs, True, F32)
    return hyena_order(y1, 0, px, b0 + 2 * ns, cw, cb, 0, 2 * ns, p['hy_bias'], hr, hi, 1, tabs, False, BF16)


def _hyena_ctx_kernel(zv_ref, z1_ref, z2_ref, wv_ref, bv_ref, w1_ref, b1_ref, w2_ref, b2_ref, bias_ref,
                      filt_ref, ff_ref, finv_ref, o_ref):
    n = zv_ref.shape[1]
    hp = lax.Precision.HIGHEST
    zero8 = jnp.zeros((8, LANE), F32)
    conv = lambda ref, w, b: _dwconv3_tile(ref[0], zero8, zero8, True, True, w, b)
    ff = ff_ref[...]
    kp = ff.shape[0] // 2
    y = conv(zv_ref, wv_ref, bv_ref)
    for o, (g_ref, w, b) in enumerate(((z1_ref, w1_ref, b1_ref), (z2_ref, w2_ref, b2_ref))):
        hsp = jnp.dot(ff, filt_ref[o], preferred_element_type=F32, precision=hp)
        ysp = jnp.dot(ff[:, :n], y, preferred_element_type=F32, precision=hp)
        hr, hi, yr, yi = hsp[:kp], hsp[kp:], ysp[:kp], ysp[kp:]
        z = jnp.concatenate([yr * hr - yi * hi, yr * hi + yi * hr], axis=0)
        cv = jnp.dot(finv_ref[...], z, preferred_element_type=F32, precision=hp)
        y = conv(g_ref, w, b) * (cv + y * bias_ref[o])
    o_ref[0] = y.astype(o_ref.dtype)


def hyena_context(pc, c0, filt, p):
    bsz, n = pc.shape[:2]
    w = p['hy_bias'].shape[-1]
    ns = w // LANE
    kp = -(-(n + 1) // 8) * 8
    kk = jnp.arange(kp, dtype=jnp.int32)[:, None]
    mm = jnp.arange(2 * n, dtype=jnp.int32)[None, :]
    ang = (math.pi / n) * ((kk * mm) % (2 * n)).astype(F32)
    valid = (kk <= n).astype(F32)
    ff = jnp.concatenate([jnp.cos(ang) * valid, -jnp.sin(ang) * valid], axis=0)
    wk = jnp.where((kk == 0) | (kk == n), 1.0, 2.0) * valid / (2 * n)
    finv = jnp.concatenate([(jnp.cos(ang) * wk).T[:n], (-jnp.sin(ang) * wk).T[:n]], axis=1)
    b0 = c0 // LANE
    zs = lambda c: pl.BlockSpec((1, n, LANE), lambda j, b: (b, 0, b0 + c + j))
    cw = lambda c: pl.BlockSpec((3, LANE), lambda j, b: (0, c + j))
    cb = lambda c: pl.BlockSpec((1, LANE), lambda j, b: (0, c + j))
    conv_w, conv_b = p['hy_conv_w'], p['hy_conv_b'].reshape(1, -1)
    return pl.pallas_call(
        _hyena_ctx_kernel,
        out_shape=jax.ShapeDtypeStruct((bsz, n, w), BF16),
        grid=(ns, bsz),
        in_specs=[zs(0), zs(ns), zs(2 * ns), cw(0), cb(0), cw(ns), cb(ns), cw(2 * ns), cb(2 * ns),
                  pl.BlockSpec((2, 1, LANE), lambda j, b: (0, 0, j)),
                  pl.BlockSpec((2, 2 * n, LANE), lambda j, b: (0, 0, j)),
                  pl.BlockSpec(ff.shape, lambda j, b: (0, 0)), pl.BlockSpec(finv.shape, lambda j, b: (0, 0))],
        out_specs=pl.BlockSpec((1, n, LANE), lambda j, b: (b, 0, j)),
        compiler_params=_cparams("parallel", "parallel"),
        name="hyena_context",
    )(pc, pc, pc, conv_w, conv_b, conv_w, conv_b, conv_w, conv_b, p['hy_bias'].reshape(2, 1, w), filt, ff, finv)


def _from_col_major(t):
    bsz, n = t.shape[:2]
    rows = n // GRID_W
    return t.reshape(bsz, GRID_W, rows, *t.shape[2:]).swapaxes(1, 2).reshape(bsz, n, *t.shape[2:])


def _filter_positions(n):
    r = jnp.arange(2 * n, dtype=jnp.int32)
    return jnp.where(r < n, r, 2 * n - r).astype(F32)


def _filter_feats(n):
    pos = _filter_positions(n)
    t = pos / max(n - 1, 1)
    freqs = jnp.linspace(1e-4, HY_POS_FREQS - 1, HY_POS_FREQS, dtype=F32)
    ang = (2.0 * math.pi / n) * pos[:, None] * freqs[None]
    feats = jnp.concatenate([t[:, None], jnp.cos(ang), -jnp.sin(ang)], axis=-1)
    return jnp.pad(feats, ((0, 0), (0, LANE - feats.shape[1])))


def _filter_hidden_kernel(f_ref, w1_ref, b1_ref, s1_ref, w2_ref, b2_ref, s2_ref, o_ref):
    hp = lax.Precision.HIGHEST
    h = jnp.sin(s1_ref[...] * (jnp.dot(f_ref[...], w1_ref[...], preferred_element_type=F32, precision=hp)
                               + b1_ref[...]))
    o_ref[...] = jnp.sin(s2_ref[...] * (jnp.dot(h, w2_ref[...], preferred_element_type=F32, precision=hp)
                                        + b2_ref[...]))


def _filter_out_kernel(h_ref, wf_ref, wb_ref, bf_ref, bb_ref, rate_ref, o_ref, *, n, chunk):
    nchunk = 2 * n // chunk
    inv_span = 1.0 / max(n - 1, 1)

    def emit(c, acc):
        r0 = pl.multiple_of(c * chunk, chunk)
        past = c < nchunk // 2
        w = jnp.where(past, wf_ref[...], wb_ref[...])
        bias = jnp.where(past, bf_ref[...], bb_ref[...])
        hr0 = pl.multiple_of((c % (nchunk // 2)) * chunk, chunk)
        val = jnp.dot(h_ref[pl.ds(hr0, chunk), :], w, preferred_element_type=F32,
                      precision=lax.Precision.HIGHEST) + bias
        r = r0 + lax.broadcasted_iota(jnp.int32, (chunk, LANE), 0)
        t = jnp.where(r < n, r, 2 * n - r).astype(F32) * inv_span
        val = jnp.where(r == n, 0.0, val * jnp.exp(-t * rate_ref[...]))
        o_ref[0, pl.ds(r0, chunk), :] = val
        return acc + jnp.sum(jnp.abs(val), axis=0, keepdims=True)

    total = lax.fori_loop(0, nchunk, emit, jnp.zeros((1, LANE), F32))
    scale = 1.0 / (total + EPS)

    def rescale(c, carry):
        r0 = pl.multiple_of(c * chunk, chunk)
        o_ref[0, pl.ds(r0, chunk), :] = o_ref[0, pl.ds(r0, chunk), :] * scale
        return carry

    lax.fori_loop(0, nchunk, rescale, 0)


def hyena_filters(n, p):
    order, width = p['hy_bias'].shape
    ns = width // LANE
    hid = p['hy_f_w2'].shape[0]
    half = LANE // 2
    assert hid <= half
    feats = _filter_feats(n)
    feats = jnp.concatenate([feats[:n], feats[n:]], axis=1)
    zpad = lambda a, rows, cols: jnp.pad(a, ((0, rows - a.shape[0]), (0, cols - a.shape[1])))
    diag2 = lambda a: jnp.concatenate([jnp.pad(a, ((0, 0), (0, a.shape[1]))),
                                       jnp.pad(a, ((0, 0), (a.shape[1], 0)))], axis=0)
    twice = lambda a: jnp.tile(zpad(a.reshape(1, -1), 1, half), (1, 2))
    w1 = diag2(zpad(p['hy_f_w1'], LANE, half))
    w2 = diag2(zpad(p['hy_f_w2'], half, half))
    w3 = zpad(p['hy_f_w3'], half, p['hy_f_w3'].shape[1])
    w3_past = jnp.pad(w3, ((0, half), (0, 0)))
    w3_future = jnp.pad(w3, ((half, 0), (0, 0)))
    tr = _pick_tile(n, 2048, 8)
    full = lambda i: (0, 0)
    hidden = pl.pallas_call(
        _filter_hidden_kernel,
        out_shape=jax.ShapeDtypeStruct((n, LANE), F32),
        grid=(n // tr,),
        in_specs=[pl.BlockSpec((tr, 2 * LANE), lambda i: (i, 0)),
                  pl.BlockSpec((2 * LANE, LANE), full), pl.BlockSpec((1, LANE), full), pl.BlockSpec((1, LANE), full),
                  pl.BlockSpec((LANE, LANE), full), pl.BlockSpec((1, LANE), full), pl.BlockSpec((1, LANE), full)],
        out_specs=pl.BlockSpec((tr, LANE), lambda i: (i, 0)),
        compiler_params=_cparams("parallel"),
        name="hyena_filter_hidden",
    )(feats, w1, twice(p['hy_f_b1']), twice(p['hy_f_freq1']), w2, twice(p['hy_f_b2']), twice(p['hy_f_freq2']))
    rates = jnp.abs(jnp.linspace(math.log(HY_DECAY_TARGET) / HY_LONG_DECAY_PCT,
                                 math.log(HY_DECAY_TARGET) / HY_SHORT_DECAY_PCT, width, dtype=F32)).reshape(1, width)
    b3 = p['hy_f_b3'].reshape(1, -1)
    return pl.pallas_call(
        functools.partial(_filter_out_kernel, n=n, chunk=min(1024, n)),
        out_shape=jax.ShapeDtypeStruct((order, 2 * n, width), F32),
        grid=(order, ns),
        in_specs=[_one((n, LANE), lambda o, j: (0, 0)),
                  pl.BlockSpec((LANE, LANE), lambda o, j: (0, 2 * o * ns + j)),
                  pl.BlockSpec((LANE, LANE), lambda o, j: (0, (2 * o + 1) * ns + j)),
                  pl.BlockSpec((1, LANE), lambda o, j: (0, 2 * o * ns + j)),
                  pl.BlockSpec((1, LANE), lambda o, j: (0, (2 * o + 1) * ns + j)),
                  pl.BlockSpec((1, LANE), lambda o, j: (0, j))],
        out_specs=pl.BlockSpec((1, 2 * n, LANE), lambda o, j: (o, 0, j)),
        compiler_params=_cparams("parallel", "parallel"),
        name="hyena_filter_out",
    )(hidden, w3_past, w3_future, b3, b3, rates)


def kernel(x, c, ctx, c_ctx, w_mod, b_mod, norm1_g, norm2_g, w_in, s5_a_re, s5_a_im, s5_log_step, s5_b_re, s5_b_im, s5_c_re, s5_c_im, s5_d, s5_glu_w, s5_glu_b, hy_conv_w, hy_conv_b, hy_f_w1, hy_f_b1, hy_f_freq1, hy_f_w2, hy_f_b2, hy_f_freq2, hy_f_w3, hy_f_b3, hy_bias, gla_wg, gla_bg, gla_norm_g, w_branch, w_out, ff_w_up, ff_conv_w, ff_conv_b, ff_w_down, final_norm_g):
    bsz, seq, d = x.shape
    nctx = ctx.shape[1]
    depth = w_mod.shape[0]
    q4 = d // 4
    rank2 = 2 * gla_wg.shape[2]
    gpad = max(LANE, q4 // 2)
    assert bsz + 1 <= 8

    c_gk = q4
    c_gv = 2 * q4
    c_gg = 4 * q4
    c_gq = c_gg + rank2
    c_gr = c_gq + q4
    c_hy = c_gr + 2 * q4
    c_mg = c_hy + 3 * q4

    rows = jnp.zeros((8, d), F32).at[:bsz].set(c).at[bsz].set(c_ctx)
    mod = modulation(rows, w_mod, b_mod)

    xs = x.reshape(bsz * seq, d)
    cs = ctx.reshape(bsz * nctx, d)
    for l in range(depth):
        ctx_out = l < depth - 1
        sh1, s1, g1, sh2, s2, g2 = [mod[l, :, i * d:(i + 1) * d][:, None, :] for i in range(6)]
        gs1 = norm1_g[l] * (1.0 + s1)
        gs2 = norm2_g[l] * (1.0 + s2)
        bx = slice(0, bsz)
        bc = slice(bsz, bsz + 1)

        wl = w_in[l]
        w_pack = jnp.concatenate([
            wl[:, 0:c_gk], wl[:, c_hy:c_mg], wl[:, c_gr:c_hy], wl[:, c_gk:c_gv], wl[:, c_gq:c_gr],
            wl[:, c_gv:c_gg], wl[:, c_gg:c_gq],
            jnp.zeros((d, gpad - rank2), F32)], axis=1).astype(BF16)
        w_gate = wl[:, c_mg:].reshape(d, 3, d).swapaxes(0, 1).astype(BF16)
        wb = w_branch[l].astype(BF16)
        w_o = w_out[l].astype(BF16)
        w_up = cast_col_blocks(ff_w_up[l], _pick_tile(ff_w_down.shape[1], FFN_TF))
        w_dn = ff_w_down[l].astype(BF16)

        p = dict(s5_a_re=s5_a_re[l], s5_a_im=s5_a_im[l], s5_log_step=s5_log_step[l],
                 s5_b_re=s5_b_re[l], s5_b_im=s5_b_im[l], s5_c_re=s5_c_re[l], s5_c_im=s5_c_im[l],
                 s5_d=s5_d[l], s5_glu_w=s5_glu_w[l], s5_glu_b=s5_glu_b[l],
                 hy_conv_w=hy_conv_w[l], hy_conv_b=hy_conv_b[l], hy_f_w1=hy_f_w1[l], hy_f_b1=hy_f_b1[l],
                 hy_f_freq1=hy_f_freq1[l], hy_f_w2=hy_f_w2[l], hy_f_b2=hy_f_b2[l],
                 hy_f_freq2=hy_f_freq2[l], hy_f_w3=hy_f_w3[l], hy_f_b3=hy_f_b3[l], hy_bias=hy_bias[l],
                 gla_wg=gla_wg[l], gla_bg=gla_bg[l], gla_norm_g=gla_norm_g[l])

        px = norm_matmul(xs, gs1[bx], sh1[bx], w_pack).reshape(bsz, seq, -1)
        pc = norm_matmul(cs, gs1[bc], sh1[bc], w_pack).reshape(bsz, nctx, -1)

        ya_c, ya_x = s5_mixer(pc, px, q4, p)
        oc_c, oc_x = gla_mixer(pc, px, 6 * q4, q4, gpad, p)
        yb_x = hyena_latent(px, q4, hyena_filters(seq, p), p)

        flat = lambda t: t.reshape(-1, t.shape[-1])
        mx = merge(xs, gs1[bx], sh1[bx], flat(ya_x), flat(yb_x), flat(oc_x), w_gate, wb)
        xs = matmul_residual(mx, w_o, xs, g1[bx])
        xs = ffn(xs, bsz, gs2[bx], sh2[bx], w_up, ff_conv_w[l], ff_conv_b[l], w_dn, g2[bx])

        if ctx_out:
            yb_c = hyena_context(pc, q4, hyena_filters(nctx, p), p)
            mc = merge(cs, gs1[bc], sh1[bc], flat(ya_c), flat(yb_c), flat(oc_c), w_gate, wb)
            cs = matmul_residual(mc, w_o, cs, g1[bc])
            cs = ffn(cs, bsz, gs2[bc], sh2[bc], w_up, ff_conv_w[l], ff_conv_b[l], w_dn, g2[bc])

    return final_rmsnorm(xs, final_norm_g).reshape(bsz, seq, d)
```

```python
import functools
import math

import jax
import jax.numpy as jnp
from jax import lax
from jax.experimental import pallas as pl
from jax.experimental.pallas import tpu as pltpu

F32 = jnp.float32
BF16 = jnp.bfloat16
EPS = 1e-6
GRID_W = 64
S5_GROUP = 16
GLA_CHUNK = 64
GLA_GATE_TEMP = 16.0
HY_POS_FREQS = 16
HY_DECAY_TARGET = 1e-2
HY_SHORT_DECAY_PCT = 0.3
HY_LONG_DECAY_PCT = 1.5
LANE = 128
VMEM_LIMIT = 56 * 1024 * 1024
BIG_VMEM_LIMIT = 60 * 1024 * 1024
FFN_TF = 512


def _pick_tile(n, cap, mult=LANE):
    best = None
    for t in range(mult, min(n, cap) + 1, mult):
        if n % t == 0:
            best = t
    assert best is not None, (n, cap, mult)
    return best


def _col_blocks(w, tn):
    nj = w.shape[-1] // tn
    return jnp.moveaxis(w.reshape(*w.shape[:-1], nj, tn), -2, 0)


def _cast_blocks_kernel(w_ref, o_ref):
    o_ref[0] = w_ref[...].astype(o_ref.dtype)


def cast_col_blocks(w, tn):
    k, n = w.shape
    return pl.pallas_call(
        _cast_blocks_kernel,
        out_shape=jax.ShapeDtypeStruct((n // tn, k, tn), BF16),
        grid=(n // tn,),
        in_specs=[pl.BlockSpec((k, tn), lambda j: (0, j))],
        out_specs=pl.BlockSpec((1, k, tn), lambda j: (j, 0, 0)),
        compiler_params=pltpu.CompilerParams(dimension_semantics=("parallel",), vmem_limit_bytes=VMEM_LIMIT),
        name="cast_col_blocks",
    )(w)


def _cparams(*sem):
    return pltpu.CompilerParams(dimension_semantics=sem, vmem_limit_bytes=VMEM_LIMIT)


def _mod_kernel(r_ref, w_ref, b_ref, o_ref):
    r = r_ref[...]
    s = r * jax.nn.sigmoid(r)
    o_ref[0] = jnp.dot(s, w_ref[0], preferred_element_type=F32,
                       precision=lax.Precision.HIGHEST) + b_ref[0]


def modulation(rows, w_mod, b_mod):
    depth, d, n = w_mod.shape
    tn = _pick_tile(n, 1024)
    return pl.pallas_call(
        _mod_kernel,
        out_shape=jax.ShapeDtypeStruct((depth, 8, n), F32),
        grid=(depth, n // tn),
        in_specs=[pl.BlockSpec((8, d), lambda l, j: (0, 0)),
                  pl.BlockSpec((1, d, tn), lambda l, j: (l, 0, j)),
                  pl.BlockSpec((1, 1, tn), lambda l, j: (l, 0, j))],
        out_specs=pl.BlockSpec((1, 8, tn), lambda l, j: (l, 0, j)),
        compiler_params=_cparams("parallel", "parallel"),
        name="modulation",
    )(rows, w_mod, b_mod.reshape(depth, 1, n))


def _norm_mm_kernel(x_ref, gs_ref, sh_ref, w_ref, o_ref, h_ref):
    @pl.when(pl.program_id(1) == 0)
    def _():
        x = x_ref[...]
        ms = jnp.mean(x * x, axis=-1, keepdims=True)
        h_ref[...] = (x * lax.rsqrt(ms + EPS) * gs_ref[0] + sh_ref[0]).astype(BF16)

    o_ref[...] = jnp.dot(h_ref[...], w_ref[0], preferred_element_type=F32).astype(o_ref.dtype)


def norm_matmul(x, gs, sh, w, out_dtype=F32):
    m, d = x.shape
    n = w.shape[1]
    nb = gs.shape[0]
    tm = _pick_tile(m // nb, 1024, 8)
    tn = _pick_tile(n, 2048)
    tpb = (m // nb) // tm
    return pl.pallas_call(
        _norm_mm_kernel,
        out_shape=jax.ShapeDtypeStruct((m, n), out_dtype),
        grid=(m // tm, n // tn),
        in_specs=[pl.BlockSpec((tm, d), lambda i, j: (i, 0)),
                  pl.BlockSpec((1, 1, d), lambda i, j: (i // tpb, 0, 0)),
                  pl.BlockSpec((1, 1, d), lambda i, j: (i // tpb, 0, 0)),
                  pl.BlockSpec((1, d, tn), lambda i, j: (j, 0, 0))],
        out_specs=pl.BlockSpec((tm, tn), lambda i, j: (i, j)),
        scratch_shapes=[pltpu.VMEM((tm, d), BF16)],
        compiler_params=_cparams("parallel", "arbitrary"),
        name="norm_matmul",
    )(x, gs, sh, _col_blocks(w, tn))


def _mm_res_kernel(a_ref, w_ref, r_ref, g_ref, o_ref):
    acc = jnp.dot(a_ref[...], w_ref[0], preferred_element_type=F32)
    o_ref[...] = r_ref[...] + g_ref[0] * acc


def matmul_residual(a, w, res, g):
    m, k = a.shape
    n = w.shape[1]
    nb = g.shape[0]
    tm = _pick_tile(m // nb, 1024, 8)
    tn = _pick_tile(n, 2048)
    tpb = (m // nb) // tm
    resident = pl.Buffered(1) if n == tn else None
    return pl.pallas_call(
        _mm_res_kernel,
        out_shape=jax.ShapeDtypeStruct((m, n), F32),
        grid=(m // tm, n // tn),
        in_specs=[pl.BlockSpec((tm, k), lambda i, j: (i, 0)),
                  pl.BlockSpec((1, k, tn), lambda i, j: (j, 0, 0), pipeline_mode=resident),
                  pl.BlockSpec((tm, tn), lambda i, j: (i, j)),
                  pl.BlockSpec((1, 1, tn), lambda i, j: (i // tpb, 0, j))],
        out_specs=pl.BlockSpec((tm, tn), lambda i, j: (i, j)),
        compiler_params=_cparams("parallel", "arbitrary"),
        name="matmul_residual",
    )(a, _col_blocks(w, tn), res, g)


def _merge_kernel(x_ref, gs_ref, sh_ref, ya_ref, yb_ref, yc_ref, wg_ref, wb_ref, o_ref, h_ref):
    @pl.when(pl.program_id(1) == 0)
    def _():
        x = x_ref[...]
        ms = jnp.mean(x * x, axis=-1, keepdims=True)
        h_ref[...] = (x * lax.rsqrt(ms + EPS) * gs_ref[0] + sh_ref[0]).astype(BF16)

    h = h_ref[...]
    m = None
    lo = 0
    for i, y_ref in enumerate((ya_ref, yb_ref, yc_ref)):
        hi = lo + y_ref.shape[1]
        gate = jax.nn.sigmoid(jnp.dot(h, wg_ref[0, i], preferred_element_type=F32))
        br = jnp.dot(y_ref[...], wb_ref[0, lo:hi, :], preferred_element_type=F32)
        m = gate * br if m is None else m + gate * br
        lo = hi
    o_ref[...] = m.astype(o_ref.dtype)


def merge(x, gs, sh, ya, yb, yc, wg, wb):
    m, d = x.shape
    nb = gs.shape[0]
    tm = _pick_tile(m // nb, 1024, 8)
    tn = _pick_tile(d, 512)
    tpb = (m // nb) // tm
    row = lambda i, j: (i, 0)
    wy = wb.shape[0]
    return pl.pallas_call(
        _merge_kernel,
        out_shape=jax.ShapeDtypeStruct((m, d), BF16),
        grid=(m // tm, d // tn),
        in_specs=[pl.BlockSpec((tm, d), row),
                  pl.BlockSpec((1, 1, d), lambda i, j: (i // tpb, 0, 0)),
                  pl.BlockSpec((1, 1, d), lambda i, j: (i // tpb, 0, 0)),
                  pl.BlockSpec((tm, ya.shape[1]), row),
                  pl.BlockSpec((tm, yb.shape[1]), row),
                  pl.BlockSpec((tm, yc.shape[1]), row),
                  pl.BlockSpec((1, 3, d, tn), lambda i, j: (j, 0, 0, 0)),
                  pl.BlockSpec((1, wy, tn), lambda i, j: (j, 0, 0))],
        out_specs=pl.BlockSpec((tm, tn), lambda i, j: (i, j)),
        scratch_shapes=[pltpu.VMEM((tm, d), BF16)],
        compiler_params=_cparams("parallel", "arbitrary"),
        name="merge",
    )(x, gs, sh, ya, yb, yc, _col_blocks(wg, tn), _col_blocks(wb, tn))


def _dwconv3_tile(a, prev8, next8, first, last, w_ref, cb_ref):
    tr = a.shape[0]
    row = lax.broadcasted_iota(jnp.int32, a.shape, 0)
    before = jnp.where(first, 0.0, prev8[7:8, :])
    after = jnp.where(last, 0.0, next8[0:1, :])
    prev = jnp.where(row == 0, before, pltpu.roll(a, 1, 0))
    nxt = jnp.where(row == tr - 1, after, pltpu.roll(a, tr - 1, 0))
    return prev * w_ref[0:1, :] + a * w_ref[1:2, :] + nxt * w_ref[2:3, :] + cb_ref[...]


def _ffn_kernel(x_ref, xp_ref, xn_ref, gs_ref, sh_ref, wa_ref, wb_ref, cw_ref, cb_ref, wd_ref, g_ref,
                o_ref, h_ref, hh_ref, *, tpb):
    i, f = pl.program_id(0), pl.program_id(1)

    def normed(x):
        ms = jnp.mean(x * x, axis=-1, keepdims=True)
        return (x * lax.rsqrt(ms + EPS) * gs_ref[0] + sh_ref[0]).astype(BF16)

    @pl.when(f == 0)
    def _():
        h_ref[...] = normed(x_ref[...])
        hh_ref[0:8, :] = normed(xp_ref[...])
        hh_ref[8:16, :] = normed(xn_ref[...])
        o_ref[...] = jnp.zeros_like(o_ref)

    wa = wa_ref[0]
    a = jnp.dot(h_ref[...], wa, preferred_element_type=F32)
    ah = jnp.dot(hh_ref[...], wa, preferred_element_type=F32)
    b = jnp.dot(h_ref[...], wb_ref[0], preferred_element_type=F32)
    pos = i % tpb
    conv = _dwconv3_tile(a, ah[0:8], ah[8:16], pos == 0, pos == tpb - 1, cw_ref, cb_ref)
    act = (conv * jax.nn.sigmoid(conv) * b).astype(BF16)
    half = o_ref.shape[1] // 2
    for lo in (0, half):
        o_ref[:, lo:lo + half] += jnp.dot(act, wd_ref[:, lo:lo + half], preferred_element_type=F32)

    @pl.when(f == pl.num_programs(1) - 1)
    def _():
        o_ref[...] = x_ref[...] + g_ref[0] * o_ref[...]


def ffn(x, nseq, gs, sh, w_up, conv_w, conv_b, w_dn, g):
    m, d = x.shape
    fh = w_dn.shape[0]
    nb = gs.shape[0]
    tm = _pick_tile(m // nseq, 1024, 8)
    tf = w_up.shape[2]
    tpb = (m // nseq) // tm
    tpm = (m // nb) // tm
    nf = fh // tf
    r8 = tm // 8
    last8 = m // 8 - 1
    mod = lambda i, f: (i // tpm, 0, 0)
    return pl.pallas_call(
        functools.partial(_ffn_kernel, tpb=tpb),
        out_shape=jax.ShapeDtypeStruct((m, d), F32),
        grid=(m // tm, nf),
        in_specs=[pl.BlockSpec((tm, d), lambda i, f: (i, 0), pipeline_mode=pl.Buffered(1)),
                  pl.BlockSpec((8, d), lambda i, f: (jnp.maximum(i * r8 - 1, 0), 0)),
                  pl.BlockSpec((8, d), lambda i, f: (jnp.minimum((i + 1) * r8, last8), 0)),
                  pl.BlockSpec((1, 1, d), mod), pl.BlockSpec((1, 1, d), mod),
                  pl.BlockSpec((1, d, tf), lambda i, f: (f, 0, 0)),
                  pl.BlockSpec((1, d, tf), lambda i, f: (f + nf, 0, 0)),
                  pl.BlockSpec((3, tf), lambda i, f: (0, f)),
                  pl.BlockSpec((1, tf), lambda i, f: (0, f)),
                  pl.BlockSpec((tf, d), lambda i, f: (f, 0)),
                  pl.BlockSpec((1, 1, d), mod)],
        out_specs=pl.BlockSpec((tm, d), lambda i, f: (i, 0)),
        scratch_shapes=[pltpu.VMEM((tm, d), BF16), pltpu.VMEM((16, d), BF16)],
        compiler_params=pltpu.CompilerParams(dimension_semantics=("parallel", "arbitrary"),
                                             vmem_limit_bytes=BIG_VMEM_LIMIT),
        name="ffn",
    )(x, x, x, gs, sh, w_up, w_up, conv_w, conv_b.reshape(1, fh), w_dn, g)


def _rmsnorm_kernel(x_ref, g_ref, o_ref):
    x = x_ref[...]
    ms = jnp.mean(x * x, axis=-1, keepdims=True)
    o_ref[...] = x * lax.rsqrt(ms + EPS) * g_ref[...]


def final_rmsnorm(x, g):
    m, d = x.shape
    tm = _pick_tile(m, 1024, 8)
    return pl.pallas_call(
        _rmsnorm_kernel,
        out_shape=jax.ShapeDtypeStruct((m, d), F32),
        grid=(m // tm,),
        in_specs=[pl.BlockSpec((tm, d), lambda i: (i, 0)), pl.BlockSpec((1, d), lambda i: (0, 0))],
        out_specs=pl.BlockSpec((tm, d), lambda i: (i, 0)),
        compiler_params=_cparams("parallel"),
        name="final_rmsnorm",
    )(x, g.reshape(1, d))


S5_CHUNK = 256


def _const_spec(shape):
    zeros = (0,) * len(shape)
    return pl.BlockSpec(shape, lambda b, k: zeros, pipeline_mode=pl.Buffered(1))


def _s5_kernel(*refs, ncc, reverse, finish):
    if finish:
        (uc_ref, ux_ref, bblk_ref, cblk_ref, enr_ref, eni_ref, epr_ref, epi_ref, ac_ref,
         pc_ref, px_ref, d_ref, gw_ref, gb_ref, yc_ref, yx_ref, h_ref) = refs
    else:
        (uc_ref, ux_ref, bblk_ref, cblk_ref, enr_ref, eni_ref, epr_ref, epi_ref, ac_ref,
         yc_ref, yx_ref, h_ref) = refs
    k = pl.program_id(1)
    t = uc_ref.shape[1]
    gn = enr_ref.shape[1]

    @pl.when(k == 0)
    def _():
        h_ref[...] = jnp.zeros_like(h_ref)

    is_ctx = k < ncc
    u = jnp.where(is_ctx, uc_ref[0], ux_ref[0])
    row = lax.broadcasted_iota(jnp.int32, (t, t), 0)
    col = lax.broadcasted_iota(jnp.int32, (t, t), 1)
    tri = jnp.where((col >= row) if reverse else (col <= row), 1.0, 0.0).astype(BF16)
    last = 0 if reverse else t - 1
    nblk, wb, sb2 = bblk_ref.shape
    sb = sb2 // 2
    ys = []
    for j in range(nblk):
        sc = slice(j * sb, (j + 1) * sb)
        bu = jnp.dot(u[:, j * wb:(j + 1) * wb].astype(BF16), bblk_ref[j], preferred_element_type=F32)
        br, bi = bu[:, :sb], bu[:, sb:]
        enr, eni = enr_ref[:, sc], eni_ref[:, sc]
        z = jnp.concatenate([br * enr - bi * eni, br * eni + bi * enr], axis=1).astype(BF16)
        cs = jnp.dot(tri, z, preferred_element_type=F32)
        hr, hi = h_ref[0:1, sc], h_ref[1:2, sc]
        acr, aci = ac_ref[0:1, sc], ac_ref[1:2, sc]
        sr = cs[:, :sb] + (hr * acr - hi * aci)
        si = cs[:, sb:] + (hr * aci + hi * acr)
        epr, epi = epr_ref[:, sc], epi_ref[:, sc]
        xr = sr * epr - si * epi
        xi = sr * epi + si * epr
        h_ref[0:1, sc] = xr[last:last + 1, :]
        h_ref[1:2, sc] = xi[last:last + 1, :]
        xs = jnp.concatenate([xr, xi], axis=1).astype(BF16)
        ys.append(jnp.dot(xs, cblk_ref[j], preferred_element_type=F32))
    y = ys[0] if nblk == 1 else jnp.concatenate(ys, axis=1)
    if finish:
        y = y + jnp.where(is_ctx, pc_ref[0], px_ref[0]) + d_ref[...] * u
        y = jax.nn.gelu(y)
        gate = jnp.dot(y.astype(BF16), gw_ref[...], preferred_element_type=F32) + gb_ref[...]
        y = y * jax.nn.sigmoid(gate)

    @pl.when(is_ctx)
    def _():
        yc_ref[0] = y.astype(yc_ref.dtype)

    @pl.when(jnp.logical_not(is_ctx))
    def _():
        yx_ref[0] = y.astype(yx_ref.dtype)


def _s5_tables(a_re, a_im, log_step, b_re, b_im, c_re, c_im, t, reverse):
    g, n, h = b_re.shape
    gpb = min(g, max(1, LANE // h))
    nblk = g // gpb
    eye = jnp.eye(gpb, dtype=F32)
    dt = jnp.exp(log_step)[:, None]
    ldr, ldi = a_re * dt, a_im * dt
    mag = jnp.exp(ldr)
    abr, abi = mag * jnp.cos(ldi), mag * jnp.sin(ldi)
    den = a_re * a_re + a_im * a_im
    nr, ni = abr - 1.0, abi
    fr = (nr * a_re + ni * a_im) / den
    fi = (ni * a_re - nr * a_im) / den
    bbr = fr[..., None] * b_re - fi[..., None] * b_im
    bbi = fr[..., None] * b_im + fi[..., None] * b_re
    blk_b = lambda m: jnp.einsum('jgnh,gk->jghkn', m.reshape(nblk, gpb, n, h), eye).reshape(nblk, gpb * h, gpb * n)
    bblk = jnp.concatenate([blk_b(bbr), blk_b(bbi)], axis=2).astype(BF16)
    blk_c = lambda m: jnp.einsum('jghn,gk->jgnkh', m.reshape(nblk, gpb, h, n), eye).reshape(nblk, gpb * n, gpb * h)
    cblk = jnp.concatenate([blk_c(c_re), -blk_c(c_im)], axis=1).astype(BF16)
    centre = float(t // 2)
    pos = jnp.arange(t, dtype=F32)[:, None]
    steps = ((t - pos) if reverse else (pos + 1.0)) - centre
    lr, li = ldr.reshape(1, g * n), ldi.reshape(1, g * n)
    er, ei = steps * lr, steps * li
    epr, epi = jnp.exp(er) * jnp.cos(ei), jnp.exp(er) * jnp.sin(ei)
    enr, eni = jnp.exp(-er) * jnp.cos(ei), -jnp.exp(-er) * jnp.sin(ei)
    ac = jnp.concatenate([jnp.exp(centre * lr) * jnp.cos(centre * li),
                          jnp.exp(centre * lr) * jnp.sin(centre * li)], axis=0)
    return bblk, cblk, (enr, eni, epr, epi, ac)


def _s5_pass(pc, px, col, width, tabs, reverse, fin=None, out_dtype=F32):
    bsz, nctx = pc.shape[:2]
    seq = px.shape[1]
    t = S5_CHUNK
    ncc, ncx = nctx // t, seq // t
    bblk, cblk, (enr, eni, epr, epi, ac) = tabs
    gn = enr.shape[1]
    if reverse:
        cidx = lambda k: jnp.maximum(ncc - 1 - k, 0)
        xidx = lambda k: jnp.minimum(ncx - 1 - (k - ncc), ncx - 1)
    else:
        cidx = lambda k: jnp.minimum(k, ncc - 1)
        xidx = lambda k: jnp.maximum(k - ncc, 0)
    in_specs = [pl.BlockSpec((1, t, width), lambda b, k: (b, cidx(k), col)),
                pl.BlockSpec((1, t, width), lambda b, k: (b, xidx(k), col)),
                _const_spec(bblk.shape), _const_spec(cblk.shape),
                _const_spec(enr.shape), _const_spec(eni.shape), _const_spec(epr.shape), _const_spec(epi.shape),
                _const_spec(ac.shape)]
    args = [pc, px, bblk, cblk, enr, eni, epr, epi, ac]
    if fin is not None:
        prev_c, prev_x, dvec, gw, gb = fin
        in_specs += [pl.BlockSpec((1, t, width), lambda b, k: (b, cidx(k), 0)),
                     pl.BlockSpec((1, t, width), lambda b, k: (b, xidx(k), 0)),
                     _const_spec((1, width)), _const_spec(gw.shape), _const_spec((1, width))]
        args += [prev_c, prev_x, dvec.reshape(1, width), gw, gb.reshape(1, width)]
    return pl.pallas_call(
        functools.partial(_s5_kernel, ncc=ncc, reverse=reverse, finish=fin is not None),
        out_shape=(jax.ShapeDtypeStruct((bsz, nctx, width), out_dtype),
                   jax.ShapeDtypeStruct((bsz, seq, width), out_dtype)),
        grid=(bsz, ncc + ncx),
        in_specs=in_specs,
        out_specs=(pl.BlockSpec((1, t, width), lambda b, k: (b, cidx(k), 0)),
                   pl.BlockSpec((1, t, width), lambda b, k: (b, xidx(k), 0))),
        scratch_shapes=[pltpu.VMEM((2, gn), F32)],
        compiler_params=_cparams("parallel", "arbitrary"),
        name="s5_rev" if reverse else "s5_fwd",
    )(*args)


def s5_mixer(pc, px, width, p):
    tabs = [_s5_tables(p['s5_a_re'][d], p['s5_a_im'][d], p['s5_log_step'][d], p['s5_b_re'][d], p['s5_b_im'][d],
                       p['s5_c_re'][d], p['s5_c_im'][d], S5_CHUNK, d == 1) for d in range(2)]
    bc, bx = _s5_pass(pc, px, 0, width, tabs[1], reverse=True)
    return _s5_pass(pc, px, 0, width, tabs[0], reverse=False,
                    fin=(bc, bx, p['s5_d'], p['s5_glu_w'].astype(BF16), p['s5_glu_b']), out_dtype=BF16)


GLA_STEP = 256
GLA_WBLK = 8


def _gla_kernel(*refs, nh, dk, dv, ncs, nxs, cps, reverse, finish, scale):
    if finish:
        (kc_ref, qc_ref, vc_ref, lc_ref, kx_ref, qx_ref, vx_ref, lx_ref, wg_ref, bg_ref, pc_ref, px_ref, ng_ref,
         rc_ref, rx_ref, oc_ref, ox_ref, s_ref, o_scr) = refs
    else:
        (kc_ref, qc_ref, vc_ref, lc_ref, kx_ref, qx_ref, vx_ref, lx_ref, wg_ref, bg_ref,
         oc_ref, ox_ref, s_ref, o_scr) = refs
    ts = kc_ref.shape[1]
    nchunk = ts // GLA_CHUNK
    step = pl.program_id(1)
    is_ctx = step < ncs
    xs = jnp.clip(nxs - 1 - (step - ncs), 0, nxs - 1) if reverse else jnp.maximum(step - ncs, 0)
    col0 = cps * (xs % (GLA_WBLK // cps))

    def latent(x_ref):
        return jnp.concatenate([x_ref[0, :, col0 + j, :] for j in range(cps)], axis=0)

    pick = lambda c_ref, x_ref: jnp.where(is_ctx, c_ref[0], latent(x_ref))

    @pl.when(pl.program_id(1) == 0)
    def _():
        s_ref[...] = jnp.zeros_like(s_ref)

    hp = lax.Precision.HIGHEST
    rank2 = wg_ref.shape[0]
    pre = jnp.dot(pick(lc_ref, lx_ref)[:, :rank2], wg_ref[...], preferred_element_type=F32,
                  precision=hp) + bg_ref[...]
    g = jax.nn.log_sigmoid(pre) * (1.0 / GLA_GATE_TEMP)
    ci = lax.broadcasted_iota(jnp.int32, (GLA_CHUNK, GLA_CHUNK), 0)
    cj = lax.broadcasted_iota(jnp.int32, (GLA_CHUNK, GLA_CHUNK), 1)
    keep = (cj >= ci) if reverse else (cj <= ci)
    cum = jnp.where(keep, 1.0, 0.0)
    b = jnp.concatenate([jnp.dot(cum, g[c * GLA_CHUNK:(c + 1) * GLA_CHUNK], preferred_element_type=F32, precision=hp)
                         for c in range(nchunk)], axis=0)
    eb = jnp.exp(b)
    enb = jnp.exp(-b)
    q_in = pick(qc_ref, qx_ref) * scale * eb
    k_out = pick(kc_ref, kx_ref) * enb
    v = pick(vc_ref, vx_ref)
    if finish:
        prev = jnp.where(is_ctx, pc_ref[0], px_ref[0])
        r = pick(rc_ref, rx_ref)
        out_gate = r * jax.nn.sigmoid(r)
    nt = (((1,), (1,)), ((), ()))
    tn = (((0,), (0,)), ((), ()))
    for c in (range(nchunk - 1, -1, -1) if reverse else range(nchunk)):
        r0 = c * GLA_CHUNK
        rows = slice(r0, r0 + GLA_CHUNK)
        end = r0 if reverse else r0 + GLA_CHUNK - 1
        etot = jnp.exp(b[end:end + 1, :])
        k_kv = k_out[rows] * etot
        for h in range(nh):
            kc = slice(h * dk, (h + 1) * dk)
            vc = slice(h * dv, (h + 1) * dv)
            qh = q_in[rows, kc].astype(BF16)
            kh = k_out[rows, kc].astype(BF16)
            kkv = k_kv[:, kc].astype(BF16)
            vh = v[rows, vc].astype(BF16)
            st = s_ref[h]
            sc = lax.dot_general(qh, kh, nt, preferred_element_type=F32)
            sc = jnp.where(keep, sc, 0.0).astype(BF16)
            o = (jnp.dot(sc, vh, preferred_element_type=F32)
                 + lax.dot_general(qh, st.astype(BF16), nt, preferred_element_type=F32))
            kvt = lax.dot_general(vh, kkv, tn, preferred_element_type=F32)
            s_ref[h] = st * etot[:, kc] + kvt
            if finish:
                o = o + prev[rows, vc]
                ms = jnp.mean(o * o, axis=-1, keepdims=True)
                o = o * lax.rsqrt(ms + EPS) * ng_ref[...] * out_gate[rows, vc]
            o_scr[rows, vc] = o

    @pl.when(is_ctx)
    def _():
        oc_ref[0] = o_scr[...].astype(oc_ref.dtype)

    @pl.when(jnp.logical_not(is_ctx))
    def _():
        ox_ref[0] = o_scr[...].astype(ox_ref.dtype)


def _gla_pass(pc, gx, c0, wgp, bg, nh, dk, dv, gpad, reverse, fin=None):
    bsz, nctx = pc.shape[:2]
    rows, wcols = gx.shape[1:3]
    seq = rows * wcols
    ts = GLA_STEP
    assert ts % rows == 0 and GLA_WBLK % (ts // rows) == 0 and wcols % GLA_WBLK == 0
    cps = ts // rows
    spb = GLA_WBLK // cps
    ncs, nxs = nctx // ts, seq // ts
    key, val = nh * dk, nh * dv
    if reverse:
        cidx = lambda k: jnp.maximum(ncs - 1 - k, 0)
        xidx = lambda k: jnp.minimum(nxs - 1 - (k - ncs), nxs - 1)
    else:
        cidx = lambda k: jnp.minimum(k, ncs - 1)
        xidx = lambda k: jnp.maximum(k - ncs, 0)

    def group(idx, base):
        return [pl.BlockSpec((1, ts, key), lambda b, k: (b, idx(k), base // key)),
                pl.BlockSpec((1, ts, key), lambda b, k: (b, idx(k), base // key + 1)),
                pl.BlockSpec((1, ts, val), lambda b, k: (b, idx(k), (base + 2 * key) // val)),
                pl.BlockSpec((1, ts, gpad), lambda b, k: (b, idx(k), (base + 2 * key + val) // gpad))]

    def lat(width, col):
        return pl.BlockSpec((1, rows, GLA_WBLK, width), lambda b, k: (b, 0, xidx(k) // spb, col))

    lat_group = [lat(key, c0 // key), lat(key, c0 // key + 1), lat(val, (c0 + 2 * key) // val),
                 lat(gpad, (c0 + 2 * key + val) // gpad)]
    out_dtype = F32 if fin is None else BF16
    in_specs = group(cidx, c0) + lat_group + [_const_spec(wgp.shape), _const_spec((1, key))]
    args = [pc] * 4 + [gx] * 4 + [wgp, bg.reshape(1, key)]
    if fin is not None:
        prev_c, prev_x, ng = fin
        in_specs += [pl.BlockSpec((1, ts, val), lambda b, k: (b, cidx(k), 0)),
                     pl.BlockSpec((1, ts, val), lambda b, k: (b, xidx(k), 0)), _const_spec((1, dv)),
                     pl.BlockSpec((1, ts, val), lambda b, k: (b, cidx(k), (c0 - val) // val)),
                     lat(val, (c0 - val) // val)]
        args += [prev_c, prev_x, ng.reshape(1, dv), pc, gx]
    return pl.pallas_call(
        functools.partial(_gla_kernel, nh=nh, dk=dk, dv=dv, ncs=ncs, nxs=nxs, cps=cps, reverse=reverse,
                          finish=fin is not None, scale=dk ** -0.5),
        out_shape=(jax.ShapeDtypeStruct((bsz, nctx, val), out_dtype),
                   jax.ShapeDtypeStruct((bsz, seq, val), out_dtype)),
        grid=(bsz, ncs + nxs),
        in_specs=in_specs,
        out_specs=(pl.BlockSpec((1, ts, val), lambda b, k: (b, cidx(k), 0)),
                   pl.BlockSpec((1, ts, val), lambda b, k: (b, xidx(k), 0))),
        scratch_shapes=[pltpu.VMEM((nh, dv, dk), F32), pltpu.VMEM((ts, val), F32)],
        compiler_params=_cparams("parallel", "arbitrary"),
        name="gla_rev" if reverse else "gla_fwd",
    )(*args)


def gla_mixer(pc, px, c0, q4, gpad, p):
    dv = p['gla_norm_g'].shape[0]
    nh = (2 * q4) // dv
    dk = q4 // nh
    rank = p['gla_wg'].shape[1]
    bsz, seq, width = px.shape
    gx = px.reshape(bsz, seq // GRID_W, GRID_W, width)
    wgp = [jnp.zeros((2 * rank, q4), F32).at[d * rank:(d + 1) * rank].set(p['gla_wg'][d]) for d in range(2)]
    kw = dict(nh=nh, dk=dk, dv=dv, gpad=gpad)
    rc, rx = _gla_pass(pc, gx, c0, wgp[1], p['gla_bg'][1], reverse=True, **kw)
    oc, ox = _gla_pass(pc, gx, c0, wgp[0], p['gla_bg'][0], reverse=False, fin=(rc, rx, p['gla_norm_g']), **kw)
    return oc, _from_col_major(ox)


HY_N2 = 128
HY_PAD = 8


def _hy_dims(n):
    n1 = 2 * n // HY_N2
    k1 = n1 // 2 + 1
    k1p = -(-k1 // 8) * 8
    return n1, k1, k1p


def _hy_tables(n):
    n1, k1, k1p = _hy_dims(n)
    big = 2 * n
    kk = jnp.arange(k1p, dtype=jnp.int32)[:, None]
    mm = jnp.arange(n1, dtype=jnp.int32)[None, :]
    ang = (2.0 * math.pi / n1) * ((kk * mm) % n1).astype(F32)
    valid = (kk < k1).astype(F32)
    f1 = jnp.concatenate([jnp.cos(ang) * valid, -jnp.sin(ang) * valid], axis=0)
    wk = jnp.where((kk == 0) | (kk == n1 // 2), 1.0, 2.0) * valid / big
    f1inv = jnp.concatenate([(jnp.cos(ang) * wk).T, (-jnp.sin(ang) * wk).T], axis=1)
    k1i = jnp.arange(k1, dtype=jnp.int32)[:, None, None]
    k2i = jnp.arange(HY_N2, dtype=jnp.int32)[None, :, None]
    n2i = jnp.arange(HY_N2, dtype=jnp.int32)[None, None, :]
    ph = (2.0 * math.pi / big) * ((n2i * (k1i + n1 * k2i)) % big).astype(F32)
    gr, gi = jnp.cos(ph), -jnp.sin(ph)
    gs = jnp.concatenate([gr, gi], axis=1)
    gts = jnp.concatenate([gr.swapaxes(1, 2), gi.swapaxes(1, 2)], axis=1)
    return f1.astype(BF16), f1inv.astype(BF16), gs.astype(BF16), gts.astype(BF16)


def _hy_stage1(src_ref, f1, a_r, a_i, nslab, k1p):
    pitch = HY_N2 + HY_PAD

    def body(i, carry):
        n2 = 2 * i
        rows = jnp.concatenate([src_ref[pl.ds(n2, nslab, stride=pitch), :],
                                src_ref[pl.ds(n2 + 1, nslab, stride=pitch), :]], axis=1)
        out = jnp.dot(f1, rows.astype(BF16), preferred_element_type=F32)
        base = pl.multiple_of(n2 * k1p, 8)
        a_r[pl.ds(base, k1p), :] = out[:k1p, :LANE]
        a_i[pl.ds(base, k1p), :] = out[k1p:, :LANE]
        base1 = pl.multiple_of(base + k1p, 8)
        a_r[pl.ds(base1, k1p), :] = out[:k1p, LANE:]
        a_i[pl.ds(base1, k1p), :] = out[k1p:, LANE:]
        return carry

    lax.fori_loop(0, HY_N2 // 2, body, 0, unroll=8)


def _hy_stage2(a_r, a_i, gs_ref, k, k1p):
    ar = a_r[pl.ds(k, HY_N2, stride=k1p), :]
    ai = a_i[pl.ds(k, HY_N2, stride=k1p), :]
    rhs = jnp.concatenate([ar, ai], axis=1).astype(BF16)
    out = jnp.dot(gs_ref[k], rhs, preferred_element_type=F32)
    h = HY_N2
    return out[:h, :LANE] - out[h:, LANE:], out[:h, LANE:] + out[h:, :LANE]


def _hyena_conv_kernel(y_ref, g_ref, wy_ref, by_ref, wg_ref, bg_ref, bias_ref, hr_ref, hi_ref,
                       f1_ref, f1inv_ref, gs_ref, gts_ref, o_ref, ypad, zbuf, a_r, a_i, *, conv_y):
    n = y_ref.shape[1]
    nslab = n // HY_N2
    pitch = HY_N2 + HY_PAD
    k1 = gs_ref.shape[0]
    k1p = f1_ref.shape[0] // 2
    h = HY_N2

    def slab_conv(ref, i, w_ref, b_ref):
        r0 = pl.multiple_of(i * HY_N2, HY_N2)
        prev8 = ref[0, pl.ds(pl.multiple_of(jnp.maximum(r0 - 8, 0), 8), 8), :]
        next8 = ref[0, pl.ds(pl.multiple_of(jnp.minimum(r0 + HY_N2, n - 8), 8), 8), :]
        return _dwconv3_tile(ref[0, pl.ds(r0, HY_N2), :], prev8, next8, i == 0, i == nslab - 1, w_ref, b_ref)

    def y_slab(i):
        if conv_y:
            return slab_conv(y_ref, i, wy_ref, by_ref)
        return y_ref[0, pl.ds(pl.multiple_of(i * HY_N2, HY_N2), HY_N2), :]

    def fill(i, carry):
        ypad[pl.ds(pl.multiple_of(i * pitch, 8), HY_N2), :] = y_slab(i)
        return carry

    lax.fori_loop(0, nslab, fill, 0, unroll=2)
    _hy_stage1(ypad, f1_ref[...], a_r, a_i, nslab, k1p)

    def freq_fwd(k, carry):
        xr, xi = _hy_stage2(a_r, a_i, gs_ref, k, k1p)
        hr, hi = hr_ref[0, k], hi_ref[0, k]
        zbuf[k] = jnp.concatenate([xr * hr - xi * hi, xr * hi + xi * hr], axis=1).astype(BF16)
        return carry

    lax.fori_loop(0, k1, freq_fwd, 0, unroll=4)

    def freq_inv(k, carry):
        out = jnp.dot(gts_ref[k], zbuf[k], preferred_element_type=F32)
        a_r[pl.ds(k, HY_N2, stride=k1p), :] = out[:h, :LANE] + out[h:, LANE:]
        a_i[pl.ds(k, HY_N2, stride=k1p), :] = out[:h, LANE:] - out[h:, :LANE]
        return carry

    lax.fori_loop(0, k1, freq_inv, 0, unroll=4)
    f1inv = f1inv_ref[...][:nslab]

    def inv1(i, carry):
        n2 = 2 * i
        b0 = pl.multiple_of(n2 * k1p, 8)
        b1 = pl.multiple_of(b0 + k1p, 8)
        rhs = jnp.concatenate(
            [jnp.concatenate([a_r[pl.ds(b0, k1p), :], a_i[pl.ds(b0, k1p), :]], axis=0),
             jnp.concatenate([a_r[pl.ds(b1, k1p), :], a_i[pl.ds(b1, k1p), :]], axis=0)], axis=1).astype(BF16)
        out = jnp.dot(f1inv, rhs, preferred_element_type=F32)
        ypad[pl.ds(n2, nslab, stride=pitch), :] = out[:, :LANE]
        ypad[pl.ds(n2 + 1, nslab, stride=pitch), :] = out[:, LANE:]
        return carry

    lax.fori_loop(0, HY_N2 // 2, inv1, 0, unroll=8)

    def finish(i, carry):
        p0 = pl.multiple_of(i * pitch, 8)
        gate = slab_conv(g_ref, i, wg_ref, bg_ref)
        res = gate * (ypad[pl.ds(p0, HY_N2), :] + y_slab(i) * bias_ref[0])
        o_ref[0, pl.ds(pl.multiple_of(i * HY_N2, HY_N2), HY_N2), :] = res.astype(o_ref.dtype)
        return carry

    lax.fori_loop(0, nslab, finish, 0, unroll=2)


def _hyena_spectrum_kernel(f_ref, f1_ref, gs_ref, hr_ref, hi_ref, a_r, a_i, fpad):
    big = f_ref.shape[1]
    nslab = big // HY_N2
    pitch = HY_N2 + HY_PAD
    k1 = gs_ref.shape[0]
    k1p = f1_ref.shape[0] // 2

    def fill(i, carry):
        fpad[pl.ds(pl.multiple_of(i * pitch, 8), HY_N2), :] = f_ref[0, pl.ds(pl.multiple_of(i * HY_N2, HY_N2), HY_N2), :]
        return carry

    lax.fori_loop(0, nslab, fill, 0, unroll=2)
    _hy_stage1(fpad, f1_ref[...], a_r, a_i, nslab, k1p)

    def freq(k, carry):
        xr, xi = _hy_stage2(a_r, a_i, gs_ref, k, k1p)
        hr_ref[0, k] = xr
        hi_ref[0, k] = xi
        return carry

    lax.fori_loop(0, k1, freq, 0, unroll=4)


def _one(shape, index_map):
    return pl.BlockSpec(shape, index_map, pipeline_mode=pl.Buffered(1))


def hyena_spectrum(filt, tabs):
    f1, _, gs, _ = tabs
    r, big, w = filt.shape
    ns = w // LANE
    n1, k1, k1p = _hy_dims(big // 2)
    shp = jax.ShapeDtypeStruct((r * ns, k1, HY_N2, LANE), F32)
    spec_o = pl.BlockSpec((1, k1, HY_N2, LANE), lambda i, j: (i * ns + j, 0, 0, 0))
    return pl.pallas_call(
        _hyena_spectrum_kernel,
        out_shape=(shp, shp),
        grid=(r, ns),
        in_specs=[_one((1, big, LANE), lambda i, j: (i, 0, j)),
                  _one(f1.shape, lambda i, j: (0, 0)),
                  _one(gs.shape, lambda i, j: (0, 0, 0))],
        out_specs=(spec_o, spec_o),
        scratch_shapes=[pltpu.VMEM((HY_N2 * k1p, LANE), F32), pltpu.VMEM((HY_N2 * k1p, LANE), F32),
                        pltpu.VMEM((n1 * (HY_N2 + HY_PAD), LANE), F32)],
        compiler_params=_cparams("parallel", "parallel"),
        name="hyena_spectrum",
    )(filt, f1, gs)


def hyena_order(y, ycol, g, gcol, conv_w, conv_b, cy, cg, bias, hr, hi, order, tabs, conv_y, out_dtype):
    f1, f1inv, gs, gts = tabs
    bsz, n = y.shape[:2]
    w = bias.shape[-1]
    ns = w // LANE
    n1, k1, k1p = _hy_dims(n)
    f1d = f1[:, :n1 // 2]
    pitch = HY_N2 + HY_PAD
    nslab = n // HY_N2
    cw = lambda c: _one((3, LANE), lambda j, b: (0, c + j))
    cb = lambda c: _one((1, LANE), lambda j, b: (0, c + j))
    hspec = _one((1, k1, HY_N2, LANE), lambda j, b: (order * ns + j, 0, 0, 0))
    return pl.pallas_call(
        functools.partial(_hyena_conv_kernel, conv_y=conv_y),
        out_shape=jax.ShapeDtypeStruct((bsz, n, w), out_dtype),
        grid=(ns, bsz),
        in_specs=[_one((1, n, LANE), lambda j, b: (b, 0, ycol + j)),
                  _one((1, n, LANE), lambda j, b: (b, 0, gcol + j)),
                  cw(cy), cb(cy), cw(cg), cb(cg),
                  _one((1, 1, LANE), lambda j, b: (order, 0, j)),
                  hspec, hspec,
                  _one(f1d.shape, lambda j, b: (0, 0)), _one(f1inv.shape, lambda j, b: (0, 0)),
                  _one(gs.shape, lambda j, b: (0, 0, 0)), _one(gts.shape, lambda j, b: (0, 0, 0))],
        out_specs=pl.BlockSpec((1, n, LANE), lambda j, b: (b, 0, j)),
        scratch_shapes=[pltpu.VMEM((nslab * pitch, LANE), F32), pltpu.VMEM((k1, HY_N2, 2 * LANE), BF16),
                        pltpu.VMEM((HY_N2 * k1p, LANE), F32), pltpu.VMEM((HY_N2 * k1p, LANE), F32)],
        compiler_params=pltpu.CompilerParams(dimension_semantics=("parallel", "parallel"),
                                             vmem_limit_bytes=BIG_VMEM_LIMIT),
        name="hyena_order",
    )(y, g, conv_w, conv_b, conv_w, conv_b, bias.reshape(bias.shape[0], 1, w), hr, hi, f1d, f1inv, gs, gts)


def hyena_latent(px, c0, filt, p):
    w = p['hy_bias'].shape[-1]
    ns = w // LANE
    n = px.shape[1]
    tabs = _hy_tables(n)
    hr, hi = hyena_spectrum(filt, tabs)
    cw, cb = p['hy_conv_w'], p['hy_conv_b'].reshape(1, -1)
    b0 = c0 // LANE
    y1 = hyena_order(px, b0, px, b0 + ns, cw, cb, 0, ns, p['hy_bias'], hr, hi, 0, tabs, True, F32)
    return hyena_order(y1, 0, px, b0 + 2 * ns, cw, cb, 0, 2 * ns, p['hy_bias'], hr, hi, 1, tabs, False, BF16)


def _hyena_ctx_kernel(zv_ref, z1_ref, z2_ref, wv_ref, bv_ref, w1_ref, b1_ref, w2_ref, b2_ref, bias_ref,
                      filt_ref, ff_ref, finv_ref, o_ref):
    n = zv_ref.shape[1]
    hp = lax.Precision.HIGHEST
    zero8 = jnp.zeros((8, LANE), F32)
    conv = lambda ref, w, b: _dwconv3_tile(ref[0], zero8, zero8, True, True, w, b)
    ff = ff_ref[...]
    kp = ff.shape[0] // 2
    y = conv(zv_ref, wv_ref, bv_ref)
    for o, (g_ref, w, b) in enumerate(((z1_ref, w1_ref, b1_ref), (z2_ref, w2_ref, b2_ref))):
        hsp = jnp.dot(ff, filt_ref[o], preferred_element_type=F32, precision=hp)
        ysp = jnp.dot(ff[:, :n], y, preferred_element_type=F32, precision=hp)
        hr, hi, yr, yi = hsp[:kp], hsp[kp:], ysp[:kp], ysp[kp:]
        z = jnp.concatenate([yr * hr - yi * hi, yr * hi + yi * hr], axis=0)
        cv = jnp.dot(finv_ref[...], z, preferred_element_type=F32, precision=hp)
        y = conv(g_ref, w, b) * (cv + y * bias_ref[o])
    o_ref[0] = y.astype(o_ref.dtype)


def hyena_context(pc, c0, filt, p):
    bsz, n = pc.shape[:2]
    w = p['hy_bias'].shape[-1]
    ns = w // LANE
    kp = -(-(n + 1) // 8) * 8
    kk = jnp.arange(kp, dtype=jnp.int32)[:, None]
    mm = jnp.arange(2 * n, dtype=jnp.int32)[None, :]
    ang = (math.pi / n) * ((kk * mm) % (2 * n)).astype(F32)
    valid = (kk <= n).astype(F32)
    ff = jnp.concatenate([jnp.cos(ang) * valid, -jnp.sin(ang) * valid], axis=0)
    wk = jnp.where((kk == 0) | (kk == n), 1.0, 2.0) * valid / (2 * n)
    finv = jnp.concatenate([(jnp.cos(ang) * wk).T[:n], (-jnp.sin(ang) * wk).T[:n]], axis=1)
    b0 = c0 // LANE
    zs = lambda c: pl.BlockSpec((1, n, LANE), lambda j, b: (b, 0, b0 + c + j))
    cw = lambda c: pl.BlockSpec((3, LANE), lambda j, b: (0, c + j))
    cb = lambda c: pl.BlockSpec((1, LANE), lambda j, b: (0, c + j))
    conv_w, conv_b = p['hy_conv_w'], p['hy_conv_b'].reshape(1, -1)
    return pl.pallas_call(
        _hyena_ctx_kernel,
        out_shape=jax.ShapeDtypeStruct((bsz, n, w), BF16),
        grid=(ns, bsz),
        in_specs=[zs(0), zs(ns), zs(2 * ns), cw(0), cb(0), cw(ns), cb(ns), cw(2 * ns), cb(2 * ns),
                  pl.BlockSpec((2, 1, LANE), lambda j, b: (0, 0, j)),
                  pl.BlockSpec((2, 2 * n, LANE), lambda j, b: (0, 0, j)),
                  pl.BlockSpec(ff.shape, lambda j, b: (0, 0)), pl.BlockSpec(finv.shape, lambda j, b: (0, 0))],
        out_specs=pl.BlockSpec((1, n, LANE), lambda j, b: (b, 0, j)),
        compiler_params=_cparams("parallel", "parallel"),
        name="hyena_context",
    )(pc, pc, pc, conv_w, conv_b, conv_w, conv_b, conv_w, conv_b, p['hy_bias'].reshape(2, 1, w), filt, ff, finv)


def _from_col_major(t):
    bsz, n = t.shape[:2]
    rows = n // GRID_W
    return t.reshape(bsz, GRID_W, rows, *t.shape[2:]).swapaxes(1, 2).reshape(bsz, n, *t.shape[2:])


def _filter_positions(n):
    r = jnp.arange(2 * n, dtype=jnp.int32)
    return jnp.where(r < n, r, 2 * n - r).astype(F32)


def _filter_feats(n):
    pos = _filter_positions(n)
    t = pos / max(n - 1, 1)
    freqs = jnp.linspace(1e-4, HY_POS_FREQS - 1, HY_POS_FREQS, dtype=F32)
    ang = (2.0 * math.pi / n) * pos[:, None] * freqs[None]
    feats = jnp.concatenate([t[:, None], jnp.cos(ang), -jnp.sin(ang)], axis=-1)
    return jnp.pad(feats, ((0, 0), (0, LANE - feats.shape[1])))


def _filter_hidden_kernel(f_ref, w1_ref, b1_ref, s1_ref, w2_ref, b2_ref, s2_ref, o_ref):
    hp = lax.Precision.HIGHEST
    h = jnp.sin(s1_ref[...] * (jnp.dot(f_ref[...], w1_ref[...], preferred_element_type=F32, precision=hp)
                               + b1_ref[...]))
    o_ref[...] = jnp.sin(s2_ref[...] * (jnp.dot(h, w2_ref[...], preferred_element_type=F32, precision=hp)
                                        + b2_ref[...]))


def _filter_out_kernel(h_ref, wf_ref, wb_ref, bf_ref, bb_ref, rate_ref, o_ref, *, n, chunk):
    nchunk = 2 * n // chunk
    inv_span = 1.0 / max(n - 1, 1)

    def emit(c, acc):
        r0 = pl.multiple_of(c * chunk, chunk)
        past = c < nchunk // 2
        w = jnp.where(past, wf_ref[...], wb_ref[...])
        bias = jnp.where(past, bf_ref[...], bb_ref[...])
        hr0 = pl.multiple_of((c % (nchunk // 2)) * chunk, chunk)
        val = jnp.dot(h_ref[pl.ds(hr0, chunk), :], w, preferred_element_type=F32,
                      precision=lax.Precision.HIGHEST) + bias
        r = r0 + lax.broadcasted_iota(jnp.int32, (chunk, LANE), 0)
        t = jnp.where(r < n, r, 2 * n - r).astype(F32) * inv_span
        val = jnp.where(r == n, 0.0, val * jnp.exp(-t * rate_ref[...]))
        o_ref[0, pl.ds(r0, chunk), :] = val
        return acc + jnp.sum(jnp.abs(val), axis=0, keepdims=True)

    total = lax.fori_loop(0, nchunk, emit, jnp.zeros((1, LANE), F32))
    scale = 1.0 / (total + EPS)

    def rescale(c, carry):
        r0 = pl.multiple_of(c * chunk, chunk)
        o_ref[0, pl.ds(r0, chunk), :] = o_ref[0, pl.ds(r0, chunk), :] * scale
        return carry

    lax.fori_loop(0, nchunk, rescale, 0)


def hyena_filters(n, p):
    order, width = p['hy_bias'].shape
    ns = width // LANE
    hid = p['hy_f_w2'].shape[0]
    half = LANE // 2
    assert hid <= half
    feats = _filter_feats(n)
    feats = jnp.concatenate([feats[:n], feats[n:]], axis=1)
    zpad = lambda a, rows, cols: jnp.pad(a, ((0, rows - a.shape[0]), (0, cols - a.shape[1])))
    diag2 = lambda a: jnp.concatenate([jnp.pad(a, ((0, 0), (0, a.shape[1]))),
                                       jnp.pad(a, ((0, 0), (a.shape[1], 0)))], axis=0)
    twice = lambda a: jnp.tile(zpad(a.reshape(1, -1), 1, half), (1, 2))
    w1 = diag2(zpad(p['hy_f_w1'], LANE, half))
    w2 = diag2(zpad(p['hy_f_w2'], half, half))
    w3 = zpad(p['hy_f_w3'], half, p['hy_f_w3'].shape[1])
    w3_past = jnp.pad(w3, ((0, half), (0, 0)))
    w3_future = jnp.pad(w3, ((half, 0), (0, 0)))
    tr = _pick_tile(n, 2048, 8)
    full = lambda i: (0, 0)
    hidden = pl.pallas_call(
        _filter_hidden_kernel,
        out_shape=jax.ShapeDtypeStruct((n, LANE), F32),
        grid=(n // tr,),
        in_specs=[pl.BlockSpec((tr, 2 * LANE), lambda i: (i, 0)),
                  pl.BlockSpec((2 * LANE, LANE), full), pl.BlockSpec((1, LANE), full), pl.BlockSpec((1, LANE), full),
                  pl.BlockSpec((LANE, LANE), full), pl.BlockSpec((1, LANE), full), pl.BlockSpec((1, LANE), full)],
        out_specs=pl.BlockSpec((tr, LANE), lambda i: (i, 0)),
        compiler_params=_cparams("parallel"),
        name="hyena_filter_hidden",
    )(feats, w1, twice(p['hy_f_b1']), twice(p['hy_f_freq1']), w2, twice(p['hy_f_b2']), twice(p['hy_f_freq2']))
    rates = jnp.abs(jnp.linspace(math.log(HY_DECAY_TARGET) / HY_LONG_DECAY_PCT,
                                 math.log(HY_DECAY_TARGET) / HY_SHORT_DECAY_PCT, width, dtype=F32)).reshape(1, width)
    b3 = p['hy_f_b3'].reshape(1, -1)
    return pl.pallas_call(
        functools.partial(_filter_out_kernel, n=n, chunk=min(1024, n)),
        out_shape=jax.ShapeDtypeStruct((order, 2 * n, width), F32),
        grid=(order, ns),
        in_specs=[_one((n, LANE), lambda o, j: (0, 0)),
                  pl.BlockSpec((LANE, LANE), lambda o, j: (0, 2 * o * ns + j)),
                  pl.BlockSpec((LANE, LANE), lambda o, j: (0, (2 * o + 1) * ns + j)),
                  pl.BlockSpec((1, LANE), lambda o, j: (0, 2 * o * ns + j)),
                  pl.BlockSpec((1, LANE), lambda o, j: (0, (2 * o + 1) * ns + j)),
                  pl.BlockSpec((1, LANE), lambda o, j: (0, j))],
        out_specs=pl.BlockSpec((1, 2 * n, LANE), lambda o, j: (o, 0, j)),
        compiler_params=_cparams("parallel", "parallel"),
        name="hyena_filter_out",
    )(hidden, w3_past, w3_future, b3, b3, rates)


def kernel(x, c, ctx, c_ctx, w_mod, b_mod, norm1_g, norm2_g, w_in, s5_a_re, s5_a_im, s5_log_step, s5_b_re, s5_b_im, s5_c_re, s5_c_im, s5_d, s5_glu_w, s5_glu_b, hy_conv_w, hy_conv_b, hy_f_w1, hy_f_b1, hy_f_freq1, hy_f_w2, hy_f_b2, hy_f_freq2, hy_f_w3, hy_f_b3, hy_bias, gla_wg, gla_bg, gla_norm_g, w_branch, w_out, ff_w_up, ff_conv_w, ff_conv_b, ff_w_down, final_norm_g):
    bsz, seq, d = x.shape
    nctx = ctx.shape[1]
    depth = w_mod.shape[0]
    q4 = d // 4
    rank2 = 2 * gla_wg.shape[2]
    gpad = max(LANE, q4 // 2)
    assert bsz + 1 <= 8

    c_gk = q4
    c_gv = 2 * q4
    c_gg = 4 * q4
    c_gq = c_gg + rank2
    c_gr = c_gq + q4
    c_hy = c_gr + 2 * q4
    c_mg = c_hy + 3 * q4

    rows = jnp.zeros((8, d), F32).at[:bsz].set(c).at[bsz].set(c_ctx)
    mod = modulation(rows, w_mod, b_mod)

    xs = x.reshape(bsz * seq, d)
    cs = ctx.reshape(bsz * nctx, d)
    for l in range(depth):
        ctx_out = l < depth - 1
        sh1, s1, g1, sh2, s2, g2 = [mod[l, :, i * d:(i + 1) * d][:, None, :] for i in range(6)]
        gs1 = norm1_g[l] * (1.0 + s1)
        gs2 = norm2_g[l] * (1.0 + s2)
        bx = slice(0, bsz)
        bc = slice(bsz, bsz + 1)

        wl = w_in[l]
        w_pack = jnp.concatenate([
            wl[:, 0:c_gk], wl[:, c_hy:c_mg], wl[:, c_gr:c_hy], wl[:, c_gk:c_gv], wl[:, c_gq:c_gr],
            wl[:, c_gv:c_gg], wl[:, c_gg:c_gq],
            jnp.zeros((d, gpad - rank2), F32)], axis=1).astype(BF16)
        w_gate = wl[:, c_mg:].reshape(d, 3, d).swapaxes(0, 1).astype(BF16)
        wb = w_branch[l].astype(BF16)
        w_o = w_out[l].astype(BF16)
        w_up = cast_col_blocks(ff_w_up[l], _pick_tile(ff_w_down.shape[1], FFN_TF))
        w_dn = ff_w_down[l].astype(BF16)

        p = dict(s5_a_re=s5_a_re[l], s5_a_im=s5_a_im[l], s5_log_step=s5_log_step[l],
                 s5_b_re=s5_b_re[l], s5_b_im=s5_b_im[l], s5_c_re=s5_c_re[l], s5_c_im=s5_c_im[l],
                 s5_d=s5_d[l], s5_glu_w=s5_glu_w[l], s5_glu_b=s5_glu_b[l],
                 hy_conv_w=hy_conv_w[l], hy_conv_b=hy_conv_b[l], hy_f_w1=hy_f_w1[l], hy_f_b1=hy_f_b1[l],
                 hy_f_freq1=hy_f_freq1[l], hy_f_w2=hy_f_w2[l], hy_f_b2=hy_f_b2[l],
                 hy_f_freq2=hy_f_freq2[l], hy_f_w3=hy_f_w3[l], hy_f_b3=hy_f_b3[l], hy_bias=hy_bias[l],
                 gla_wg=gla_wg[l], gla_bg=gla_bg[l], gla_norm_g=gla_norm_g[l])

        px = norm_matmul(xs, gs1[bx], sh1[bx], w_pack).reshape(bsz, seq, -1)
        pc = norm_matmul(cs, gs1[bc], sh1[bc], w_pack).reshape(bsz, nctx, -1)

        ya_c, ya_x = s5_mixer(pc, px, q4, p)
        oc_c, oc_x = gla_mixer(pc, px, 6 * q4, q4, gpad, p)
        yb_x = hyena_latent(px, q4, hyena_filters(seq, p), p)

        flat = lambda t: t.reshape(-1, t.shape[-1])
        mx = merge(xs, gs1[bx], sh1[bx], flat(ya_x), flat(yb_x), flat(oc_x), w_gate, wb)
        xs = matmul_residual(mx, w_o, xs, g1[bx])
        xs = ffn(xs, bsz, gs2[bx], sh2[bx], w_up, ff_conv_w[l], ff_conv_b[l], w_dn, g2[bx])

        if ctx_out:
            yb_c = hyena_context(pc, q4, hyena_filters(nctx, p), p)
            mc = merge(cs, gs1[bc], sh1[bc], flat(ya_c), flat(yb_c), flat(oc_c), w_gate, wb)
            cs = matmul_residual(mc, w_o, cs, g1[bc])
            cs = ffn(cs, bsz, gs2[bc], sh2[bc], w_up, ff_conv_w[l], ff_conv_b[l], w_dn, g2[bc])

    return final_rmsnorm(xs, final_norm_g).reshape(bsz, seq, d)
```
